```python
import jax, jax.numpy as jnp
from jax import lax
import numpy as np

D_MODEL = 1024
BATCH = 16
SEQ = 2048
DEPTH = 1

CHUNK = 64
PLE_DIM = 256
EPS = 1e-6

A_HEADS = 8
A_HEAD_DIM = 64
A_WIDTH = A_HEADS * A_HEAD_DIM
A_LOOKBACK = 8
A_BAND = A_LOOKBACK + 1
REL_CLIP = 128
N_REL = 2 * REL_CLIP + 1

B_HEADS = 4
B_KEY_DIM = 128
B_VAL_DIM = 128
B_QK_WIDTH = B_HEADS * B_KEY_DIM
B_V_WIDTH = B_HEADS * B_VAL_DIM
CONV_WIDTH = 4
B_CONV_CH = 2 * B_QK_WIDTH + B_V_WIDTH

D_FF = ((8 * D_MODEL // 3 + 255) // 256) * 256

N_BRANCH = 2
SPLIT_A = 3 * A_WIDTH
SPLIT_CONV = SPLIT_A + B_CONV_CH
SPLIT_Z = SPLIT_CONV + B_V_WIDTH
SPLIT_BETA = SPLIT_Z + B_HEADS
SPLIT_DECAY = SPLIT_BETA + B_HEADS
D_IN = SPLIT_DECAY + N_BRANCH * D_MODEL

kernel_name = "hybrid_chunked_attn_gated_deltanet_block"


def rmsnorm(x, g):
    xf = x.astype(jnp.float32)
    y = xf * lax.rsqrt(jnp.mean(xf * xf, axis=-1, keepdims=True) + EPS)
    return (y * g.astype(jnp.float32)).astype(x.dtype)


def l2norm(x):
    return x * lax.rsqrt(jnp.sum(x * x, axis=-1, keepdims=True) + EPS)


def causal_depthwise_conv(x, w):
    s = x.shape[1]
    k = w.shape[0]
    xp = jnp.pad(x, ((0, 0), (k - 1, 0), (0, 0)))
    out = xp[:, 0:s] * w[0]
    for i in range(1, k):
        out = out + xp[:, i:i + s] * w[i]
    return out


def chunked_band_attention(q, k, v, rel_bias):
    b, s, h, d = q.shape
    nc = s // CHUNK
    qc = q.reshape(b, nc, CHUNK, h, d) * (d ** -0.5)
    pad = ((0, 0), (A_LOOKBACK, 0), (0, 0), (0, 0), (0, 0))
    kc = jnp.pad(k.reshape(b, nc, CHUNK, h, d), pad)
    vc = jnp.pad(v.reshape(b, nc, CHUNK, h, d), pad)
    scores = jnp.stack(
        [jnp.einsum('bnqhd,bnkhd->bhnqk', qc, kc[:, j:j + nc]) for j in range(A_BAND)],
        axis=4)
    scores = scores.reshape(b, h, nc, CHUNK, A_BAND * CHUNK).astype(jnp.float32)
    qi = jnp.arange(CHUNK)
    kj = jnp.arange(A_BAND * CHUNK)
    rel = (A_LOOKBACK * CHUNK + qi[:, None]) - kj[None, :]
    idx = jnp.clip(rel, -REL_CLIP, REL_CLIP) + REL_CLIP
    bias = rel_bias[:, idx].astype(jnp.float32)
    valid = (jnp.arange(nc)[:, None] + jnp.arange(A_BAND)[None, :]) >= A_LOOKBACK
    valid = jnp.repeat(valid, CHUNK, axis=1)
    scores = jnp.where(valid[None, None, :, None, :], scores + bias[None, :, None], -1e30)
    probs = jax.nn.softmax(scores, axis=-1).astype(v.dtype)
    probs = probs.reshape(b, h, nc, CHUNK, A_BAND, CHUNK)
    out = jnp.einsum('bhnqk,bnkhd->bnqhd', probs[:, :, :, :, 0], vc[:, 0:nc])
    for j in range(1, A_BAND):
        out = out + jnp.einsum('bhnqk,bnkhd->bnqhd', probs[:, :, :, :, j], vc[:, j:j + nc])
    return out.reshape(b, s, h * d)


def gated_delta_rule(q, k, v, g, beta):
    out_dtype = v.dtype
    f32 = jnp.float32
    b, s, h, dk = q.shape
    dv = v.shape[-1]
    nc = s // CHUNK
    L = CHUNK

    def to_chunks(t):
        return jnp.moveaxis(t.reshape((b, nc, L, h) + t.shape[3:]), 3, 1)

    q = to_chunks(l2norm(q.astype(f32)) * (dk ** -0.5))
    k = to_chunks(l2norm(k.astype(f32)))
    v = to_chunks(v.astype(f32))
    g = to_chunks(g.astype(f32))
    beta = to_chunks(beta.astype(f32))

    gc = jnp.cumsum(g, axis=-1)
    ti = jnp.arange(L)
    tri_incl = ti[:, None] >= ti[None, :]
    tri_strict = ti[:, None] > ti[None, :]
    diff = gc[..., :, None] - gc[..., None, :]
    decay = jnp.exp(jnp.where(tri_incl, diff, -jnp.inf))
    kk = jnp.einsum('bhnid,bhnjd->bhnij', k, k)
    a_mat = jnp.where(tri_strict, beta[..., :, None] * kk * decay, 0.0)
    eye = jnp.eye(L, dtype=f32)
    rhs = jnp.concatenate([beta[..., None] * v,
                           (beta * jnp.exp(gc))[..., None] * k], axis=-1)
    sol = lax.linalg.triangular_solve(eye + a_mat, rhs, left_side=True,
                                      lower=True, unit_diagonal=True)
    u, wk = sol[..., :dv], sol[..., dv:]
    pqk = jnp.einsum('bhnid,bhnjd->bhnij', q, k) * decay
    gamma = jnp.exp(gc)
    g_last = gc[..., -1]
    kdec = k * jnp.exp(g_last[..., None] - gc)[..., None]

    def front(t):
        return jnp.moveaxis(t, 2, 0)

    xs = (front(u), front(wk), front(q), front(pqk), front(gamma), front(kdec), front(g_last))

    def step(state, inp):
        u_c, wk_c, q_c, p_c, gam_c, kd_c, gl_c = inp
        w = u_c - jnp.einsum('bhld,bhvd->bhlv', wk_c, state)
        o = gam_c[..., None] * jnp.einsum('bhld,bhvd->bhlv', q_c, state) \
            + jnp.einsum('bhts,bhsv->bhtv', p_c, w)
        state = jnp.exp(gl_c)[..., None, None] * state + jnp.einsum('bhlv,bhld->bhvd', w, kd_c)
        return state, o

    s0 = jnp.zeros((b, h, dv, dk), f32)
    _, o = lax.scan(step, s0, xs)
    o = jnp.transpose(o, (1, 0, 3, 2, 4)).reshape(b, s, h, dv)
    return o.astype(out_dtype)


def setup_inputs(seed: int = 0) -> dict:
    key = jax.random.key(seed)
    ks = jax.random.split(key, 24)
    f32 = jnp.float32

    def nrm(k, shape, scale):
        return jax.random.normal(k, shape, f32) * scale

    def gain(k, shape):
        return 1.0 + 0.1 * jax.random.normal(k, shape, f32)

    x = jax.random.normal(ks[0], (BATCH, SEQ, D_MODEL), f32)
    p = jax.random.normal(ks[1], (DEPTH, BATCH, SEQ, PLE_DIM), f32)
    g_mix = gain(ks[2], (DEPTH, D_MODEL))
    w_in = nrm(ks[3], (DEPTH, D_MODEL, D_IN), D_MODEL ** -0.5)
    conv_w = nrm(ks[4], (DEPTH, CONV_WIDTH, B_CONV_CH), CONV_WIDTH ** -0.5)
    a_log = jnp.log(jax.random.uniform(ks[5], (DEPTH, B_HEADS), f32, 1.0, 16.0))
    dt = jnp.exp(jax.random.uniform(ks[6], (DEPTH, B_HEADS), f32,
                                    np.log(1e-3), np.log(1e-1)))
    dt_bias = dt + jnp.log(-jnp.expm1(-dt))
    rel_bias = nrm(ks[7], (DEPTH, A_HEADS, N_REL), 0.1)
    w_onorm = gain(ks[8], (DEPTH, B_VAL_DIM))
    w_branch_a = nrm(ks[9], (DEPTH, A_WIDTH, D_MODEL), A_WIDTH ** -0.5)
    w_branch_b = nrm(ks[10], (DEPTH, B_V_WIDTH, D_MODEL), B_V_WIDTH ** -0.5)
    w_out = nrm(ks[11], (DEPTH, D_MODEL, D_MODEL), D_MODEL ** -0.5)
    g_ffn = gain(ks[12], (DEPTH, D_MODEL))
    w_gate_up = nrm(ks[13], (DEPTH, D_MODEL, 2 * D_FF), D_MODEL ** -0.5)
    w_down = nrm(ks[14], (DEPTH, D_FF, D_MODEL), D_FF ** -0.5)
    g_ple = gain(ks[15], (DEPTH, D_MODEL))
    w_ple_gate = nrm(ks[16], (DEPTH, D_MODEL, D_MODEL), D_MODEL ** -0.5)
    w_ple_proj = nrm(ks[17], (DEPTH, PLE_DIM, D_MODEL), PLE_DIM ** -0.5)
    g_final = gain(ks[18], (D_MODEL,))
    return {"x": x, "p": p, "g_mix": g_mix, "w_in": w_in, "conv_w": conv_w,
            "a_log": a_log, "dt_bias": dt_bias, "rel_bias": rel_bias, "w_onorm": w_onorm,
            "w_branch_a": w_branch_a, "w_branch_b": w_branch_b, "w_out": w_out,
            "g_ffn": g_ffn, "w_gate_up": w_gate_up, "w_down": w_down,
            "g_ple": g_ple, "w_ple_gate": w_ple_gate, "w_ple_proj": w_ple_proj,
            "g_final": g_final}


def reference(x, p, g_mix, w_in, conv_w, a_log, dt_bias, rel_bias, w_onorm,
              w_branch_a, w_branch_b, w_out, g_ffn, w_gate_up, w_down,
              g_ple, w_ple_gate, w_ple_proj, g_final):
    b, s, _ = x.shape
    for i in range(DEPTH):
        h = rmsnorm(x, g_mix[i])
        proj = h @ w_in[i]
        qkv_a = proj[..., :SPLIT_A]
        conv_in = proj[..., SPLIT_A:SPLIT_CONV]
        z = proj[..., SPLIT_CONV:SPLIT_Z]
        beta_raw = proj[..., SPLIT_Z:SPLIT_BETA]
        decay_raw = proj[..., SPLIT_BETA:SPLIT_DECAY]
        gates = proj[..., SPLIT_DECAY:]

        q_a = qkv_a[..., :A_WIDTH].reshape(b, s, A_HEADS, A_HEAD_DIM)
        k_a = qkv_a[..., A_WIDTH:2 * A_WIDTH].reshape(b, s, A_HEADS, A_HEAD_DIM)
        v_a = qkv_a[..., 2 * A_WIDTH:].reshape(b, s, A_HEADS, A_HEAD_DIM)
        y_a = chunked_band_attention(q_a, k_a, v_a, rel_bias[i])

        c = jax.nn.silu(causal_depthwise_conv(conv_in, conv_w[i]))
        q_b = c[..., :B_QK_WIDTH].reshape(b, s, B_HEADS, B_KEY_DIM)
        k_b = c[..., B_QK_WIDTH:2 * B_QK_WIDTH].reshape(b, s, B_HEADS, B_KEY_DIM)
        v_b = c[..., 2 * B_QK_WIDTH:].reshape(b, s, B_HEADS, B_VAL_DIM)
        beta = jax.nn.sigmoid(beta_raw)
        g = -jnp.exp(a_log[i]) * jax.nn.softplus(decay_raw + dt_bias[i])
        o_b = gated_delta_rule(q_b, k_b, v_b, g, beta)
        o_b = rmsnorm(o_b, w_onorm[i]) * jax.nn.silu(z.reshape(b, s, B_HEADS, B_VAL_DIM))
        y_b = o_b.reshape(b, s, B_V_WIDTH)

        gate_a = jax.nn.sigmoid(gates[..., :D_MODEL])
        gate_b = jax.nn.sigmoid(gates[..., D_MODEL:])
        merged = gate_a * (y_a @ w_branch_a[i]) + gate_b * (y_b @ w_branch_b[i])
        x = x + merged @ w_out[i]

        h = rmsnorm(x, g_ffn[i])
        gu = h @ w_gate_up[i]
        x = x + (jax.nn.silu(gu[..., :D_FF]) * gu[..., D_FF:]) @ w_down[i]

        ple_gate = jax.nn.sigmoid(rmsnorm(x, g_ple[i]) @ w_ple_gate[i])
        x = x + ple_gate * (p[i] @ w_ple_proj[i])
    return rmsnorm(x, g_final)
```

```python
import functools

import numpy as np
import jax
import jax.numpy as jnp
from jax import lax
from jax.experimental import pallas as pl
from jax.experimental.pallas import tpu as pltpu

D_MODEL = 1024
CHUNK = 64
PLE_DIM = 256
EPS = 1e-6

A_HEADS = 8
A_HEAD_DIM = 64
A_WIDTH = A_HEADS * A_HEAD_DIM
A_LOOKBACK = 8
REL_CLIP = 128

B_HEADS = 4
B_DIM = 128
B_WIDTH = B_HEADS * B_DIM
CONV_WIDTH = 4
B_CONV_CH = 3 * B_WIDTH

D_FF = 2816

SPLIT_A = 3 * A_WIDTH
SPLIT_CONV = SPLIT_A + B_CONV_CH
SPLIT_Z = SPLIT_CONV + B_WIDTH
SPLIT_BETA = SPLIT_Z + B_HEADS
SPLIT_DECAY = SPLIT_BETA + B_HEADS

LANES = 128
HALO = 8
TOKEN_TILE = 512
Q_GROUP = 256
K_WINDOW = Q_GROUP + A_LOOKBACK * CHUNK
SLAB = 256
FF_SLAB = 256
VMEM_LIMIT = 56 * 1024 * 1024

F32 = jnp.float32
BF16 = jnp.bfloat16
NT_DIMS = (((1,), (1,)), ((), ()))
TN_DIMS = (((0,), (0,)), ((), ()))


def _dot(a, b):
    return jnp.dot(a, b, preferred_element_type=F32)


def _dot_nt(a, b):
    return lax.dot_general(a, b, NT_DIMS, preferred_element_type=F32)


def _dot_tn(a, b):
    return lax.dot_general(a, b, TN_DIMS, preferred_element_type=F32)


def _rms(x, g):
    return x * lax.rsqrt(jnp.mean(x * x, axis=-1, keepdims=True) + EPS) * g


def _sigmoid(x):
    return 1.0 / (1.0 + jnp.exp(-x))


def _silu(x):
    return x * _sigmoid(x)


def _params(*sem):
    return pltpu.CompilerParams(dimension_semantics=sem, vmem_limit_bytes=VMEM_LIMIT)


def _const_spec(shape):
    nd = len(shape)
    return pl.BlockSpec(shape, lambda *_: (0,) * nd, pipeline_mode=pl.Buffered(1))


def _inproj_kernel(tiles_per_seq, x_ref, g_ref, w_ref, cw_ref, alog_ref, dtb_ref,
                   qkva_ref, cqkv_ref, z_ref, bg_ref, cbuf):
    tm = x_ref.shape[0]
    i = pl.program_id(0)
    h = _rms(x_ref[...], g_ref[...]).astype(BF16)

    qkv = _dot(h, w_ref[:, 0:SPLIT_A])
    qkva_ref[:, 0:A_WIDTH] = (qkv[:, 0:A_WIDTH] * (A_HEAD_DIM ** -0.5)).astype(BF16)
    qkva_ref[:, A_WIDTH:] = qkv[:, A_WIDTH:].astype(BF16)

    @pl.when(i % tiles_per_seq == 0)
    def _():
        cbuf[0:HALO, :] = jnp.zeros((HALO, B_CONV_CH), F32)

    @pl.when(i % tiles_per_seq != 0)
    def _():
        cbuf[0:HALO, :] = cbuf[tm:tm + HALO, :]

    cbuf[HALO:, :] = _dot(h, w_ref[:, SPLIT_A:SPLIT_CONV])
    for j in range(B_CONV_CH // LANES):
        cols = slice(j * LANES, (j + 1) * LANES)
        base = HALO - (CONV_WIDTH - 1)
        acc = cbuf[base:base + tm, cols] * cw_ref[0:1, cols]
        for t in range(1, CONV_WIDTH):
            acc = acc + cbuf[base + t:base + t + tm, cols] * cw_ref[t:t + 1, cols]
        c = _silu(acc)
        if j < 2 * B_HEADS:
            c = c * lax.rsqrt(jnp.sum(c * c, axis=-1, keepdims=True) + EPS)
            if j < B_HEADS:
                c = c * (B_DIM ** -0.5)
        cqkv_ref[:, cols] = c.astype(BF16)

    z_ref[...] = _dot(h, w_ref[:, SPLIT_CONV:SPLIT_Z]).astype(BF16)

    raw = _dot(h, w_ref[:, SPLIT_Z:SPLIT_Z + LANES])
    lane = lax.broadcasted_iota(jnp.int32, raw.shape, 1)
    sp_in = raw + dtb_ref[...]
    softplus = jnp.maximum(sp_in, 0.0) + jnp.log1p(jnp.exp(-jnp.abs(sp_in)))
    bg_ref[...] = jnp.where(lane < B_HEADS, _sigmoid(raw), -jnp.exp(alog_ref[...]) * softplus)


def _inproj(x2, g_mix, w_in_r, conv_w, alog_pad, dtb_pad, seq):
    t = x2.shape[0]
    tm = TOKEN_TILE
    nw = w_in_r.shape[1]
    row = lambda width: pl.BlockSpec((tm, width), lambda i: (i, 0))
    return pl.pallas_call(
        functools.partial(_inproj_kernel, seq // tm),
        grid=(t // tm,),
        in_specs=[row(D_MODEL), _const_spec((1, D_MODEL)), _const_spec((D_MODEL, nw)),
                  _const_spec((CONV_WIDTH, B_CONV_CH)), _const_spec((1, LANES)), _const_spec((1, LANES))],
        out_specs=[row(SPLIT_A), row(B_CONV_CH), row(B_WIDTH), row(LANES)],
        out_shape=[jax.ShapeDtypeStruct((t, SPLIT_A), BF16), jax.ShapeDtypeStruct((t, B_CONV_CH), BF16),
                   jax.ShapeDtypeStruct((t, B_WIDTH), BF16), jax.ShapeDtypeStruct((t, LANES), F32)],
        scratch_shapes=[pltpu.VMEM((tm + HALO, B_CONV_CH), F32)],
        compiler_params=_params("arbitrary"),
        name="inproj",
    )(x2, g_mix, w_in_r, conv_w, alog_pad, dtb_pad)


def _attn_kernel(q_ref, k_ref, v_ref, bias_ref, o_ref, kpad, vpad):
    seq = q_ref.shape[0]
    pad = A_LOOKBACK * CHUNK
    kpad[0:pad, :] = jnp.zeros((pad, A_WIDTH), BF16)
    vpad[0:pad, :] = jnp.zeros((pad, A_WIDTH), BF16)
    kpad[pad:, :] = k_ref[...]
    vpad[pad:, :] = v_ref[...]
    lane = lax.broadcasted_iota(jnp.int32, (Q_GROUP, LANES), 1)
    col = lax.broadcasted_iota(jnp.int32, (Q_GROUP, K_WINDOW), 1)

    def group(g, carry):
        r0 = pl.multiple_of(g * Q_GROUP, Q_GROUP)
        in_seq = col >= pad - g * Q_GROUP
        for hp in range(A_HEADS // 2):
            cols = slice(hp * LANES, (hp + 1) * LANES)
            q2 = q_ref[pl.ds(r0, Q_GROUP), cols]
            k2 = kpad[pl.ds(r0, K_WINDOW), cols]
            v2 = vpad[pl.ds(r0, K_WINDOW), cols]
            outs = []
            for sub in range(2):
                head_lanes = (lane >= sub * A_HEAD_DIM) & (lane < (sub + 1) * A_HEAD_DIM)
                qm = jnp.where(head_lanes, q2, jnp.zeros_like(q2))
                s = _dot_nt(qm, k2)
                s = jnp.where(in_seq, s + bias_ref[2 * hp + sub], -1e30)
                p = jnp.exp(s - jnp.max(s, axis=-1, keepdims=True))
                inv = 1.0 / jnp.sum(p, axis=-1, keepdims=True)
                outs.append(_dot(p.astype(BF16), v2) * inv)
            o_ref[pl.ds(r0, Q_GROUP), cols] = jnp.where(lane < A_HEAD_DIM, outs[0], outs[1]).astype(BF16)
        return carry

    lax.fori_loop(0, seq // Q_GROUP, group, 0)


def _attention(qkva, bias_tab):
    b, seq, _ = qkva.shape
    blk = lambda j: pl.BlockSpec((None, seq, A_WIDTH), lambda i, j=j: (i, 0, j))
    return pl.pallas_call(
        _attn_kernel,
        grid=(b,),
        in_specs=[blk(0), blk(1), blk(2), _const_spec(bias_tab.shape)],
        out_specs=pl.BlockSpec((None, seq, A_WIDTH), lambda i: (i, 0, 0)),
        out_shape=jax.ShapeDtypeStruct((b, seq, A_WIDTH), BF16),
        scratch_shapes=[pltpu.VMEM((seq + A_LOOKBACK * CHUNK, A_WIDTH), BF16)] * 2,
        compiler_params=_params("arbitrary"),
        name="band_attention",
    )(qkva, qkva, qkva, bias_tab)


def _band_bias_table(rel_bias):
    qi = np.arange(Q_GROUP)[:, None]
    kj = np.arange(K_WINDOW)[None, :]
    rel = A_LOOKBACK * CHUNK + qi - kj
    idx = np.clip(rel, -REL_CLIP, REL_CLIP) + REL_CLIP
    qc, kc = qi // CHUNK, kj // CHUNK
    in_band = (kc >= qc) & (kc <= qc + A_LOOKBACK)
    return jnp.where(in_band[None], rel_bias[:, idx].astype(F32), -1e30)


def _delta_masks():
    i = np.arange(SLAB)[:, None]
    j = np.arange(SLAB)[None, :]
    same = (i // CHUNK) == (j // CHUNK)
    cum = (same & (i >= j)).astype(np.float32)
    tot = same.astype(np.float32)
    levels = [(((i ^ j) >> l) == 1) & (i > j) for l in range(6)]
    return jnp.asarray(np.concatenate([cum, tot], 0)), jnp.asarray(np.stack(levels).astype(np.float32), BF16)


def _delta_kernel(q_ref, k_ref, v_ref, bg_ref, z_ref, wn_ref, sum_ref, lvl_ref, o_ref, s_ref):
    seq = q_ref.shape[0]
    s_ref[...] = jnp.zeros(s_ref.shape, F32)
    ri = lax.broadcasted_iota(jnp.int32, (SLAB, SLAB), 0)
    ci = lax.broadcasted_iota(jnp.int32, (SLAB, SLAB), 1)
    same_chunk = (ri // CHUNK) == (ci // CHUNK)
    incl = same_chunk & (ri >= ci)
    strict = same_chunk & (ri > ci)
    eye = (ri == ci).astype(F32)

    def slab(s, carry):
        r0 = pl.multiple_of(s * SLAB, SLAB)
        rows = pl.ds(r0, SLAB)
        bg = bg_ref[rows, :]
        sums = jnp.dot(sum_ref[...], bg, preferred_element_type=F32, precision=lax.Precision.HIGHEST)
        gc_all, gl_all = sums[:SLAB], sums[SLAB:]
        gc_all_t = gc_all.T
        for h in range(B_HEADS):
            cols = slice(h * B_DIM, (h + 1) * B_DIM)
            q, k, v = q_ref[rows, cols], k_ref[rows, cols], v_ref[rows, cols]
            beta = bg[:, h:h + 1]
            gc = gc_all[:, B_HEADS + h:B_HEADS + h + 1]
            gl = gl_all[:, B_HEADS + h:B_HEADS + h + 1]
            gc_row = gc_all_t[B_HEADS + h:B_HEADS + h + 1, :]
            decay = jnp.exp(jnp.where(incl, gc - gc_row, -1e30))
            qk = _dot_nt(jnp.concatenate([q, k], axis=0), k)
            pqk = (qk[:SLAB] * decay).astype(BF16)
            a = jnp.where(strict, qk[SLAB:] * decay * beta, 0.0)
            a16 = a.astype(BF16)
            d = eye - a * lvl_ref[0].astype(F32)
            for l in range(1, 6):
                d16 = d.astype(BF16)
                d = d - _dot(d16, _dot(a16 * lvl_ref[l], d16).astype(BF16))
            gamma = jnp.exp(gc)
            kf, vf = k.astype(F32), v.astype(F32)
            rhs = jnp.concatenate([beta * vf, (beta * gamma) * kf], axis=1).astype(BF16)
            sol = _dot(d.astype(BF16), rhs)
            sol16 = sol.astype(BF16)
            x2 = _dot(pqk, sol16)
            o0 = x2[:, :B_DIM]
            qp = (gamma * q.astype(F32) - x2[:, B_DIM:]).astype(BF16)
            kdec = (kf * jnp.exp(gl - gc)).astype(BF16)
            egl = jnp.exp(gl)
            zf = z_ref[rows, cols].astype(F32)
            for c in range(SLAB // CHUNK):
                cr = slice(c * CHUNK, (c + 1) * CHUNK)
                state = s_ref[h]
                o = _dot_nt(qp[cr], state.astype(BF16)) + o0[cr]
                mn = _dot_tn(sol16[cr], kdec[cr])
                s_ref[h] = (egl[c * CHUNK:c * CHUNK + 1, :] * state
                            - _dot(state.astype(BF16), mn[B_DIM:].astype(BF16)) + mn[:B_DIM])
                y = _rms(o, wn_ref[...]) * _silu(zf[cr])
                o_ref[pl.ds(r0 + c * CHUNK, CHUNK), cols] = y.astype(BF16)
        return carry

    lax.fori_loop(0, seq // SLAB, slab, 0)


def _deltanet(cqkv, bg, z, w_onorm):
    b, seq, _ = cqkv.shape
    sum_mask, lvl_mask = _delta_masks()
    blk = lambda j: pl.BlockSpec((None, seq, B_WIDTH), lambda i, j=j: (i, 0, j))
    return pl.pallas_call(
        _delta_kernel,
        grid=(b,),
        in_specs=[blk(0), blk(1), blk(2),
                  pl.BlockSpec((None, seq, LANES), lambda i: (i, 0, 0)),
                  pl.BlockSpec((None, seq, B_WIDTH), lambda i: (i, 0, 0)),
                  _const_spec((1, B_DIM)), _const_spec(sum_mask.shape), _const_spec(lvl_mask.shape)],
        out_specs=pl.BlockSpec((None, seq, B_WIDTH), lambda i: (i, 0, 0)),
        out_shape=jax.ShapeDtypeStruct((b, seq, B_WIDTH), BF16),
        scratch_shapes=[pltpu.VMEM((B_HEADS, B_DIM, B_DIM), F32)],
        compiler_params=_params("arbitrary"),
        name="gated_deltanet",
    )(cqkv, cqkv, cqkv, bg, z, w_onorm, sum_mask, lvl_mask)


def _merge_kernel(x_ref, ya_ref, yb_ref, g_ref, wg_ref, wa_ref, wb_ref, wo_ref, o_ref):
    x = x_ref[...]
    h = _rms(x, g_ref[...]).astype(BF16)
    ga = _sigmoid(_dot(h, wg_ref[:, :D_MODEL]))
    merged = ga * _dot(ya_ref[...], wa_ref[...])
    gb = _sigmoid(_dot(h, wg_ref[:, D_MODEL:]))
    merged = merged + gb * _dot(yb_ref[...], wb_ref[...])
    o_ref[...] = x + _dot(merged.astype(BF16), wo_ref[...])


def _merge(x2, ya, yb, g_mix, w_gates, w_a, w_b, w_out):
    t = x2.shape[0]
    tm = TOKEN_TILE
    row = lambda width: pl.BlockSpec((tm, width), lambda i: (i, 0))
    return pl.pallas_call(
        _merge_kernel,
        grid=(t // tm,),
        in_specs=[row(D_MODEL), row(A_WIDTH), row(B_WIDTH), _const_spec((1, D_MODEL)),
                  _const_spec(w_gates.shape), _const_spec(w_a.shape), _const_spec(w_b.shape),
                  _const_spec(w_out.shape)],
        out_specs=row(D_MODEL),
        out_shape=jax.ShapeDtypeStruct((t, D_MODEL), F32),
        compiler_params=_params("arbitrary"),
        name="gated_merge",
    )(x2, ya, yb, g_mix, w_gates, w_a, w_b, w_out)


def _ffn_kernel(final, x_ref, p_ref, gf_ref, wgu_ref, wd_ref, gp_ref, wpg_ref, wpp_ref, gfin_ref, o_ref, acc):
    x = x_ref[...]
    h = _rms(x, gf_ref[...]).astype(BF16)
    acc[...] = x
    for j in range(D_FF // FF_SLAB):
        gate = _dot(h, wgu_ref[:, j * FF_SLAB:(j + 1) * FF_SLAB])
        up = _dot(h, wgu_ref[:, D_FF + j * FF_SLAB:D_FF + (j + 1) * FF_SLAB])
        act = (_silu(gate) * up).astype(BF16)
        acc[...] += _dot(act, wd_ref[j * FF_SLAB:(j + 1) * FF_SLAB, :])
    x = acc[...]
    hp = _rms(x, gp_ref[...]).astype(BF16)
    ple_gate = _sigmoid(_dot(hp, wpg_ref[...]))
    x = x + ple_gate * _dot(p_ref[...].astype(BF16), wpp_ref[...])
    o_ref[...] = _rms(x, gfin_ref[...]) if final else x


def _ffn(x2, p2, g_ffn, w_gate_up, w_down, g_ple, w_ple_gate, w_ple_proj, g_final, final):
    t = x2.shape[0]
    tm = TOKEN_TILE
    row = lambda width: pl.BlockSpec((tm, width), lambda i: (i, 0))
    return pl.pallas_call(
        functools.partial(_ffn_kernel, final),
        grid=(t // tm,),
        in_specs=[row(D_MODEL), row(PLE_DIM), _const_spec((1, D_MODEL)),
                  _const_spec(w_gate_up.shape), _const_spec(w_down.shape), _const_spec((1, D_MODEL)),
                  _const_spec(w_ple_gate.shape), _const_spec(w_ple_proj.shape), _const_spec((1, D_MODEL))],
        out_specs=row(D_MODEL),
        out_shape=jax.ShapeDtypeStruct((t, D_MODEL), F32),
        scratch_shapes=[pltpu.VMEM((tm, D_MODEL), F32)],
        compiler_params=_params("arbitrary"),
        name="ffn_ple_final",
    )(x2, p2, g_ffn, w_gate_up, w_down, g_ple, w_ple_gate, w_ple_proj, g_final)


def _layer(final, x2, p2, seq, g_mix, w_in, conv_w, a_log, dt_bias, rel_bias, w_onorm, w_branch_a, w_branch_b,
           w_out, g_ffn, w_gate_up, w_down, g_ple, w_ple_gate, w_ple_proj, g_final):
    t = x2.shape[0]
    b = t // seq
    row = lambda v: v.reshape(1, -1).astype(F32)
    w_bd = jnp.pad(w_in[:, SPLIT_Z:SPLIT_DECAY], ((0, 0), (0, LANES - 2 * B_HEADS)))
    w_in_r = jnp.concatenate([w_in[:, :SPLIT_Z], w_bd], axis=1).astype(BF16)
    w_gates = w_in[:, SPLIT_DECAY:].astype(BF16)
    head_pad = lambda v: jnp.pad(v.astype(F32), (B_HEADS, LANES - 2 * B_HEADS)).reshape(1, LANES)

    qkva, cqkv, z, bg = _inproj(x2, row(g_mix), w_in_r, conv_w.astype(F32), head_pad(a_log),
                                head_pad(dt_bias), seq)
    ya = _attention(qkva.reshape(b, seq, SPLIT_A), _band_bias_table(rel_bias))
    yb = _deltanet(cqkv.reshape(b, seq, B_CONV_CH), bg.reshape(b, seq, LANES),
                   z.reshape(b, seq, B_WIDTH), row(w_onorm))
    x2 = _merge(x2, ya.reshape(t, A_WIDTH), yb.reshape(t, B_WIDTH), row(g_mix), w_gates,
                w_branch_a.astype(BF16), w_branch_b.astype(BF16), w_out.astype(BF16))
    return _ffn(x2, p2, row(g_ffn), w_gate_up.astype(BF16), w_down.astype(BF16), row(g_ple),
                w_ple_gate.astype(BF16), w_ple_proj.astype(BF16), row(g_final), final)


def kernel(x, p, g_mix, w_in, conv_w, a_log, dt_bias, rel_bias, w_onorm, w_branch_a, w_branch_b, w_out,
           g_ffn, w_gate_up, w_down, g_ple, w_ple_gate, w_ple_proj, g_final):
    b, seq, _ = x.shape
    depth = p.shape[0]
    x2 = x.reshape(b * seq, D_MODEL)
    for i in range(depth):
        x2 = _layer(i == depth - 1, x2, p[i].reshape(b * seq, PLE_DIM), seq, g_mix[i], w_in[i], conv_w[i],
                    a_log[i], dt_bias[i], rel_bias[i], w_onorm[i], w_branch_a[i], w_branch_b[i], w_out[i],
                    g_ffn[i], w_gate_up[i], w_down[i], g_ple[i], w_ple_gate[i], w_ple_proj[i], g_final)
    return x2.reshape(b, seq, D_MODEL)
```

```python
import functools

import numpy as np
import jax
import jax.numpy as jnp
from jax import lax
from jax.experimental import pallas as pl
from jax.experimental.pallas import tpu as pltpu

D_MODEL = 1024
CHUNK = 64
PLE_DIM = 256
EPS = 1e-6

A_HEADS = 8
A_HEAD_DIM = 64
A_WIDTH = A_HEADS * A_HEAD_DIM
A_LOOKBACK = 8
REL_CLIP = 128

B_HEADS = 4
B_DIM = 128
B_WIDTH = B_HEADS * B_DIM
CONV_WIDTH = 4
B_CONV_CH = 3 * B_WIDTH

D_FF = 2816

SPLIT_A = 3 * A_WIDTH
SPLIT_CONV = SPLIT_A + B_CONV_CH
SPLIT_Z = SPLIT_CONV + B_WIDTH
SPLIT_BETA = SPLIT_Z + B_HEADS
SPLIT_DECAY = SPLIT_BETA + B_HEADS

LANES = 128
HALO = 8
TOKEN_TILE = 512
Q_GROUP = 256
K_WINDOW = Q_GROUP + A_LOOKBACK * CHUNK
ROLL_W = 1024
SLAB = 256
FF_SLAB = 256
VMEM_LIMIT = 56 * 1024 * 1024

F32 = jnp.float32
BF16 = jnp.bfloat16
NT_DIMS = (((1,), (1,)), ((), ()))
TN_DIMS = (((0,), (0,)), ((), ()))


def _dot(a, b):
    return jnp.dot(a, b, preferred_element_type=F32)


def _dot_nt(a, b):
    return lax.dot_general(a, b, NT_DIMS, preferred_element_type=F32)


def _dot_tn(a, b):
    return lax.dot_general(a, b, TN_DIMS, preferred_element_type=F32)


def _rms(x, g):
    return x * lax.rsqrt(jnp.mean(x * x, axis=-1, keepdims=True) + EPS) * g


def _sigmoid(x):
    return 1.0 / (1.0 + jnp.exp(-x))


def _silu(x):
    return x * _sigmoid(x)


def _params(*sem):
    return pltpu.CompilerParams(dimension_semantics=sem, vmem_limit_bytes=VMEM_LIMIT)


def _const_spec(shape):
    nd = len(shape)
    return pl.BlockSpec(shape, lambda *_: (0,) * nd, pipeline_mode=pl.Buffered(1))


def _inproj_kernel(tiles_per_seq, x_ref, g_ref, w_ref, cw_ref, alog_ref, dtb_ref,
                   qkva_ref, cqkv_ref, z_ref, bg_ref, cbuf):
    tm = x_ref.shape[0]
    i = pl.program_id(0)
    h = _rms(x_ref[...], g_ref[...]).astype(BF16)

    qkv = _dot(h, w_ref[:, 0:SPLIT_A])
    qkva_ref[:, 0:A_WIDTH] = (qkv[:, 0:A_WIDTH] * (A_HEAD_DIM ** -0.5)).astype(BF16)
    qkva_ref[:, A_WIDTH:] = qkv[:, A_WIDTH:].astype(BF16)

    @pl.when(i % tiles_per_seq == 0)
    def _():
        cbuf[0:HALO, :] = jnp.zeros((HALO, B_CONV_CH), F32)

    @pl.when(i % tiles_per_seq != 0)
    def _():
        cbuf[0:HALO, :] = cbuf[tm:tm + HALO, :]

    cbuf[HALO:, :] = _dot(h, w_ref[:, SPLIT_A:SPLIT_CONV])
    for j in range(B_CONV_CH // LANES):
        cols = slice(j * LANES, (j + 1) * LANES)
        base = HALO - (CONV_WIDTH - 1)
        acc = cbuf[base:base + tm, cols] * cw_ref[0:1, cols]
        for t in range(1, CONV_WIDTH):
            acc = acc + cbuf[base + t:base + t + tm, cols] * cw_ref[t:t + 1, cols]
        c = _silu(acc)
        if j < 2 * B_HEADS:
            c = c * lax.rsqrt(jnp.sum(c * c, axis=-1, keepdims=True) + EPS)
            if j < B_HEADS:
                c = c * (B_DIM ** -0.5)
        cqkv_ref[:, cols] = c.astype(BF16)

    z_ref[...] = _dot(h, w_ref[:, SPLIT_CONV:SPLIT_Z]).astype(BF16)

    raw = _dot(h, w_ref[:, SPLIT_Z:SPLIT_Z + LANES])
    lane = lax.broadcasted_iota(jnp.int32, raw.shape, 1)
    sp_in = raw + dtb_ref[...]
    softplus = jnp.maximum(sp_in, 0.0) + jnp.log1p(jnp.exp(-jnp.abs(sp_in)))
    bg_ref[...] = jnp.where(lane < B_HEADS, _sigmoid(raw), -jnp.exp(alog_ref[...]) * softplus)


def _inproj(x2, g_mix, w_in_r, conv_w, alog_pad, dtb_pad, seq):
    t = x2.shape[0]
    tm = TOKEN_TILE
    nw = w_in_r.shape[1]
    row = lambda width: pl.BlockSpec((tm, width), lambda i: (i, 0))
    return pl.pallas_call(
        functools.partial(_inproj_kernel, seq // tm),
        grid=(t // tm,),
        in_specs=[row(D_MODEL), _const_spec((1, D_MODEL)), _const_spec((D_MODEL, nw)),
                  _const_spec((CONV_WIDTH, B_CONV_CH)), _const_spec((1, LANES)), _const_spec((1, LANES))],
        out_specs=[row(SPLIT_A), row(B_CONV_CH), row(B_WIDTH), row(LANES)],
        out_shape=[jax.ShapeDtypeStruct((t, SPLIT_A), BF16), jax.ShapeDtypeStruct((t, B_CONV_CH), BF16),
                   jax.ShapeDtypeStruct((t, B_WIDTH), BF16), jax.ShapeDtypeStruct((t, LANES), F32)],
        scratch_shapes=[pltpu.VMEM((tm + HALO, B_CONV_CH), F32)],
        compiler_params=_params("arbitrary"),
        name="inproj",
    )(x2, g_mix, w_in_r, conv_w, alog_pad, dtb_pad)


def _attn_kernel(q_ref, k_ref, v_ref, trow_ref, o_ref, kpad, vpad, bias):
    seq = q_ref.shape[0]
    pad = A_LOOKBACK * CHUNK
    lane = lax.broadcasted_iota(jnp.int32, (Q_GROUP, LANES), 1)
    col = lax.broadcasted_iota(jnp.int32, (Q_GROUP, K_WINDOW), 1)

    @pl.when(pl.program_id(0) == 0)
    def _():
        qc = lax.broadcasted_iota(jnp.int32, (Q_GROUP, K_WINDOW), 0) // CHUNK
        kc = col // CHUNK
        in_band = (kc >= qc) & (kc <= qc + A_LOOKBACK)
        for h in range(A_HEADS):
            base = jnp.broadcast_to(trow_ref[h], (Q_GROUP, ROLL_W))
            toeplitz = pltpu.roll(base, ROLL_W - (Q_GROUP - 1), 1, stride=1, stride_axis=0)
            bias[h] = jnp.where(in_band, toeplitz[:, :K_WINDOW], -1e30)

    kpad[0:pad, :] = jnp.zeros((pad, A_WIDTH), BF16)
    vpad[0:pad, :] = jnp.zeros((pad, A_WIDTH), BF16)
    kpad[pad:, :] = k_ref[...]
    vpad[pad:, :] = v_ref[...]

    def group(g, carry):
        r0 = pl.multiple_of(g * Q_GROUP, Q_GROUP)
        in_seq = col >= pad - g * Q_GROUP
        for hp in range(A_HEADS // 2):
            cols = slice(hp * LANES, (hp + 1) * LANES)
            q2 = q_ref[pl.ds(r0, Q_GROUP), cols]
            k2 = kpad[pl.ds(r0, K_WINDOW), cols]
            v2 = vpad[pl.ds(r0, K_WINDOW), cols]
            outs = []
            for sub in range(2):
                head_lanes = (lane >= sub * A_HEAD_DIM) & (lane < (sub + 1) * A_HEAD_DIM)
                qm = jnp.where(head_lanes, q2, jnp.zeros_like(q2))
                s = _dot_nt(qm, k2)
                s = jnp.where(in_seq, s + bias[2 * hp + sub], -1e30)
                p = jnp.exp(s - jnp.max(s, axis=-1, keepdims=True))
                inv = 1.0 / jnp.sum(p, axis=-1, keepdims=True)
                outs.append(_dot(p.astype(BF16), v2) * inv)
            o_ref[pl.ds(r0, Q_GROUP), cols] = jnp.where(lane < A_HEAD_DIM, outs[0], outs[1]).astype(BF16)
        return carry

    lax.fori_loop(0, seq // Q_GROUP, group, 0)


def _attention(qkva, trow):
    b, seq, _ = qkva.shape
    blk = lambda j: pl.BlockSpec((None, seq, A_WIDTH), lambda i, j=j: (i, 0, j))
    return pl.pallas_call(
        _attn_kernel,
        grid=(b,),
        in_specs=[blk(0), blk(1), blk(2), _const_spec(trow.shape)],
        out_specs=pl.BlockSpec((None, seq, A_WIDTH), lambda i: (i, 0, 0)),
        out_shape=jax.ShapeDtypeStruct((b, seq, A_WIDTH), BF16),
        scratch_shapes=[pltpu.VMEM((seq + A_LOOKBACK * CHUNK, A_WIDTH), BF16)] * 2
        + [pltpu.VMEM((A_HEADS, Q_GROUP, K_WINDOW), F32)],
        compiler_params=_params("arbitrary"),
        name="band_attention",
    )(qkva, qkva, qkva, trow)


def _toeplitz_row(rel_bias):
    n_far = K_WINDOW - 1 - REL_CLIP
    n_near = ROLL_W - n_far - (2 * REL_CLIP + 1)
    far = jnp.broadcast_to(rel_bias[:, -1:], (A_HEADS, n_far))
    near = jnp.broadcast_to(rel_bias[:, :1], (A_HEADS, n_near))
    t = jnp.concatenate([far, rel_bias[:, ::-1], near], axis=1).astype(F32)
    return t.reshape(A_HEADS, 1, ROLL_W)


def _delta_masks():
    i = np.arange(SLAB)[:, None]
    j = np.arange(SLAB)[None, :]
    same = (i // CHUNK) == (j // CHUNK)
    cum = (same & (i >= j)).astype(np.float32)
    tot = same.astype(np.float32)
    levels = [(((i ^ j) >> l) == 1) & (i > j) for l in range(6)]
    return jnp.asarray(np.concatenate([cum, tot], 0)), jnp.asarray(np.stack(levels).astype(np.float32), BF16)


def _delta_kernel(q_ref, k_ref, v_ref, bg_ref, z_ref, wn_ref, sum_ref, lvl_ref, o_ref, s_ref):
    seq = q_ref.shape[0]
    s_ref[...] = jnp.zeros(s_ref.shape, F32)
    ri = lax.broadcasted_iota(jnp.int32, (SLAB, SLAB), 0)
    ci = lax.broadcasted_iota(jnp.int32, (SLAB, SLAB), 1)
    same_chunk = (ri // CHUNK) == (ci // CHUNK)
    incl = same_chunk & (ri >= ci)
    strict = same_chunk & (ri > ci)
    eye = (ri == ci).astype(F32)

    def slab(s, carry):
        r0 = pl.multiple_of(s * SLAB, SLAB)
        rows = pl.ds(r0, SLAB)
        bg = bg_ref[rows, :]
        sums = jnp.dot(sum_ref[...], bg, preferred_element_type=F32, precision=lax.Precision.HIGHEST)
        gc_all, gl_all = sums[:SLAB], sums[SLAB:]
        gc_all_t = gc_all.T
        for h in range(B_HEADS):
            cols = slice(h * B_DIM, (h + 1) * B_DIM)
            q, k, v = q_ref[rows, cols], k_ref[rows, cols], v_ref[rows, cols]
            beta = bg[:, h:h + 1]
            gc = gc_all[:, B_HEADS + h:B_HEADS + h + 1]
            gl = gl_all[:, B_HEADS + h:B_HEADS + h + 1]
            gc_row = gc_all_t[B_HEADS + h:B_HEADS + h + 1, :]
            decay = jnp.exp(jnp.where(incl, gc - gc_row, -1e30))
            qk = _dot_nt(jnp.concatenate([q, k], axis=0), k)
            pqk = (qk[:SLAB] * decay).astype(BF16)
            a = jnp.where(strict, qk[SLAB:] * decay * beta, 0.0)
            a16 = a.astype(BF16)
            d = eye - a * lvl_ref[0].astype(F32)
            for l in range(1, 6):
                d16 = d.astype(BF16)
                d = d - _dot(d16, _dot(a16 * lvl_ref[l], d16).astype(BF16))
            gamma = jnp.exp(gc)
            kf, vf = k.astype(F32), v.astype(F32)
            rhs = jnp.concatenate([beta * vf, (beta * gamma) * kf], axis=1).astype(BF16)
            sol = _dot(d.astype(BF16), rhs)
            sol16 = sol.astype(BF16)
            x2 = _dot(pqk, sol16)
            o0 = x2[:, :B_DIM]
            qp = (gamma * q.astype(F32) - x2[:, B_DIM:]).astype(BF16)
            kdec = (kf * jnp.exp(gl - gc)).astype(BF16)
            egl = jnp.exp(gl)
            zf = z_ref[rows, cols].astype(F32)
            for c in range(SLAB // CHUNK):
                cr = slice(c * CHUNK, (c + 1) * CHUNK)
                state = s_ref[h]
                o = _dot_nt(qp[cr], state.astype(BF16)) + o0[cr]
                mn = _dot_tn(sol16[cr], kdec[cr])
                s_ref[h] = (egl[c * CHUNK:c * CHUNK + 1, :] * state
                            - _dot(state.astype(BF16), mn[B_DIM:].astype(BF16)) + mn[:B_DIM])
                y = _rms(o, wn_ref[...]) * _silu(zf[cr])
                o_ref[pl.ds(r0 + c * CHUNK, CHUNK), cols] = y.astype(BF16)
        return carry

    lax.fori_loop(0, seq // SLAB, slab, 0)


def _deltanet(cqkv, bg, z, w_onorm):
    b, seq, _ = cqkv.shape
    sum_mask, lvl_mask = _delta_masks()
    blk = lambda j: pl.BlockSpec((None, seq, B_WIDTH), lambda i, j=j: (i, 0, j))
    return pl.pallas_call(
        _delta_kernel,
        grid=(b,),
        in_specs=[blk(0), blk(1), blk(2),
                  pl.BlockSpec((None, seq, LANES), lambda i: (i, 0, 0)),
                  pl.BlockSpec((None, seq, B_WIDTH), lambda i: (i, 0, 0)),
                  _const_spec((1, B_DIM)), _const_spec(sum_mask.shape), _const_spec(lvl_mask.shape)],
        out_specs=pl.BlockSpec((None, seq, B_WIDTH), lambda i: (i, 0, 0)),
        out_shape=jax.ShapeDtypeStruct((b, seq, B_WIDTH), BF16),
        scratch_shapes=[pltpu.VMEM((B_HEADS, B_DIM, B_DIM), F32)],
        compiler_params=_params("arbitrary"),
        name="gated_deltanet",
    )(cqkv, cqkv, cqkv, bg, z, w_onorm, sum_mask, lvl_mask)


def _merge_kernel(x_ref, ya_ref, yb_ref, g_ref, wg_ref, wa_ref, wb_ref, wo_ref, o_ref):
    x = x_ref[...]
    h = _rms(x, g_ref[...]).astype(BF16)
    ga = _sigmoid(_dot(h, wg_ref[:, :D_MODEL]))
    merged = ga * _dot(ya_ref[...], wa_ref[...])
    gb = _sigmoid(_dot(h, wg_ref[:, D_MODEL:]))
    merged = merged + gb * _dot(yb_ref[...], wb_ref[...])
    o_ref[...] = x + _dot(merged.astype(BF16), wo_ref[...])


def _merge(x2, ya, yb, g_mix, w_gates, w_a, w_b, w_out):
    t = x2.shape[0]
    tm = TOKEN_TILE
    row = lambda width: pl.BlockSpec((tm, width), lambda i: (i, 0))
    return pl.pallas_call(
        _merge_kernel,
        grid=(t // tm,),
        in_specs=[row(D_MODEL), row(A_WIDTH), row(B_WIDTH), _const_spec((1, D_MODEL)),
                  _const_spec(w_gates.shape), _const_spec(w_a.shape), _const_spec(w_b.shape),
                  _const_spec(w_out.shape)],
        out_specs=row(D_MODEL),
        out_shape=jax.ShapeDtypeStruct((t, D_MODEL), F32),
        compiler_params=_params("arbitrary"),
        name="gated_merge",
    )(x2, ya, yb, g_mix, w_gates, w_a, w_b, w_out)


def _ffn_kernel(final, x_ref, p_ref, gf_ref, wgu_ref, wd_ref, gp_ref, wpg_ref, wpp_ref, gfin_ref, o_ref, acc):
    x = x_ref[...]
    h = _rms(x, gf_ref[...]).astype(BF16)
    acc[...] = x
    for j in range(D_FF // FF_SLAB):
        gate = _dot(h, wgu_ref[:, j * FF_SLAB:(j + 1) * FF_SLAB])
        up = _dot(h, wgu_ref[:, D_FF + j * FF_SLAB:D_FF + (j + 1) * FF_SLAB])
        act = (_silu(gate) * up).astype(BF16)
        acc[...] += _dot(act, wd_ref[j * FF_SLAB:(j + 1) * FF_SLAB, :])
    x = acc[...]
    hp = _rms(x, gp_ref[...]).astype(BF16)
    ple_gate = _sigmoid(_dot(hp, wpg_ref[...]))
    x = x + ple_gate * _dot(p_ref[...].astype(BF16), wpp_ref[...])
    o_ref[...] = _rms(x, gfin_ref[...]) if final else x


def _ffn(x2, p2, g_ffn, w_gate_up, w_down, g_ple, w_ple_gate, w_ple_proj, g_final, final):
    t = x2.shape[0]
    tm = TOKEN_TILE
    row = lambda width: pl.BlockSpec((tm, width), lambda i: (i, 0))
    return pl.pallas_call(
        functools.partial(_ffn_kernel, final),
        grid=(t // tm,),
        in_specs=[row(D_MODEL), row(PLE_DIM), _const_spec((1, D_MODEL)),
                  _const_spec(w_gate_up.shape), _const_spec(w_down.shape), _const_spec((1, D_MODEL)),
                  _const_spec(w_ple_gate.shape), _const_spec(w_ple_proj.shape), _const_spec((1, D_MODEL))],
        out_specs=row(D_MODEL),
        out_shape=jax.ShapeDtypeStruct((t, D_MODEL), F32),
        scratch_shapes=[pltpu.VMEM((tm, D_MODEL), F32)],
        compiler_params=_params("arbitrary"),
        name="ffn_ple_final",
    )(x2, p2, g_ffn, w_gate_up, w_down, g_ple, w_ple_gate, w_ple_proj, g_final)


def _layer(final, x2, p2, seq, g_mix, w_in, conv_w, a_log, dt_bias, rel_bias, w_onorm, w_branch_a, w_branch_b,
           w_out, g_ffn, w_gate_up, w_down, g_ple, w_ple_gate, w_ple_proj, g_final):
    t = x2.shape[0]
    b = t // seq
    row = lambda v: v.reshape(1, -1).astype(F32)
    w_bd = jnp.pad(w_in[:, SPLIT_Z:SPLIT_DECAY], ((0, 0), (0, LANES - 2 * B_HEADS)))
    w_in_r = jnp.concatenate([w_in[:, :SPLIT_Z], w_bd], axis=1).astype(BF16)
    w_gates = w_in[:, SPLIT_DECAY:].astype(BF16)
    head_pad = lambda v: jnp.pad(v.astype(F32), (B_HEADS, LANES - 2 * B_HEADS)).reshape(1, LANES)

    qkva, cqkv, z, bg = _inproj(x2, row(g_mix), w_in_r, conv_w.astype(F32), head_pad(a_log),
                                head_pad(dt_bias), seq)
    ya = _attention(qkva.reshape(b, seq, SPLIT_A), _toeplitz_row(rel_bias))
    yb = _deltanet(cqkv.reshape(b, seq, B_CONV_CH), bg.reshape(b, seq, LANES),
                   z.reshape(b, seq, B_WIDTH), row(w_onorm))
    x2 = _merge(x2, ya.reshape(t, A_WIDTH), yb.reshape(t, B_WIDTH), row(g_mix), w_gates,
                w_branch_a.astype(BF16), w_branch_b.astype(BF16), w_out.astype(BF16))
    return _ffn(x2, p2, row(g_ffn), w_gate_up.astype(BF16), w_down.astype(BF16), row(g_ple),
                w_ple_gate.astype(BF16), w_ple_proj.astype(BF16), row(g_final), final)


def kernel(x, p, g_mix, w_in, conv_w, a_log, dt_bias, rel_bias, w_onorm, w_branch_a, w_branch_b, w_out,
           g_ffn, w_gate_up, w_down, g_ple, w_ple_gate, w_ple_proj, g_final):
    b, seq, _ = x.shape
    depth = p.shape[0]
    x2 = x.reshape(b * seq, D_MODEL)
    for i in range(depth):
        x2 = _layer(i == depth - 1, x2, p[i].reshape(b * seq, PLE_DIM), seq, g_mix[i], w_in[i], conv_w[i],
                    a_log[i], dt_bias[i], rel_bias[i], w_onorm[i], w_branch_a[i], w_branch_b[i], w_out[i],
                    g_ffn[i], w_gate_up[i], w_down[i], g_ple[i], w_ple_gate[i], w_ple_proj[i], g_final)
    return x2.reshape(b, seq, D_MODEL)
```

```python
import functools

import numpy as np
import jax
import jax.numpy as jnp
from jax import lax
from jax.experimental import pallas as pl
from jax.experimental.pallas import tpu as pltpu

D_MODEL = 1024
CHUNK = 64
PLE_DIM = 256
EPS = 1e-6

A_HEADS = 8
A_HEAD_DIM = 64
A_WIDTH = A_HEADS * A_HEAD_DIM
A_LOOKBACK = 8
REL_CLIP = 128

B_HEADS = 4
B_DIM = 128
B_WIDTH = B_HEADS * B_DIM
CONV_WIDTH = 4
B_CONV_CH = 3 * B_WIDTH

D_FF = 2816

SPLIT_A = 3 * A_WIDTH
SPLIT_CONV = SPLIT_A + B_CONV_CH
SPLIT_Z = SPLIT_CONV + B_WIDTH
SPLIT_BETA = SPLIT_Z + B_HEADS
SPLIT_DECAY = SPLIT_BETA + B_HEADS

LANES = 128
HALO = 8
TOKEN_TILE = 512
Q_GROUP = 256
K_WINDOW = Q_GROUP + A_LOOKBACK * CHUNK
ROLL_W = 1024
SLAB = 256
FF_SLAB = 256
VMEM_LIMIT = 56 * 1024 * 1024

F32 = jnp.float32
BF16 = jnp.bfloat16
NT_DIMS = (((1,), (1,)), ((), ()))
TN_DIMS = (((0,), (0,)), ((), ()))


def _dot(a, b):
    return jnp.dot(a, b, preferred_element_type=F32)


def _dot_nt(a, b):
    return lax.dot_general(a, b, NT_DIMS, preferred_element_type=F32)


def _dot_tn(a, b):
    return lax.dot_general(a, b, TN_DIMS, preferred_element_type=F32)


def _rms(x, g):
    return x * lax.rsqrt(jnp.mean(x * x, axis=-1, keepdims=True) + EPS) * g


def _sigmoid(x):
    return 1.0 / (1.0 + jnp.exp(-x))


def _silu(x):
    return x * _sigmoid(x)


def _params(*sem):
    return pltpu.CompilerParams(dimension_semantics=sem, vmem_limit_bytes=VMEM_LIMIT)


def _const_spec(shape):
    nd = len(shape)
    return pl.BlockSpec(shape, lambda *_: (0,) * nd, pipeline_mode=pl.Buffered(1))


def _inproj_kernel(tiles_per_seq, x_ref, g_ref, w_ref, cw_ref, alog_ref, dtb_ref,
                   qkva_ref, cqkv_ref, z_ref, bg_ref, cbuf):
    tm = x_ref.shape[0]
    i = pl.program_id(0)
    h = _rms(x_ref[...], g_ref[...]).astype(BF16)

    qkv = _dot(h, w_ref[:, 0:SPLIT_A])
    qkva_ref[:, 0:A_WIDTH] = (qkv[:, 0:A_WIDTH] * (A_HEAD_DIM ** -0.5)).astype(BF16)
    qkva_ref[:, A_WIDTH:] = qkv[:, A_WIDTH:].astype(BF16)

    @pl.when(i % tiles_per_seq == 0)
    def _():
        cbuf[0:HALO, :] = jnp.zeros((HALO, B_CONV_CH), F32)

    @pl.when(i % tiles_per_seq != 0)
    def _():
        cbuf[0:HALO, :] = cbuf[tm:tm + HALO, :]

    cbuf[HALO:, :] = _dot(h, w_ref[:, SPLIT_A:SPLIT_CONV])
    for j in range(B_CONV_CH // LANES):
        cols = slice(j * LANES, (j + 1) * LANES)
        base = HALO - (CONV_WIDTH - 1)
        acc = cbuf[base:base + tm, cols] * cw_ref[0:1, cols]
        for t in range(1, CONV_WIDTH):
            acc = acc + cbuf[base + t:base + t + tm, cols] * cw_ref[t:t + 1, cols]
        c = _silu(acc)
        if j < 2 * B_HEADS:
            c = c * lax.rsqrt(jnp.sum(c * c, axis=-1, keepdims=True) + EPS)
            if j < B_HEADS:
                c = c * (B_DIM ** -0.5)
        cqkv_ref[:, cols] = c.astype(BF16)

    z_ref[...] = _dot(h, w_ref[:, SPLIT_CONV:SPLIT_Z]).astype(BF16)

    raw = _dot(h, w_ref[:, SPLIT_Z:SPLIT_Z + LANES])
    lane = lax.broadcasted_iota(jnp.int32, raw.shape, 1)
    sp_in = raw + dtb_ref[...]
    softplus = jnp.maximum(sp_in, 0.0) + jnp.log1p(jnp.exp(-jnp.abs(sp_in)))
    bg_ref[...] = jnp.where(lane < B_HEADS, _sigmoid(raw), -jnp.exp(alog_ref[...]) * softplus)


def _inproj(x2, g_mix, w_in_r, conv_w, alog_pad, dtb_pad, seq):
    t = x2.shape[0]
    tm = TOKEN_TILE
    nw = w_in_r.shape[1]
    row = lambda width: pl.BlockSpec((tm, width), lambda i: (i, 0))
    return pl.pallas_call(
        functools.partial(_inproj_kernel, seq // tm),
        grid=(t // tm,),
        in_specs=[row(D_MODEL), _const_spec((1, D_MODEL)), _const_spec((D_MODEL, nw)),
                  _const_spec((CONV_WIDTH, B_CONV_CH)), _const_spec((1, LANES)), _const_spec((1, LANES))],
        out_specs=[row(SPLIT_A), row(B_CONV_CH), row(B_WIDTH), row(LANES)],
        out_shape=[jax.ShapeDtypeStruct((t, SPLIT_A), BF16), jax.ShapeDtypeStruct((t, B_CONV_CH), BF16),
                   jax.ShapeDtypeStruct((t, B_WIDTH), BF16), jax.ShapeDtypeStruct((t, LANES), F32)],
        scratch_shapes=[pltpu.VMEM((tm + HALO, B_CONV_CH), F32)],
        compiler_params=_params("arbitrary"),
        name="inproj",
    )(x2, g_mix, w_in_r, conv_w, alog_pad, dtb_pad)


def _attn_kernel(q_ref, k_ref, v_ref, trow_ref, o_ref, kpad, vpad, bias):
    seq = q_ref.shape[0]
    pad = A_LOOKBACK * CHUNK
    lane = lax.broadcasted_iota(jnp.int32, (Q_GROUP, LANES), 1)
    col = lax.broadcasted_iota(jnp.int32, (Q_GROUP, K_WINDOW), 1)

    @pl.when(pl.program_id(0) == 0)
    def _():
        qc = lax.broadcasted_iota(jnp.int32, (Q_GROUP, K_WINDOW), 0) // CHUNK
        kc = col // CHUNK
        in_band = (kc >= qc) & (kc <= qc + A_LOOKBACK)
        for h in range(A_HEADS):
            base = jnp.broadcast_to(trow_ref[h], (Q_GROUP, ROLL_W))
            toeplitz = pltpu.roll(base, ROLL_W - (Q_GROUP - 1), 1, stride=1, stride_axis=0)
            bias[h] = jnp.where(in_band, toeplitz[:, :K_WINDOW], -1e30)

    kpad[0:pad, :] = jnp.zeros((pad, A_WIDTH), BF16)
    vpad[0:pad, :] = jnp.zeros((pad, A_WIDTH), BF16)
    kpad[pad:, :] = k_ref[...]
    vpad[pad:, :] = v_ref[...]

    def group(g, carry):
        r0 = pl.multiple_of(g * Q_GROUP, Q_GROUP)
        in_seq = col >= pad - g * Q_GROUP
        for hp in range(A_HEADS // 2):
            cols = slice(hp * LANES, (hp + 1) * LANES)
            q2 = q_ref[pl.ds(r0, Q_GROUP), cols]
            k2 = kpad[pl.ds(r0, K_WINDOW), cols]
            v2 = vpad[pl.ds(r0, K_WINDOW), cols]
            outs = []
            for sub in range(2):
                head_lanes = (lane >= sub * A_HEAD_DIM) & (lane < (sub + 1) * A_HEAD_DIM)
                qm = jnp.where(head_lanes, q2, jnp.zeros_like(q2))
                s = _dot_nt(qm, k2)
                s = jnp.where(in_seq, s + bias[2 * hp + sub], -1e30)
                p = jnp.exp(s - jnp.max(s, axis=-1, keepdims=True))
                inv = 1.0 / jnp.sum(p, axis=-1, keepdims=True)
                outs.append(_dot(p.astype(BF16), v2) * inv)
            o_ref[pl.ds(r0, Q_GROUP), cols] = jnp.where(lane < A_HEAD_DIM, outs[0], outs[1]).astype(BF16)
        return carry

    lax.fori_loop(0, seq // Q_GROUP, group, 0)


def _attention(qkva, trow):
    b, seq, _ = qkva.shape
    blk = lambda j: pl.BlockSpec((None, seq, A_WIDTH), lambda i, j=j: (i, 0, j))
    return pl.pallas_call(
        _attn_kernel,
        grid=(b,),
        in_specs=[blk(0), blk(1), blk(2), _const_spec(trow.shape)],
        out_specs=pl.BlockSpec((None, seq, A_WIDTH), lambda i: (i, 0, 0)),
        out_shape=jax.ShapeDtypeStruct((b, seq, A_WIDTH), BF16),
        scratch_shapes=[pltpu.VMEM((seq + A_LOOKBACK * CHUNK, A_WIDTH), BF16)] * 2
        + [pltpu.VMEM((A_HEADS, Q_GROUP, K_WINDOW), F32)],
        compiler_params=_params("arbitrary"),
        name="band_attention",
    )(qkva, qkva, qkva, trow)


def _toeplitz_row(rel_bias):
    n_far = K_WINDOW - 1 - REL_CLIP
    n_near = ROLL_W - n_far - (2 * REL_CLIP + 1)
    far = jnp.broadcast_to(rel_bias[:, -1:], (A_HEADS, n_far))
    near = jnp.broadcast_to(rel_bias[:, :1], (A_HEADS, n_near))
    t = jnp.concatenate([far, rel_bias[:, ::-1], near], axis=1).astype(F32)
    return t.reshape(A_HEADS, 1, ROLL_W)


def _stack(x):
    return jnp.concatenate([x[c * CHUNK:(c + 1) * CHUNK] for c in range(SLAB // CHUNK)], axis=1)


def _stack_col(col, lane_chunk):
    out = jnp.broadcast_to(col[0:CHUNK], (CHUNK, SLAB))
    for c in range(1, SLAB // CHUNK):
        out = jnp.where(lane_chunk == c, jnp.broadcast_to(col[c * CHUNK:(c + 1) * CHUNK], (CHUNK, SLAB)), out)
    return out


def _block_diag(x_st, lane_chunk):
    zero = jnp.zeros_like(x_st)
    return jnp.concatenate([jnp.where(lane_chunk == c, x_st, zero) for c in range(SLAB // CHUNK)], axis=0)


def _delta_kernel(q_ref, k_ref, v_ref, bg_ref, z_ref, wn_ref, o_ref, cs, gl_s, gct, st_ref):
    seq = q_ref.shape[0]
    heads = range(B_HEADS)
    n_c = SLAB // CHUNK

    row_in_chunk = lax.broadcasted_iota(jnp.int32, (seq, LANES), 0) % CHUNK
    cs[0:CHUNK, :] = jnp.zeros((CHUNK, LANES), F32)
    cs[CHUNK:, :] = bg_ref[...]
    shift = 1
    while shift < CHUNK:
        shifted = cs[CHUNK - shift:CHUNK - shift + seq, :]
        cs[CHUNK:, :] = cs[CHUNK:, :] + jnp.where(row_in_chunk >= shift, shifted, 0.0)
        shift *= 2
    gc_seq = cs[CHUNK:, :]
    g3 = gc_seq.reshape(seq // CHUNK, CHUNK, LANES)
    gl_s[...] = jnp.broadcast_to(g3[:, CHUNK - 1:CHUNK, :], g3.shape).reshape(seq, LANES)
    gc_t = gc_seq.T
    for s in range(seq // SLAB):
        gct[s] = gc_t[0:HALO, s * SLAB:(s + 1) * SLAB]
    st_ref[...] = jnp.zeros(st_ref.shape, F32)

    lane_st = lax.broadcasted_iota(jnp.int32, (CHUNK, SLAB), 1)
    i_st = lax.broadcasted_iota(jnp.int32, (CHUNK, SLAB), 0)
    j_st = lane_st % CHUNK
    lane_chunk = lane_st // CHUNK
    incl_st = i_st >= j_st
    strict_st = i_st > j_st
    eye_st = (i_st == j_st).astype(F32)
    level = [strict_st & (((i_st ^ j_st) >> l) == 1) for l in range(6)]
    kbd_mask = (lax.broadcasted_iota(jnp.int32, (SLAB, n_c * B_DIM), 0) // CHUNK
                == lax.broadcasted_iota(jnp.int32, (SLAB, n_c * B_DIM), 1) // B_DIM)

    def slab(s, carry):
        r0 = pl.multiple_of(s * SLAB, SLAB)
        rows = pl.ds(r0, SLAB)
        bg = bg_ref[rows, :]
        gcs = cs[pl.ds(CHUNK + r0, SLAB), :]
        gls = gl_s[rows, :]
        gam_all, kd_all, egl_all = jnp.exp(gcs), jnp.exp(gls - gcs), jnp.exp(gls)
        gct_s = gct[s]
        col = lambda arr, h: arr[:, B_HEADS + h:B_HEADS + h + 1]
        hcols = [slice(h * B_DIM, (h + 1) * B_DIM) for h in heads]

        q = [q_ref[rows, hcols[h]] for h in heads]
        k = [k_ref[rows, hcols[h]] for h in heads]
        kf = [k[h].astype(F32) for h in heads]
        bk = [bg[:, h:h + 1] * kf[h] for h in heads]
        kbd = [jnp.where(kbd_mask, jnp.concatenate([k[h]] * n_c, axis=1), jnp.zeros((), BF16)) for h in heads]
        qk = [_dot_nt(jnp.concatenate([_stack(q[h]), _stack(bk[h].astype(BF16))], axis=0), kbd[h])
              for h in heads]
        decay = [jnp.exp(jnp.where(incl_st, _stack_col(col(gcs, h), lane_chunk)
                                   - gct_s[B_HEADS + h:B_HEADS + h + 1, :], -1e30)) for h in heads]
        pqk = [(qk[h][:CHUNK] * decay[h]).astype(BF16) for h in heads]
        a = [jnp.where(strict_st, qk[h][CHUNK:] * decay[h], 0.0) for h in heads]

        d = [eye_st - jnp.where(level[0], a[h], 0.0) for h in heads]
        for l in range(1, 6):
            dbd = [_block_diag(d[h].astype(BF16), lane_chunk) for h in heads]
            t1 = [_dot(jnp.where(level[l], a[h], 0.0).astype(BF16), dbd[h]) for h in heads]
            t1bd = [_block_diag(t1[h].astype(BF16), lane_chunk) for h in heads]
            d = [d[h] - _dot(d[h].astype(BF16), t1bd[h]) for h in heads]

        rhs = [jnp.concatenate([bg[:, h:h + 1] * v_ref[rows, hcols[h]].astype(F32), col(gam_all, h) * bk[h]],
                               axis=1).astype(BF16) for h in heads]
        sol16 = [_dot(_block_diag(d[h].astype(BF16), lane_chunk), rhs[h]).astype(BF16) for h in heads]
        x2 = [_dot(_block_diag(pqk[h], lane_chunk), sol16[h]) for h in heads]
        qp = [(col(gam_all, h) * q[h].astype(F32) - x2[h][:, B_DIM:]).astype(BF16) for h in heads]
        kdec = [(kf[h] * col(kd_all, h)).astype(BF16) for h in heads]
        chunk_rows = [slice(c * CHUNK, (c + 1) * CHUNK) for c in range(n_c)]
        mnt = [[_dot_tn(kdec[h][cr], sol16[h][cr]) for cr in chunk_rows] for h in heads]

        for c, cr in enumerate(chunk_rows):
            for h in heads:
                st = st_ref[h]
                st16 = st.astype(BF16)
                o = _dot(qp[h][cr], st16) + x2[h][cr, :B_DIM]
                egl = egl_all[c * CHUNK:c * CHUNK + 1, B_HEADS + h:B_HEADS + h + 1]
                st_ref[h] = egl * st - _dot(mnt[h][c][:, B_DIM:].astype(BF16), st16) + mnt[h][c][:, :B_DIM]
                y = _rms(o, wn_ref[...]) * _silu(z_ref[pl.ds(r0 + c * CHUNK, CHUNK), hcols[h]].astype(F32))
                o_ref[pl.ds(r0 + c * CHUNK, CHUNK), hcols[h]] = y.astype(BF16)
        return carry

    lax.fori_loop(0, seq // SLAB, slab, 0)


def _deltanet(cqkv, bg, z, w_onorm):
    b, seq, _ = cqkv.shape
    blk = lambda j: pl.BlockSpec((None, seq, B_WIDTH), lambda i, j=j: (i, 0, j))
    return pl.pallas_call(
        _delta_kernel,
        grid=(b,),
        in_specs=[blk(0), blk(1), blk(2),
                  pl.BlockSpec((None, seq, LANES), lambda i: (i, 0, 0)),
                  pl.BlockSpec((None, seq, B_WIDTH), lambda i: (i, 0, 0)),
                  _const_spec((1, B_DIM))],
        out_specs=pl.BlockSpec((None, seq, B_WIDTH), lambda i: (i, 0, 0)),
        out_shape=jax.ShapeDtypeStruct((b, seq, B_WIDTH), BF16),
        scratch_shapes=[pltpu.VMEM((CHUNK + seq, LANES), F32),
                        pltpu.VMEM((seq, LANES), F32),
                        pltpu.VMEM((seq // SLAB, HALO, SLAB), F32),
                        pltpu.VMEM((B_HEADS, B_DIM, B_DIM), F32)],
        compiler_params=_params("arbitrary"),
        name="gated_deltanet",
    )(cqkv, cqkv, cqkv, bg, z, w_onorm)


def _merge_kernel(x_ref, ya_ref, yb_ref, g_ref, wg_ref, wa_ref, wb_ref, wo_ref, o_ref):
    x = x_ref[...]
    h = _rms(x, g_ref[...]).astype(BF16)
    ga = _sigmoid(_dot(h, wg_ref[:, :D_MODEL]))
    merged = ga * _dot(ya_ref[...], wa_ref[...])
    gb = _sigmoid(_dot(h, wg_ref[:, D_MODEL:]))
    merged = merged + gb * _dot(yb_ref[...], wb_ref[...])
    o_ref[...] = x + _dot(merged.astype(BF16), wo_ref[...])


def _merge(x2, ya, yb, g_mix, w_gates, w_a, w_b, w_out):
    t = x2.shape[0]
    tm = TOKEN_TILE
    row = lambda width: pl.BlockSpec((tm, width), lambda i: (i, 0))
    return pl.pallas_call(
        _merge_kernel,
        grid=(t // tm,),
        in_specs=[row(D_MODEL), row(A_WIDTH), row(B_WIDTH), _const_spec((1, D_MODEL)),
                  _const_spec(w_gates.shape), _const_spec(w_a.shape), _const_spec(w_b.shape),
                  _const_spec(w_out.shape)],
        out_specs=row(D_MODEL),
        out_shape=jax.ShapeDtypeStruct((t, D_MODEL), F32),
        compiler_params=_params("arbitrary"),
        name="gated_merge",
    )(x2, ya, yb, g_mix, w_gates, w_a, w_b, w_out)


def _ffn_kernel(final, x_ref, p_ref, gf_ref, wgu_ref, wd_ref, gp_ref, wpg_ref, wpp_ref, gfin_ref, o_ref, acc):
    x = x_ref[...]
    h = _rms(x, gf_ref[...]).astype(BF16)
    acc[...] = x
    for j in range(D_FF // FF_SLAB):
        gate = _dot(h, wgu_ref[:, j * FF_SLAB:(j + 1) * FF_SLAB])
        up = _dot(h, wgu_ref[:, D_FF + j * FF_SLAB:D_FF + (j + 1) * FF_SLAB])
        act = (_silu(gate) * up).astype(BF16)
        acc[...] += _dot(act, wd_ref[j * FF_SLAB:(j + 1) * FF_SLAB, :])
    x = acc[...]
    hp = _rms(x, gp_ref[...]).astype(BF16)
    ple_gate = _sigmoid(_dot(hp, wpg_ref[...]))
    x = x + ple_gate * _dot(p_ref[...].astype(BF16), wpp_ref[...])
    o_ref[...] = _rms(x, gfin_ref[...]) if final else x


def _ffn(x2, p2, g_ffn, w_gate_up, w_down, g_ple, w_ple_gate, w_ple_proj, g_final, final):
    t = x2.shape[0]
    tm = TOKEN_TILE
    row = lambda width: pl.BlockSpec((tm, width), lambda i: (i, 0))
    return pl.pallas_call(
        functools.partial(_ffn_kernel, final),
        grid=(t // tm,),
        in_specs=[row(D_MODEL), row(PLE_DIM), _const_spec((1, D_MODEL)),
                  _const_spec(w_gate_up.shape), _const_spec(w_down.shape), _const_spec((1, D_MODEL)),
                  _const_spec(w_ple_gate.shape), _const_spec(w_ple_proj.shape), _const_spec((1, D_MODEL))],
        out_specs=row(D_MODEL),
        out_shape=jax.ShapeDtypeStruct((t, D_MODEL), F32),
        scratch_shapes=[pltpu.VMEM((tm, D_MODEL), F32)],
        compiler_params=_params("arbitrary"),
        name="ffn_ple_final",
    )(x2, p2, g_ffn, w_gate_up, w_down, g_ple, w_ple_gate, w_ple_proj, g_final)


def _layer(final, x2, p2, seq, g_mix, w_in, conv_w, a_log, dt_bias, rel_bias, w_onorm, w_branch_a, w_branch_b,
           w_out, g_ffn, w_gate_up, w_down, g_ple, w_ple_gate, w_ple_proj, g_final):
    t = x2.shape[0]
    b = t // seq
    row = lambda v: v.reshape(1, -1).astype(F32)
    w_bd = jnp.pad(w_in[:, SPLIT_Z:SPLIT_DECAY], ((0, 0), (0, LANES - 2 * B_HEADS)))
    w_in_r = jnp.concatenate([w_in[:, :SPLIT_Z], w_bd], axis=1).astype(BF16)
    w_gates = w_in[:, SPLIT_DECAY:].astype(BF16)
    head_pad = lambda v: jnp.pad(v.astype(F32), (B_HEADS, LANES - 2 * B_HEADS)).reshape(1, LANES)

    qkva, cqkv, z, bg = _inproj(x2, row(g_mix), w_in_r, conv_w.astype(F32), head_pad(a_log),
                                head_pad(dt_bias), seq)
    ya = _attention(qkva.reshape(b, seq, SPLIT_A), _toeplitz_row(rel_bias))
    yb = _deltanet(cqkv.reshape(b, seq, B_CONV_CH), bg.reshape(b, seq, LANES),
                   z.reshape(b, seq, B_WIDTH), row(w_onorm))
    x2 = _merge(x2, ya.reshape(t, A_WIDTH), yb.reshape(t, B_WIDTH), row(g_mix), w_gates,
                w_branch_a.astype(BF16), w_branch_b.astype(BF16), w_out.astype(BF16))
    return _ffn(x2, p2, row(g_ffn), w_gate_up.astype(BF16), w_down.astype(BF16), row(g_ple),
                w_ple_gate.astype(BF16), w_ple_proj.astype(BF16), row(g_final), final)


def kernel(x, p, g_mix, w_in, conv_w, a_log, dt_bias, rel_bias, w_onorm, w_branch_a, w_branch_b, w_out,
           g_ffn, w_gate_up, w_down, g_ple, w_ple_gate, w_ple_proj, g_final):
    b, seq, _ = x.shape
    depth = p.shape[0]
    x2 = x.reshape(b * seq, D_MODEL)
    for i in range(depth):
        x2 = _layer(i == depth - 1, x2, p[i].reshape(b * seq, PLE_DIM), seq, g_mix[i], w_in[i], conv_w[i],
                    a_log[i], dt_bias[i], rel_bias[i], w_onorm[i], w_branch_a[i], w_branch_b[i], w_out[i],
                    g_ffn[i], w_gate_up[i], w_down[i], g_ple[i], w_ple_gate[i], w_ple_proj[i], g_final)
    return x2.reshape(b, seq, D_MODEL)
```

```python
import functools

import numpy as np
import jax
import jax.numpy as jnp
from jax import lax
from jax.experimental import pallas as pl
from jax.experimental.pallas import tpu as pltpu

D_MODEL = 1024
CHUNK = 64
PLE_DIM = 256
EPS = 1e-6

A_HEADS = 8
A_HEAD_DIM = 64
A_WIDTH = A_HEADS * A_HEAD_DIM
A_LOOKBACK = 8
REL_CLIP = 128

B_HEADS = 4
B_DIM = 128
B_WIDTH = B_HEADS * B_DIM
CONV_WIDTH = 4
B_CONV_CH = 3 * B_WIDTH

D_FF = 2816

SPLIT_A = 3 * A_WIDTH
SPLIT_CONV = SPLIT_A + B_CONV_CH
SPLIT_Z = SPLIT_CONV + B_WIDTH
SPLIT_BETA = SPLIT_Z + B_HEADS
SPLIT_DECAY = SPLIT_BETA + B_HEADS

LANES = 128
MXU_COLS = 256
HALO = 8
TOKEN_TILE = 512
Q_GROUP = 256
K_WINDOW = Q_GROUP + A_LOOKBACK * CHUNK
ROLL_W = 1024
SLAB = 256
FF_SLAB = 256
VMEM_LIMIT = 56 * 1024 * 1024

F32 = jnp.float32
BF16 = jnp.bfloat16
NT_DIMS = (((1,), (1,)), ((), ()))
TN_DIMS = (((0,), (0,)), ((), ()))


def _dot(a, b):
    return jnp.dot(a, b, preferred_element_type=F32)


def _dot_nt(a, b):
    return lax.dot_general(a, b, NT_DIMS, preferred_element_type=F32)


def _dot_tn(a, b):
    return lax.dot_general(a, b, TN_DIMS, preferred_element_type=F32)


def _rms(x, g):
    return x * lax.rsqrt(jnp.mean(x * x, axis=-1, keepdims=True) + EPS) * g


def _sigmoid(x):
    return 0.5 * jnp.tanh(0.5 * x) + 0.5


def _silu(x):
    h = 0.5 * x
    return h * jnp.tanh(h) + h


def _params(*sem):
    return pltpu.CompilerParams(dimension_semantics=sem, vmem_limit_bytes=VMEM_LIMIT)


def _const_spec(shape):
    nd = len(shape)
    return pl.BlockSpec(shape, lambda *_: (0,) * nd, pipeline_mode=pl.Buffered(1))


def _inproj_kernel(tiles_per_seq, x_ref, g_ref, w_ref, cw_ref, alog_ref, dtb_ref,
                   qkva_ref, cqkv_ref, z_ref, bg_ref, cbuf):
    tm = x_ref.shape[0]
    i = pl.program_id(0)
    h = _rms(x_ref[...], g_ref[...]).astype(BF16)

    @pl.when(i % tiles_per_seq == 0)
    def _():
        cbuf[0:HALO, :] = jnp.zeros((HALO, B_CONV_CH), F32)

    @pl.when(i % tiles_per_seq != 0)
    def _():
        cbuf[0:HALO, :] = cbuf[tm:tm + HALO, :]

    plain = [(qkva_ref, 0, j) for j in range(SPLIT_A // MXU_COLS)]
    plain += [(z_ref, SPLIT_CONV, j) for j in range(B_WIDTH // MXU_COLS)]

    def plain_slab(out_ref, w_col0, j):
        cols = slice(j * MXU_COLS, (j + 1) * MXU_COLS)
        out_ref[:, cols] = _dot(h, w_ref[:, w_col0 + j * MXU_COLS:w_col0 + (j + 1) * MXU_COLS]).astype(BF16)

    for j in range(B_CONV_CH // MXU_COLS):
        cols = slice(j * MXU_COLS, (j + 1) * MXU_COLS)
        cbuf[HALO:, cols] = _dot(h, w_ref[:, SPLIT_A + j * MXU_COLS:SPLIT_A + (j + 1) * MXU_COLS])
        plain_slab(*plain.pop(0))
        xs = cbuf[:, cols]
        w = [0.5 * cw_ref[t:t + 1, cols] for t in range(CONV_WIDTH)]
        xs1 = pltpu.roll(xs, 1, 0)
        hc = (pltpu.roll(w[0] * xs1 + w[1] * xs, 2, 0) + (w[2] * xs1 + w[3] * xs))[HALO:]
        c = hc * jnp.tanh(hc) + hc
        for half in range(MXU_COLS // LANES):
            head = j * (MXU_COLS // LANES) + half
            ch = c[:, half * LANES:(half + 1) * LANES]
            if head < 2 * B_HEADS:
                scale = B_DIM ** -0.5 if head < B_HEADS else 1.0
                ch = ch * (lax.rsqrt(jnp.sum(ch * ch, axis=-1, keepdims=True) + EPS) * scale)
            cqkv_ref[:, head * LANES:(head + 1) * LANES] = ch.astype(BF16)

    raw = _dot(h, w_ref[:, SPLIT_Z:SPLIT_Z + LANES])
    lane = lax.broadcasted_iota(jnp.int32, raw.shape, 1)
    sp_in = raw + dtb_ref[...]
    softplus = jnp.maximum(sp_in, 0.0) + jnp.log1p(jnp.exp(-jnp.abs(sp_in)))
    bg_ref[...] = jnp.where(lane < B_HEADS, _sigmoid(raw), -jnp.exp(alog_ref[...]) * softplus)

    for args in plain:
        plain_slab(*args)


def _inproj(x2, g_mix, w_in_r, conv_w, alog_pad, dtb_pad, seq):
    t = x2.shape[0]
    tm = TOKEN_TILE
    nw = w_in_r.shape[1]
    row = lambda width: pl.BlockSpec((tm, width), lambda i: (i, 0))
    return pl.pallas_call(
        functools.partial(_inproj_kernel, seq // tm),
        grid=(t // tm,),
        in_specs=[row(D_MODEL), _const_spec((1, D_MODEL)), _const_spec((D_MODEL, nw)),
                  _const_spec((CONV_WIDTH, B_CONV_CH)), _const_spec((1, LANES)), _const_spec((1, LANES))],
        out_specs=[row(SPLIT_A), row(B_CONV_CH), row(B_WIDTH), row(LANES)],
        out_shape=[jax.ShapeDtypeStruct((t, SPLIT_A), BF16), jax.ShapeDtypeStruct((t, B_CONV_CH), BF16),
                   jax.ShapeDtypeStruct((t, B_WIDTH), BF16), jax.ShapeDtypeStruct((t, LANES), F32)],
        scratch_shapes=[pltpu.VMEM((tm + HALO, B_CONV_CH), F32)],
        compiler_params=_params("arbitrary"),
        name="inproj",
    )(x2, g_mix, w_in_r, conv_w, alog_pad, dtb_pad)


def _attn_kernel(q_ref, k_ref, v_ref, trow_ref, o_ref, kpad, vpad, bias):
    seq = q_ref.shape[0]
    pad = A_LOOKBACK * CHUNK
    lane = lax.broadcasted_iota(jnp.int32, (Q_GROUP, LANES), 1)
    col = lax.broadcasted_iota(jnp.int32, (Q_GROUP, K_WINDOW), 1)

    @pl.when(pl.program_id(0) == 0)
    def _():
        qc = lax.broadcasted_iota(jnp.int32, (Q_GROUP, K_WINDOW), 0) // CHUNK
        kc = col // CHUNK
        in_band = (kc >= qc) & (kc <= qc + A_LOOKBACK)
        for h in range(A_HEADS):
            base = jnp.broadcast_to(trow_ref[h], (Q_GROUP, ROLL_W))
            toeplitz = pltpu.roll(base, ROLL_W - (Q_GROUP - 1), 1, stride=1, stride_axis=0)
            bias[h] = jnp.where(in_band, toeplitz[:, :K_WINDOW], -1e30)

    kpad[0:pad, :] = jnp.zeros((pad, A_WIDTH), BF16)
    vpad[0:pad, :] = jnp.zeros((pad, A_WIDTH), BF16)
    kpad[pad:, :] = k_ref[...]
    vpad[pad:, :] = v_ref[...]

    def group(g, carry):
        r0 = pl.multiple_of(g * Q_GROUP, Q_GROUP)
        in_seq = col >= pad - g * Q_GROUP
        for hp in range(A_HEADS // 2):
            cols = slice(hp * LANES, (hp + 1) * LANES)
            q2 = q_ref[pl.ds(r0, Q_GROUP), cols]
            k2 = kpad[pl.ds(r0, K_WINDOW), cols]
            v2 = vpad[pl.ds(r0, K_WINDOW), cols]
            outs = []
            for sub in range(2):
                head_lanes = (lane >= sub * A_HEAD_DIM) & (lane < (sub + 1) * A_HEAD_DIM)
                qm = jnp.where(head_lanes, q2, jnp.zeros_like(q2))
                s = _dot_nt(qm, k2)
                s = jnp.where(in_seq, s + bias[2 * hp + sub], -1e30)
                p = jnp.exp(s - jnp.max(s, axis=-1, keepdims=True))
                inv = 1.0 / jnp.sum(p, axis=-1, keepdims=True)
                outs.append(_dot(p.astype(BF16), v2) * inv)
            o_ref[pl.ds(r0, Q_GROUP), cols] = jnp.where(lane < A_HEAD_DIM, outs[0], outs[1]).astype(BF16)
        return carry

    lax.fori_loop(0, seq // Q_GROUP, group, 0)


def _attention(qkva, trow):
    b, seq, _ = qkva.shape
    blk = lambda j: pl.BlockSpec((None, seq, A_WIDTH), lambda i, j=j: (i, 0, j))
    return pl.pallas_call(
        _attn_kernel,
        grid=(b,),
        in_specs=[blk(0), blk(1), blk(2), _const_spec(trow.shape)],
        out_specs=pl.BlockSpec((None, seq, A_WIDTH), lambda i: (i, 0, 0)),
        out_shape=jax.ShapeDtypeStruct((b, seq, A_WIDTH), BF16),
        scratch_shapes=[pltpu.VMEM((seq + A_LOOKBACK * CHUNK, A_WIDTH), BF16)] * 2
        + [pltpu.VMEM((A_HEADS, Q_GROUP, K_WINDOW), F32)],
        compiler_params=_params("arbitrary"),
        name="band_attention",
    )(qkva, qkva, qkva, trow)


def _toeplitz_row(rel_bias):
    n_far = K_WINDOW - 1 - REL_CLIP
    n_near = ROLL_W - n_far - (2 * REL_CLIP + 1)
    far = jnp.broadcast_to(rel_bias[:, -1:], (A_HEADS, n_far))
    near = jnp.broadcast_to(rel_bias[:, :1], (A_HEADS, n_near))
    t = jnp.concatenate([far, rel_bias[:, ::-1], near], axis=1).astype(F32)
    return t.reshape(A_HEADS, 1, ROLL_W)


def _stack(x):
    return jnp.concatenate([x[c * CHUNK:(c + 1) * CHUNK] for c in range(SLAB // CHUNK)], axis=1)


def _stack_col(col, lane_chunk):
    out = jnp.broadcast_to(col[0:CHUNK], (CHUNK, SLAB))
    for c in range(1, SLAB // CHUNK):
        out = jnp.where(lane_chunk == c, jnp.broadcast_to(col[c * CHUNK:(c + 1) * CHUNK], (CHUNK, SLAB)), out)
    return out


def _block_diag(x_st, lane_chunk):
    zero = jnp.zeros_like(x_st)
    return jnp.concatenate([jnp.where(lane_chunk == c, x_st, zero) for c in range(SLAB // CHUNK)], axis=0)


def _delta_kernel(q_ref, k_ref, v_ref, bg_ref, z_ref, wn_ref, o_ref, cs, gl_s, gct, st_ref):
    seq = q_ref.shape[0]
    heads = range(B_HEADS)
    n_c = SLAB // CHUNK

    row_in_chunk = lax.broadcasted_iota(jnp.int32, (seq, LANES), 0) % CHUNK
    cs[0:CHUNK, :] = jnp.zeros((CHUNK, LANES), F32)
    cs[CHUNK:, :] = bg_ref[...]
    shift = 1
    while shift < CHUNK:
        shifted = cs[CHUNK - shift:CHUNK - shift + seq, :]
        cs[CHUNK:, :] = cs[CHUNK:, :] + jnp.where(row_in_chunk >= shift, shifted, 0.0)
        shift *= 2
    gc_seq = cs[CHUNK:, :]
    g3 = gc_seq.reshape(seq // CHUNK, CHUNK, LANES)
    gl_s[...] = jnp.broadcast_to(g3[:, CHUNK - 1:CHUNK, :], g3.shape).reshape(seq, LANES)
    gc_t = gc_seq.T
    for s in range(seq // SLAB):
        gct[s] = gc_t[0:HALO, s * SLAB:(s + 1) * SLAB]
    st_ref[...] = jnp.zeros(st_ref.shape, F32)

    lane_st = lax.broadcasted_iota(jnp.int32, (CHUNK, SLAB), 1)
    i_st = lax.broadcasted_iota(jnp.int32, (CHUNK, SLAB), 0)
    j_st = lane_st % CHUNK
    lane_chunk = lane_st // CHUNK
    incl_st = i_st >= j_st
    strict_st = i_st > j_st
    eye_st = (i_st == j_st).astype(F32)
    level = [strict_st & (((i_st ^ j_st) >> l) == 1) for l in range(6)]
    kbd_mask = (lax.broadcasted_iota(jnp.int32, (SLAB, n_c * B_DIM), 0) // CHUNK
                == lax.broadcasted_iota(jnp.int32, (SLAB, n_c * B_DIM), 1) // B_DIM)

    def slab(s, carry):
        r0 = pl.multiple_of(s * SLAB, SLAB)
        rows = pl.ds(r0, SLAB)
        bg = bg_ref[rows, :]
        gcs = cs[pl.ds(CHUNK + r0, SLAB), :]
        gls = gl_s[rows, :]
        gam_all, kd_all, egl_all = jnp.exp(gcs), jnp.exp(gls - gcs), jnp.exp(gls)
        gct_s = gct[s]
        col = lambda arr, h: arr[:, B_HEADS + h:B_HEADS + h + 1]
        hcols = [slice(h * B_DIM, (h + 1) * B_DIM) for h in heads]

        q = [q_ref[rows, hcols[h]] for h in heads]
        k = [k_ref[rows, hcols[h]] for h in heads]
        kf = [k[h].astype(F32) for h in heads]
        bk = [bg[:, h:h + 1] * kf[h] for h in heads]
        kbd = [jnp.where(kbd_mask, jnp.concatenate([k[h]] * n_c, axis=1), jnp.zeros((), BF16)) for h in heads]
        qk = [_dot_nt(jnp.concatenate([_stack(q[h]), _stack(bk[h].astype(BF16))], axis=0), kbd[h])
              for h in heads]
        decay = [jnp.exp(jnp.where(incl_st, _stack_col(col(gcs, h), lane_chunk)
                                   - gct_s[B_HEADS + h:B_HEADS + h + 1, :], -1e30)) for h in heads]
        pqk = [(qk[h][:CHUNK] * decay[h]).astype(BF16) for h in heads]
        a = [jnp.where(strict_st, qk[h][CHUNK:] * decay[h], 0.0) for h in heads]

        d = [eye_st - jnp.where(level[0], a[h], 0.0) for h in heads]
        for l in range(1, 6):
            dbd = [_block_diag(d[h].astype(BF16), lane_chunk) for h in heads]
            t1 = [_dot(jnp.where(level[l], a[h], 0.0).astype(BF16), dbd[h]) for h in heads]
            t1bd = [_block_diag(t1[h].astype(BF16), lane_chunk) for h in heads]
            d = [d[h] - _dot(d[h].astype(BF16), t1bd[h]) for h in heads]

        rhs = [jnp.concatenate([bg[:, h:h + 1] * v_ref[rows, hcols[h]].astype(F32), col(gam_all, h) * bk[h]],
                               axis=1).astype(BF16) for h in heads]
        sol16 = [_dot(_block_diag(d[h].astype(BF16), lane_chunk), rhs[h]).astype(BF16) for h in heads]
        x2 = [_dot(_block_diag(pqk[h], lane_chunk), sol16[h]) for h in heads]
        qp = [(col(gam_all, h) * q[h].astype(F32) - x2[h][:, B_DIM:]).astype(BF16) for h in heads]
        kdec = [(kf[h] * col(kd_all, h)).astype(BF16) for h in heads]
        chunk_rows = [slice(c * CHUNK, (c + 1) * CHUNK) for c in range(n_c)]
        mnt = [[_dot_tn(kdec[h][cr], sol16[h][cr]) for cr in chunk_rows] for h in heads]

        for c, cr in enumerate(chunk_rows):
            for h in heads:
                st = st_ref[h]
                st16 = st.astype(BF16)
                o = _dot(qp[h][cr], st16) + x2[h][cr, :B_DIM]
                egl = egl_all[c * CHUNK:c * CHUNK + 1, B_HEADS + h:B_HEADS + h + 1]
                st_ref[h] = egl * st - _dot(mnt[h][c][:, B_DIM:].astype(BF16), st16) + mnt[h][c][:, :B_DIM]
                y = _rms(o, wn_ref[...]) * _silu(z_ref[pl.ds(r0 + c * CHUNK, CHUNK), hcols[h]].astype(F32))
                o_ref[pl.ds(r0 + c * CHUNK, CHUNK), hcols[h]] = y.astype(BF16)
        return carry

    lax.fori_loop(0, seq // SLAB, slab, 0)


def _deltanet(cqkv, bg, z, w_onorm):
    b, seq, _ = cqkv.shape
    blk = lambda j: pl.BlockSpec((None, seq, B_WIDTH), lambda i, j=j: (i, 0, j))
    return pl.pallas_call(
        _delta_kernel,
        grid=(b,),
        in_specs=[blk(0), blk(1), blk(2),
                  pl.BlockSpec((None, seq, LANES), lambda i: (i, 0, 0)),
                  pl.BlockSpec((None, seq, B_WIDTH), lambda i: (i, 0, 0)),
                  _const_spec((1, B_DIM))],
        out_specs=pl.BlockSpec((None, seq, B_WIDTH), lambda i: (i, 0, 0)),
        out_shape=jax.ShapeDtypeStruct((b, seq, B_WIDTH), BF16),
        scratch_shapes=[pltpu.VMEM((CHUNK + seq, LANES), F32),
                        pltpu.VMEM((seq, LANES), F32),
                        pltpu.VMEM((seq // SLAB, HALO, SLAB), F32),
                        pltpu.VMEM((B_HEADS, B_DIM, B_DIM), F32)],
        compiler_params=_params("arbitrary"),
        name="gated_deltanet",
    )(cqkv, cqkv, cqkv, bg, z, w_onorm)


def _merge_kernel(x_ref, ya_ref, yb_ref, g_ref, wg_ref, wa_ref, wb_ref, wo_ref, o_ref):
    x = x_ref[...]
    h = _rms(x, g_ref[...]).astype(BF16)
    ga = _sigmoid(_dot(h, wg_ref[:, :D_MODEL]))
    merged = ga * _dot(ya_ref[...], wa_ref[...])
    gb = _sigmoid(_dot(h, wg_ref[:, D_MODEL:]))
    merged = merged + gb * _dot(yb_ref[...], wb_ref[...])
    o_ref[...] = x + _dot(merged.astype(BF16), wo_ref[...])


def _merge(x2, ya, yb, g_mix, w_gates, w_a, w_b, w_out):
    t = x2.shape[0]
    tm = TOKEN_TILE
    row = lambda width: pl.BlockSpec((tm, width), lambda i: (i, 0))
    return pl.pallas_call(
        _merge_kernel,
        grid=(t // tm,),
        in_specs=[row(D_MODEL), row(A_WIDTH), row(B_WIDTH), _const_spec((1, D_MODEL)),
                  _const_spec(w_gates.shape), _const_spec(w_a.shape), _const_spec(w_b.shape),
                  _const_spec(w_out.shape)],
        out_specs=row(D_MODEL),
        out_shape=jax.ShapeDtypeStruct((t, D_MODEL), F32),
        compiler_params=_params("arbitrary"),
        name="gated_merge",
    )(x2, ya, yb, g_mix, w_gates, w_a, w_b, w_out)


def _ffn_kernel(final, x_ref, p_ref, gf_ref, wgu_ref, wd_ref, gp_ref, wpg_ref, wpp_ref, gfin_ref, o_ref, acc):
    x = x_ref[...]
    h = _rms(x, gf_ref[...]).astype(BF16)
    acc[...] = x
    for j in range(D_FF // FF_SLAB):
        gate = _dot(h, wgu_ref[:, j * FF_SLAB:(j + 1) * FF_SLAB])
        up = _dot(h, wgu_ref[:, D_FF + j * FF_SLAB:D_FF + (j + 1) * FF_SLAB])
        act = (_silu(gate) * up).astype(BF16)
        acc[...] += _dot(act, wd_ref[j * FF_SLAB:(j + 1) * FF_SLAB, :])
    x = acc[...]
    hp = _rms(x, gp_ref[...]).astype(BF16)
    ple_gate = _sigmoid(_dot(hp, wpg_ref[...]))
    x = x + ple_gate * _dot(p_ref[...].astype(BF16), wpp_ref[...])
    o_ref[...] = _rms(x, gfin_ref[...]) if final else x


def _ffn(x2, p2, g_ffn, w_gate_up, w_down, g_ple, w_ple_gate, w_ple_proj, g_final, final):
    t = x2.shape[0]
    tm = TOKEN_TILE
    row = lambda width: pl.BlockSpec((tm, width), lambda i: (i, 0))
    return pl.pallas_call(
        functools.partial(_ffn_kernel, final),
        grid=(t // tm,),
        in_specs=[row(D_MODEL), row(PLE_DIM), _const_spec((1, D_MODEL)),
                  _const_spec(w_gate_up.shape), _const_spec(w_down.shape), _const_spec((1, D_MODEL)),
                  _const_spec(w_ple_gate.shape), _const_spec(w_ple_proj.shape), _const_spec((1, D_MODEL))],
        out_specs=row(D_MODEL),
        out_shape=jax.ShapeDtypeStruct((t, D_MODEL), F32),
        scratch_shapes=[pltpu.VMEM((tm, D_MODEL), F32)],
        compiler_params=_params("arbitrary"),
        name="ffn_ple_final",
    )(x2, p2, g_ffn, w_gate_up, w_down, g_ple, w_ple_gate, w_ple_proj, g_final)


def _layer(final, x2, p2, seq, g_mix, w_in, conv_w, a_log, dt_bias, rel_bias, w_onorm, w_branch_a, w_branch_b,
           w_out, g_ffn, w_gate_up, w_down, g_ple, w_ple_gate, w_ple_proj, g_final):
    t = x2.shape[0]
    b = t // seq
    row = lambda v: v.reshape(1, -1).astype(F32)
    w_bd = jnp.pad(w_in[:, SPLIT_Z:SPLIT_DECAY], ((0, 0), (0, LANES - 2 * B_HEADS)))
    w_in_r = jnp.concatenate([w_in[:, :A_WIDTH] * (A_HEAD_DIM ** -0.5), w_in[:, A_WIDTH:SPLIT_Z], w_bd],
                             axis=1).astype(BF16)
    w_gates = w_in[:, SPLIT_DECAY:].astype(BF16)
    head_pad = lambda v: jnp.pad(v.astype(F32), (B_HEADS, LANES - 2 * B_HEADS)).reshape(1, LANES)

    qkva, cqkv, z, bg = _inproj(x2, row(g_mix), w_in_r, conv_w.astype(F32), head_pad(a_log),
                                head_pad(dt_bias), seq)
    ya = _attention(qkva.reshape(b, seq, SPLIT_A), _toeplitz_row(rel_bias))
    yb = _deltanet(cqkv.reshape(b, seq, B_CONV_CH), bg.reshape(b, seq, LANES),
                   z.reshape(b, seq, B_WIDTH), row(w_onorm))
    x2 = _merge(x2, ya.reshape(t, A_WIDTH), yb.reshape(t, B_WIDTH), row(g_mix), w_gates,
                w_branch_a.astype(BF16), w_branch_b.astype(BF16), w_out.astype(BF16))
    return _ffn(x2, p2, row(g_ffn), w_gate_up.astype(BF16), w_down.astype(BF16), row(g_ple),
                w_ple_gate.astype(BF16), w_ple_proj.astype(BF16), row(g_final), final)


def kernel(x, p, g_mix, w_in, conv_w, a_log, dt_bias, rel_bias, w_onorm, w_branch_a, w_branch_b, w_out,
           g_ffn, w_gate_up, w_down, g_ple, w_ple_gate, w_ple_proj, g_final):
    b, seq, _ = x.shape
    depth = p.shape[0]
    x2 = x.reshape(b * seq, D_MODEL)
    for i in range(depth):
        x2 = _layer(i == depth - 1, x2, p[i].reshape(b * seq, PLE_DIM), seq, g_mix[i], w_in[i], conv_w[i],
                    a_log[i], dt_bias[i], rel_bias[i], w_onorm[i], w_branch_a[i], w_branch_b[i], w_out[i],
                    g_ffn[i], w_gate_up[i], w_down[i], g_ple[i], w_ple_gate[i], w_ple_proj[i], g_final)
    return x2.reshape(b, seq, D_MODEL)
```

```python
import functools

import numpy as np
import jax
import jax.numpy as jnp
from jax import lax
from jax.experimental import pallas as pl
from jax.experimental.pallas import tpu as pltpu

D_MODEL = 1024
CHUNK = 64
PLE_DIM = 256
EPS = 1e-6

A_HEADS = 8
A_HEAD_DIM = 64
A_WIDTH = A_HEADS * A_HEAD_DIM
A_LOOKBACK = 8
REL_CLIP = 128

B_HEADS = 4
B_DIM = 128
B_WIDTH = B_HEADS * B_DIM
CONV_WIDTH = 4
B_CONV_CH = 3 * B_WIDTH

D_FF = 2816

SPLIT_A = 3 * A_WIDTH
SPLIT_CONV = SPLIT_A + B_CONV_CH
SPLIT_Z = SPLIT_CONV + B_WIDTH
SPLIT_BETA = SPLIT_Z + B_HEADS
SPLIT_DECAY = SPLIT_BETA + B_HEADS

LANES = 128
MXU_COLS = 256
HALO = 8
TOKEN_TILE = 512
Q_GROUP = 256
K_WINDOW = Q_GROUP + A_LOOKBACK * CHUNK
ROLL_W = 1024
SLAB = 256
FF_SLAB = 256
VMEM_LIMIT = 56 * 1024 * 1024

F32 = jnp.float32
BF16 = jnp.bfloat16
NT_DIMS = (((1,), (1,)), ((), ()))
TN_DIMS = (((0,), (0,)), ((), ()))


def _dot(a, b):
    return jnp.dot(a, b, preferred_element_type=F32)


def _dot_nt(a, b):
    return lax.dot_general(a, b, NT_DIMS, preferred_element_type=F32)


def _dot_tn(a, b):
    return lax.dot_general(a, b, TN_DIMS, preferred_element_type=F32)


def _rms(x, g):
    return x * lax.rsqrt(jnp.mean(x * x, axis=-1, keepdims=True) + EPS) * g


def _sigmoid(x):
    return 0.5 * jnp.tanh(0.5 * x) + 0.5


def _silu(x):
    h = 0.5 * x
    return h * jnp.tanh(h) + h


def _params(*sem):
    return pltpu.CompilerParams(dimension_semantics=sem, vmem_limit_bytes=VMEM_LIMIT)


def _const_spec(shape):
    nd = len(shape)
    return pl.BlockSpec(shape, lambda *_: (0,) * nd, pipeline_mode=pl.Buffered(1))


def _inproj_kernel(tiles_per_seq, x_ref, g_ref, w_ref, cw_ref, alog_ref, dtb_ref,
                   qkva_ref, cqkv_ref, z_ref, bg_ref, cbuf):
    tm = x_ref.shape[0]
    i = pl.program_id(0)
    h = _rms(x_ref[...], g_ref[...]).astype(BF16)

    @pl.when(i % tiles_per_seq == 0)
    def _():
        cbuf[0:HALO, :] = jnp.zeros((HALO, B_CONV_CH), F32)

    @pl.when(i % tiles_per_seq != 0)
    def _():
        cbuf[0:HALO, :] = cbuf[tm:tm + HALO, :]

    plain = [(qkva_ref, 0, j) for j in range(SPLIT_A // MXU_COLS)]
    plain += [(z_ref, SPLIT_CONV, j) for j in range(B_WIDTH // MXU_COLS)]

    def plain_slab(out_ref, w_col0, j):
        cols = slice(j * MXU_COLS, (j + 1) * MXU_COLS)
        out_ref[:, cols] = _dot(h, w_ref[:, w_col0 + j * MXU_COLS:w_col0 + (j + 1) * MXU_COLS]).astype(BF16)

    for j in range(B_CONV_CH // MXU_COLS):
        cols = slice(j * MXU_COLS, (j + 1) * MXU_COLS)
        cbuf[HALO:, cols] = _dot(h, w_ref[:, SPLIT_A + j * MXU_COLS:SPLIT_A + (j + 1) * MXU_COLS])
        plain_slab(*plain.pop(0))
        xs = cbuf[:, cols]
        w = [0.5 * cw_ref[t:t + 1, cols] for t in range(CONV_WIDTH)]
        xs1 = pltpu.roll(xs, 1, 0)
        hc = (pltpu.roll(w[0] * xs1 + w[1] * xs, 2, 0) + (w[2] * xs1 + w[3] * xs))[HALO:]
        c = hc * jnp.tanh(hc) + hc
        for half in range(MXU_COLS // LANES):
            head = j * (MXU_COLS // LANES) + half
            ch = c[:, half * LANES:(half + 1) * LANES]
            if head < 2 * B_HEADS:
                scale = B_DIM ** -0.5 if head < B_HEADS else 1.0
                ch = ch * (lax.rsqrt(jnp.sum(ch * ch, axis=-1, keepdims=True) + EPS) * scale)
            cqkv_ref[:, head * LANES:(head + 1) * LANES] = ch.astype(BF16)

    raw = _dot(h, w_ref[:, SPLIT_Z:SPLIT_Z + LANES])
    lane = lax.broadcasted_iota(jnp.int32, raw.shape, 1)
    sp_in = raw + dtb_ref[...]
    softplus = jnp.maximum(sp_in, 0.0) + jnp.log1p(jnp.exp(-jnp.abs(sp_in)))
    bg_ref[...] = jnp.where(lane < B_HEADS, _sigmoid(raw), -jnp.exp(alog_ref[...]) * softplus)

    for args in plain:
        plain_slab(*args)


def _inproj(x2, g_mix, w_in_r, conv_w, alog_pad, dtb_pad, seq):
    t = x2.shape[0]
    tm = TOKEN_TILE
    nw = w_in_r.shape[1]
    row = lambda width: pl.BlockSpec((tm, width), lambda i: (i, 0))
    return pl.pallas_call(
        functools.partial(_inproj_kernel, seq // tm),
        grid=(t // tm,),
        in_specs=[row(D_MODEL), _const_spec((1, D_MODEL)), _const_spec((D_MODEL, nw)),
                  _const_spec((CONV_WIDTH, B_CONV_CH)), _const_spec((1, LANES)), _const_spec((1, LANES))],
        out_specs=[row(SPLIT_A), row(B_CONV_CH), row(B_WIDTH), row(LANES)],
        out_shape=[jax.ShapeDtypeStruct((t, SPLIT_A), BF16), jax.ShapeDtypeStruct((t, B_CONV_CH), BF16),
                   jax.ShapeDtypeStruct((t, B_WIDTH), BF16), jax.ShapeDtypeStruct((t, LANES), F32)],
        scratch_shapes=[pltpu.VMEM((tm + HALO, B_CONV_CH), F32)],
        compiler_params=_params("arbitrary"),
        name="inproj",
    )(x2, g_mix, w_in_r, conv_w, alog_pad, dtb_pad)


def _attn_kernel(q_ref, k_ref, v_ref, trow_ref, o_ref, bias, s_scr):
    seq = q_ref.shape[0]
    pad = A_LOOKBACK * CHUNK
    lane = lax.broadcasted_iota(jnp.int32, (Q_GROUP, LANES), 1)
    col = lax.broadcasted_iota(jnp.int32, (Q_GROUP, K_WINDOW), 1)

    @pl.when(pl.program_id(0) == 0)
    def _():
        qc = lax.broadcasted_iota(jnp.int32, (Q_GROUP, K_WINDOW), 0) // CHUNK
        kc = col // CHUNK
        in_band = (kc >= qc) & (kc <= qc + A_LOOKBACK)
        for h in range(A_HEADS):
            base = jnp.broadcast_to(trow_ref[h], (Q_GROUP, ROLL_W))
            toeplitz = pltpu.roll(base, ROLL_W - (Q_GROUP - 1), 1, stride=1, stride_axis=0)
            bias[h] = jnp.where(in_band, toeplitz[:, :K_WINDOW], -1e30)

    def group(q0, k0, n_keys):
        b0 = K_WINDOW - n_keys
        pair_cols = lambda head: slice(head // 2 * LANES, (head // 2 + 1) * LANES)

        def scores(head):
            sub = head % 2
            q2 = q_ref[pl.ds(q0, Q_GROUP), pair_cols(head)]
            head_lanes = (lane >= sub * A_HEAD_DIM) & (lane < (sub + 1) * A_HEAD_DIM)
            qm = jnp.where(head_lanes, q2, jnp.zeros_like(q2))
            s_scr[sub, :, 0:n_keys] = _dot_nt(qm, k_ref[pl.ds(k0, n_keys), pair_cols(head)])

        scores(0)
        outs = []
        for head in range(A_HEADS):
            sub = head % 2
            if head + 1 < A_HEADS:
                scores(head + 1)
            m = jnp.max(s_scr[sub, :, 0:n_keys] + bias[head, :, b0:], axis=-1, keepdims=True)
            p = jnp.exp(s_scr[sub, :, 0:n_keys] + bias[head, :, b0:] - m)
            inv = 1.0 / jnp.sum(p, axis=-1, keepdims=True)
            v2 = v_ref[pl.ds(k0, n_keys), pair_cols(head)]
            outs.append(_dot(p.astype(BF16), v2) * inv)
            if sub == 1:
                o_ref[pl.ds(q0, Q_GROUP), pair_cols(head)] = jnp.where(
                    lane < A_HEAD_DIM, outs[head - 1], outs[head]).astype(BF16)

    n_short = pad // Q_GROUP
    for g in range(n_short):
        group(g * Q_GROUP, 0, (g + 1) * Q_GROUP)

    def full_group(g, carry):
        q0 = pl.multiple_of(g * Q_GROUP, Q_GROUP)
        group(q0, pl.multiple_of(q0 - pad, Q_GROUP), K_WINDOW)
        return carry

    lax.fori_loop(n_short, seq // Q_GROUP, full_group, 0)


def _attention(qkva, trow):
    b, seq, _ = qkva.shape
    blk = lambda j: pl.BlockSpec((None, seq, A_WIDTH), lambda i, j=j: (i, 0, j))
    return pl.pallas_call(
        _attn_kernel,
        grid=(b,),
        in_specs=[blk(0), blk(1), blk(2), _const_spec(trow.shape)],
        out_specs=pl.BlockSpec((None, seq, A_WIDTH), lambda i: (i, 0, 0)),
        out_shape=jax.ShapeDtypeStruct((b, seq, A_WIDTH), BF16),
        scratch_shapes=[pltpu.VMEM((A_HEADS, Q_GROUP, K_WINDOW), F32),
                        pltpu.VMEM((2, Q_GROUP, K_WINDOW), F32)],
        compiler_params=_params("arbitrary"),
        name="band_attention",
    )(qkva, qkva, qkva, trow)


def _toeplitz_row(rel_bias):
    n_far = K_WINDOW - 1 - REL_CLIP
    n_near = ROLL_W - n_far - (2 * REL_CLIP + 1)
    far = jnp.broadcast_to(rel_bias[:, -1:], (A_HEADS, n_far))
    near = jnp.broadcast_to(rel_bias[:, :1], (A_HEADS, n_near))
    t = jnp.concatenate([far, rel_bias[:, ::-1], near], axis=1).astype(F32)
    return t.reshape(A_HEADS, 1, ROLL_W)


def _stack(x):
    return jnp.concatenate([x[c * CHUNK:(c + 1) * CHUNK] for c in range(SLAB // CHUNK)], axis=1)


def _stack_col(col, lane_chunk):
    out = jnp.broadcast_to(col[0:CHUNK], (CHUNK, SLAB))
    for c in range(1, SLAB // CHUNK):
        out = jnp.where(lane_chunk == c, jnp.broadcast_to(col[c * CHUNK:(c + 1) * CHUNK], (CHUNK, SLAB)), out)
    return out


def _block_diag(x_st, lane_chunk):
    zero = jnp.zeros_like(x_st)
    return jnp.concatenate([jnp.where(lane_chunk == c, x_st, zero) for c in range(SLAB // CHUNK)], axis=0)


def _delta_kernel(q_ref, k_ref, v_ref, bg_ref, z_ref, wn_ref, o_ref, cs, gl_s, gct, st_ref):
    seq = q_ref.shape[0]
    heads = range(B_HEADS)
    n_c = SLAB // CHUNK

    row_in_chunk = lax.broadcasted_iota(jnp.int32, (seq, LANES), 0) % CHUNK
    cs[0:CHUNK, :] = jnp.zeros((CHUNK, LANES), F32)
    cs[CHUNK:, :] = bg_ref[...]
    shift = 1
    while shift < CHUNK:
        shifted = cs[CHUNK - shift:CHUNK - shift + seq, :]
        cs[CHUNK:, :] = cs[CHUNK:, :] + jnp.where(row_in_chunk >= shift, shifted, 0.0)
        shift *= 2
    gc_seq = cs[CHUNK:, :]
    g3 = gc_seq.reshape(seq // CHUNK, CHUNK, LANES)
    gl_s[...] = jnp.broadcast_to(g3[:, CHUNK - 1:CHUNK, :], g3.shape).reshape(seq, LANES)
    gc_t = gc_seq.T
    for s in range(seq // SLAB):
        gct[s] = gc_t[0:HALO, s * SLAB:(s + 1) * SLAB]
    st_ref[...] = jnp.zeros(st_ref.shape, F32)

    lane_st = lax.broadcasted_iota(jnp.int32, (CHUNK, SLAB), 1)
    i_st = lax.broadcasted_iota(jnp.int32, (CHUNK, SLAB), 0)
    j_st = lane_st % CHUNK
    lane_chunk = lane_st // CHUNK
    incl_st = i_st >= j_st
    strict_st = i_st > j_st
    eye_st = (i_st == j_st).astype(F32)
    level = [strict_st & (((i_st ^ j_st) >> l) == 1) for l in range(6)]
    kbd_mask = (lax.broadcasted_iota(jnp.int32, (SLAB, n_c * B_DIM), 0) // CHUNK
                == lax.broadcasted_iota(jnp.int32, (SLAB, n_c * B_DIM), 1) // B_DIM)

    def slab(s, carry):
        r0 = pl.multiple_of(s * SLAB, SLAB)
        rows = pl.ds(r0, SLAB)
        bg = bg_ref[rows, :]
        gcs = cs[pl.ds(CHUNK + r0, SLAB), :]
        gls = gl_s[rows, :]
        gam_all, kd_all, egl_all = jnp.exp(gcs), jnp.exp(gls - gcs), jnp.exp(gls)
        gct_s = gct[s]
        col = lambda arr, h: arr[:, B_HEADS + h:B_HEADS + h + 1]
        hcols = [slice(h * B_DIM, (h + 1) * B_DIM) for h in heads]

        q = [q_ref[rows, hcols[h]] for h in heads]
        k = [k_ref[rows, hcols[h]] for h in heads]
        kf = [k[h].astype(F32) for h in heads]
        bk = [bg[:, h:h + 1] * kf[h] for h in heads]
        kbd = [jnp.where(kbd_mask, jnp.concatenate([k[h]] * n_c, axis=1), jnp.zeros((), BF16)) for h in heads]
        qk = [_dot_nt(jnp.concatenate([_stack(q[h]), _stack(bk[h].astype(BF16))], axis=0), kbd[h])
              for h in heads]
        decay = [jnp.exp(jnp.where(incl_st, _stack_col(col(gcs, h), lane_chunk)
                                   - gct_s[B_HEADS + h:B_HEADS + h + 1, :], -1e30)) for h in heads]
        pqk = [(qk[h][:CHUNK] * decay[h]).astype(BF16) for h in heads]
        a = [jnp.where(strict_st, qk[h][CHUNK:] * decay[h], 0.0) for h in heads]

        d = [eye_st - jnp.where(level[0], a[h], 0.0) for h in heads]
        for l in range(1, 6):
            dbd = [_block_diag(d[h].astype(BF16), lane_chunk) for h in heads]
            t1 = [_dot(jnp.where(level[l], a[h], 0.0).astype(BF16), dbd[h]) for h in heads]
            t1bd = [_block_diag(t1[h].astype(BF16), lane_chunk) for h in heads]
            d = [d[h] - _dot(d[h].astype(BF16), t1bd[h]) for h in heads]

        rhs = [jnp.concatenate([bg[:, h:h + 1] * v_ref[rows, hcols[h]].astype(F32), col(gam_all, h) * bk[h]],
                               axis=1).astype(BF16) for h in heads]
        sol16 = [_dot(_block_diag(d[h].astype(BF16), lane_chunk), rhs[h]).astype(BF16) for h in heads]
        x2 = [_dot(_block_diag(pqk[h], lane_chunk), sol16[h]) for h in heads]
        qp = [(col(gam_all, h) * q[h].astype(F32) - x2[h][:, B_DIM:]).astype(BF16) for h in heads]
        kdec = [(kf[h] * col(kd_all, h)).astype(BF16) for h in heads]
        chunk_rows = [slice(c * CHUNK, (c + 1) * CHUNK) for c in range(n_c)]
        mnt = [[_dot_tn(kdec[h][cr], sol16[h][cr]) for cr in chunk_rows] for h in heads]

        for c, cr in enumerate(chunk_rows):
            for h in heads:
                st = st_ref[h]
                st16 = st.astype(BF16)
                o = _dot(qp[h][cr], st16) + x2[h][cr, :B_DIM]
                egl = egl_all[c * CHUNK:c * CHUNK + 1, B_HEADS + h:B_HEADS + h + 1]
                st_ref[h] = egl * st - _dot(mnt[h][c][:, B_DIM:].astype(BF16), st16) + mnt[h][c][:, :B_DIM]
                y = _rms(o, wn_ref[...]) * _silu(z_ref[pl.ds(r0 + c * CHUNK, CHUNK), hcols[h]].astype(F32))
                o_ref[pl.ds(r0 + c * CHUNK, CHUNK), hcols[h]] = y.astype(BF16)
        return carry

    lax.fori_loop(0, seq // SLAB, slab, 0)


def _deltanet(cqkv, bg, z, w_onorm):
    b, seq, _ = cqkv.shape
    blk = lambda j: pl.BlockSpec((None, seq, B_WIDTH), lambda i, j=j: (i, 0, j))
    return pl.pallas_call(
        _delta_kernel,
        grid=(b,),
        in_specs=[blk(0), blk(1), blk(2),
                  pl.BlockSpec((None, seq, LANES), lambda i: (i, 0, 0)),
                  pl.BlockSpec((None, seq, B_WIDTH), lambda i: (i, 0, 0)),
                  _const_spec((1, B_DIM))],
        out_specs=pl.BlockSpec((None, seq, B_WIDTH), lambda i: (i, 0, 0)),
        out_shape=jax.ShapeDtypeStruct((b, seq, B_WIDTH), BF16),
        scratch_shapes=[pltpu.VMEM((CHUNK + seq, LANES), F32),
                        pltpu.VMEM((seq, LANES), F32),
                        pltpu.VMEM((seq // SLAB, HALO, SLAB), F32),
                        pltpu.VMEM((B_HEADS, B_DIM, B_DIM), F32)],
        compiler_params=_params("arbitrary"),
        name="gated_deltanet",
    )(cqkv, cqkv, cqkv, bg, z, w_onorm)


def _merge_kernel(x_ref, ya_ref, yb_ref, g_ref, wg_ref, wa_ref, wb_ref, wo_ref, o_ref):
    x = x_ref[...]
    h = _rms(x, g_ref[...]).astype(BF16)
    ga = _sigmoid(_dot(h, wg_ref[:, :D_MODEL]))
    merged = ga * _dot(ya_ref[...], wa_ref[...])
    gb = _sigmoid(_dot(h, wg_ref[:, D_MODEL:]))
    merged = merged + gb * _dot(yb_ref[...], wb_ref[...])
    o_ref[...] = x + _dot(merged.astype(BF16), wo_ref[...])


def _merge(x2, ya, yb, g_mix, w_gates, w_a, w_b, w_out):
    t = x2.shape[0]
    tm = TOKEN_TILE
    row = lambda width: pl.BlockSpec((tm, width), lambda i: (i, 0))
    return pl.pallas_call(
        _merge_kernel,
        grid=(t // tm,),
        in_specs=[row(D_MODEL), row(A_WIDTH), row(B_WIDTH), _const_spec((1, D_MODEL)),
                  _const_spec(w_gates.shape), _const_spec(w_a.shape), _const_spec(w_b.shape),
                  _const_spec(w_out.shape)],
        out_specs=row(D_MODEL),
        out_shape=jax.ShapeDtypeStruct((t, D_MODEL), F32),
        compiler_params=_params("arbitrary"),
        name="gated_merge",
    )(x2, ya, yb, g_mix, w_gates, w_a, w_b, w_out)


def _ffn_kernel(final, x_ref, p_ref, gf_ref, wgu_ref, wd_ref, gp_ref, wpg_ref, wpp_ref, gfin_ref, o_ref, acc):
    x = x_ref[...]
    h = _rms(x, gf_ref[...]).astype(BF16)
    acc[...] = x
    for j in range(D_FF // FF_SLAB):
        gate = _dot(h, wgu_ref[:, j * FF_SLAB:(j + 1) * FF_SLAB])
        up = _dot(h, wgu_ref[:, D_FF + j * FF_SLAB:D_FF + (j + 1) * FF_SLAB])
        act = (_silu(gate) * up).astype(BF16)
        acc[...] += _dot(act, wd_ref[j * FF_SLAB:(j + 1) * FF_SLAB, :])
    x = acc[...]
    hp = _rms(x, gp_ref[...]).astype(BF16)
    ple_gate = _sigmoid(_dot(hp, wpg_ref[...]))
    x = x + ple_gate * _dot(p_ref[...].astype(BF16), wpp_ref[...])
    o_ref[...] = _rms(x, gfin_ref[...]) if final else x


def _ffn(x2, p2, g_ffn, w_gate_up, w_down, g_ple, w_ple_gate, w_ple_proj, g_final, final):
    t = x2.shape[0]
    tm = TOKEN_TILE
    row = lambda width: pl.BlockSpec((tm, width), lambda i: (i, 0))
    return pl.pallas_call(
        functools.partial(_ffn_kernel, final),
        grid=(t // tm,),
        in_specs=[row(D_MODEL), row(PLE_DIM), _const_spec((1, D_MODEL)),
                  _const_spec(w_gate_up.shape), _const_spec(w_down.shape), _const_spec((1, D_MODEL)),
                  _const_spec(w_ple_gate.shape), _const_spec(w_ple_proj.shape), _const_spec((1, D_MODEL))],
        out_specs=row(D_MODEL),
        out_shape=jax.ShapeDtypeStruct((t, D_MODEL), F32),
        scratch_shapes=[pltpu.VMEM((tm, D_MODEL), F32)],
        compiler_params=_params("arbitrary"),
        name="ffn_ple_final",
    )(x2, p2, g_ffn, w_gate_up, w_down, g_ple, w_ple_gate, w_ple_proj, g_final)


def _layer(final, x2, p2, seq, g_mix, w_in, conv_w, a_log, dt_bias, rel_bias, w_onorm, w_branch_a, w_branch_b,
           w_out, g_ffn, w_gate_up, w_down, g_ple, w_ple_gate, w_ple_proj, g_final):
    t = x2.shape[0]
    b = t // seq
    row = lambda v: v.reshape(1, -1).astype(F32)
    w_bd = jnp.pad(w_in[:, SPLIT_Z:SPLIT_DECAY], ((0, 0), (0, LANES - 2 * B_HEADS)))
    w_in_r = jnp.concatenate([w_in[:, :A_WIDTH] * (A_HEAD_DIM ** -0.5), w_in[:, A_WIDTH:SPLIT_Z], w_bd],
                             axis=1).astype(BF16)
    w_gates = w_in[:, SPLIT_DECAY:].astype(BF16)
    head_pad = lambda v: jnp.pad(v.astype(F32), (B_HEADS, LANES - 2 * B_HEADS)).reshape(1, LANES)

    qkva, cqkv, z, bg = _inproj(x2, row(g_mix), w_in_r, conv_w.astype(F32), head_pad(a_log),
                                head_pad(dt_bias), seq)
    ya = _attention(qkva.reshape(b, seq, SPLIT_A), _toeplitz_row(rel_bias))
    yb = _deltanet(cqkv.reshape(b, seq, B_CONV_CH), bg.reshape(b, seq, LANES),
                   z.reshape(b, seq, B_WIDTH), row(w_onorm))
    x2 = _merge(x2, ya.reshape(t, A_WIDTH), yb.reshape(t, B_WIDTH), row(g_mix), w_gates,
                w_branch_a.astype(BF16), w_branch_b.astype(BF16), w_out.astype(BF16))
    return _ffn(x2, p2, row(g_ffn), w_gate_up.astype(BF16), w_down.astype(BF16), row(g_ple),
                w_ple_gate.astype(BF16), w_ple_proj.astype(BF16), row(g_final), final)


def kernel(x, p, g_mix, w_in, conv_w, a_log, dt_bias, rel_bias, w_onorm, w_branch_a, w_branch_b, w_out,
           g_ffn, w_gate_up, w_down, g_ple, w_ple_gate, w_ple_proj, g_final):
    b, seq, _ = x.shape
    depth = p.shape[0]
    x2 = x.reshape(b * seq, D_MODEL)
    for i in range(depth):
        x2 = _layer(i == depth - 1, x2, p[i].reshape(b * seq, PLE_DIM), seq, g_mix[i], w_in[i], conv_w[i],
                    a_log[i], dt_bias[i], rel_bias[i], w_onorm[i], w_branch_a[i], w_branch_b[i], w_out[i],
                    g_ffn[i], w_gate_up[i], w_down[i], g_ple[i], w_ple_gate[i], w_ple_proj[i], g_final)
    return x2.reshape(b, seq, D_MODEL)
```

```python
import functools

import numpy as np
import jax
import jax.numpy as jnp
from jax import lax
from jax.experimental import pallas as pl
from jax.experimental.pallas import tpu as pltpu

D_MODEL = 1024
CHUNK = 64
PLE_DIM = 256
EPS = 1e-6

A_HEADS = 8
A_HEAD_DIM = 64
A_WIDTH = A_HEADS * A_HEAD_DIM
A_LOOKBACK = 8
REL_CLIP = 128

B_HEADS = 4
B_DIM = 128
B_WIDTH = B_HEADS * B_DIM
CONV_WIDTH = 4
B_CONV_CH = 3 * B_WIDTH

D_FF = 2816

SPLIT_A = 3 * A_WIDTH
SPLIT_CONV = SPLIT_A + B_CONV_CH
SPLIT_Z = SPLIT_CONV + B_WIDTH
SPLIT_BETA = SPLIT_Z + B_HEADS
SPLIT_DECAY = SPLIT_BETA + B_HEADS

LANES = 128
MXU_COLS = 256
HALO = 8
TOKEN_TILE = 512
Q_GROUP = 256
K_WINDOW = Q_GROUP + A_LOOKBACK * CHUNK
ROLL_W = 1024
SLAB = 256
PIPE = 2
FF_SLAB = 256
VMEM_LIMIT = 56 * 1024 * 1024

F32 = jnp.float32
BF16 = jnp.bfloat16
NT_DIMS = (((1,), (1,)), ((), ()))
TN_DIMS = (((0,), (0,)), ((), ()))


def _dot(a, b):
    return jnp.dot(a, b, preferred_element_type=F32)


def _dot_nt(a, b):
    return lax.dot_general(a, b, NT_DIMS, preferred_element_type=F32)


def _dot_tn(a, b):
    return lax.dot_general(a, b, TN_DIMS, preferred_element_type=F32)


def _rms(x, g):
    return x * lax.rsqrt(jnp.mean(x * x, axis=-1, keepdims=True) + EPS) * g


def _sigmoid(x):
    return 0.5 * jnp.tanh(0.5 * x) + 0.5


def _silu(x):
    h = 0.5 * x
    return h * jnp.tanh(h) + h


def _params(*sem):
    return pltpu.CompilerParams(dimension_semantics=sem, vmem_limit_bytes=VMEM_LIMIT)


def _const_spec(shape):
    nd = len(shape)
    return pl.BlockSpec(shape, lambda *_: (0,) * nd, pipeline_mode=pl.Buffered(1))


def _inproj_kernel(tiles_per_seq, x_ref, g_ref, w_ref, cw_ref, alog_ref, dtb_ref,
                   qkva_ref, cqkv_ref, z_ref, bg_ref, cbuf):
    tm = x_ref.shape[0]
    i = pl.program_id(0)
    h = _rms(x_ref[...], g_ref[...]).astype(BF16)

    @pl.when(i % tiles_per_seq == 0)
    def _():
        cbuf[0:HALO, :] = jnp.zeros((HALO, B_CONV_CH), F32)

    @pl.when(i % tiles_per_seq != 0)
    def _():
        cbuf[0:HALO, :] = cbuf[tm:tm + HALO, :]

    plain = [(qkva_ref, 0, j) for j in range(SPLIT_A // MXU_COLS)]
    plain += [(z_ref, SPLIT_CONV, j) for j in range(B_WIDTH // MXU_COLS)]

    def plain_slab(out_ref, w_col0, j):
        cols = slice(j * MXU_COLS, (j + 1) * MXU_COLS)
        out_ref[:, cols] = _dot(h, w_ref[:, w_col0 + j * MXU_COLS:w_col0 + (j + 1) * MXU_COLS]).astype(BF16)

    for j in range(B_CONV_CH // MXU_COLS):
        cols = slice(j * MXU_COLS, (j + 1) * MXU_COLS)
        cbuf[HALO:, cols] = _dot(h, w_ref[:, SPLIT_A + j * MXU_COLS:SPLIT_A + (j + 1) * MXU_COLS])
        plain_slab(*plain.pop(0))
        xs = cbuf[:, cols]
        w = [0.5 * cw_ref[t:t + 1, cols] for t in range(CONV_WIDTH)]
        xs1 = pltpu.roll(xs, 1, 0)
        hc = (pltpu.roll(w[0] * xs1 + w[1] * xs, 2, 0) + (w[2] * xs1 + w[3] * xs))[HALO:]
        c = hc * jnp.tanh(hc) + hc
        for half in range(MXU_COLS // LANES):
            head = j * (MXU_COLS // LANES) + half
            ch = c[:, half * LANES:(half + 1) * LANES]
            if head < 2 * B_HEADS:
                scale = B_DIM ** -0.5 if head < B_HEADS else 1.0
                ch = ch * (lax.rsqrt(jnp.sum(ch * ch, axis=-1, keepdims=True) + EPS) * scale)
            cqkv_ref[:, head * LANES:(head + 1) * LANES] = ch.astype(BF16)

    raw = _dot(h, w_ref[:, SPLIT_Z:SPLIT_Z + LANES])
    lane = lax.broadcasted_iota(jnp.int32, raw.shape, 1)
    sp_in = raw + dtb_ref[...]
    softplus = jnp.maximum(sp_in, 0.0) + jnp.log1p(jnp.exp(-jnp.abs(sp_in)))
    bg_ref[...] = jnp.where(lane < B_HEADS, _sigmoid(raw), -jnp.exp(alog_ref[...]) * softplus)

    for args in plain:
        plain_slab(*args)


def _inproj(x2, g_mix, w_in_r, conv_w, alog_pad, dtb_pad, seq):
    t = x2.shape[0]
    tm = TOKEN_TILE
    nw = w_in_r.shape[1]
    row = lambda width: pl.BlockSpec((tm, width), lambda i: (i, 0))
    return pl.pallas_call(
        functools.partial(_inproj_kernel, seq // tm),
        grid=(t // tm,),
        in_specs=[row(D_MODEL), _const_spec((1, D_MODEL)), _const_spec((D_MODEL, nw)),
                  _const_spec((CONV_WIDTH, B_CONV_CH)), _const_spec((1, LANES)), _const_spec((1, LANES))],
        out_specs=[row(SPLIT_A), row(B_CONV_CH), row(B_WIDTH), row(LANES)],
        out_shape=[jax.ShapeDtypeStruct((t, SPLIT_A), BF16), jax.ShapeDtypeStruct((t, B_CONV_CH), BF16),
                   jax.ShapeDtypeStruct((t, B_WIDTH), BF16), jax.ShapeDtypeStruct((t, LANES), F32)],
        scratch_shapes=[pltpu.VMEM((tm + HALO, B_CONV_CH), F32)],
        compiler_params=_params("arbitrary"),
        name="inproj",
    )(x2, g_mix, w_in_r, conv_w, alog_pad, dtb_pad)


def _attn_kernel(q_ref, k_ref, v_ref, trow_ref, o_ref, bias, s_scr):
    seq = q_ref.shape[0]
    pad = A_LOOKBACK * CHUNK
    lane = lax.broadcasted_iota(jnp.int32, (Q_GROUP, LANES), 1)
    col = lax.broadcasted_iota(jnp.int32, (Q_GROUP, K_WINDOW), 1)

    @pl.when(pl.program_id(0) == 0)
    def _():
        qc = lax.broadcasted_iota(jnp.int32, (Q_GROUP, K_WINDOW), 0) // CHUNK
        kc = col // CHUNK
        in_band = (kc >= qc) & (kc <= qc + A_LOOKBACK)
        for h in range(A_HEADS):
            base = jnp.broadcast_to(trow_ref[h], (Q_GROUP, ROLL_W))
            toeplitz = pltpu.roll(base, ROLL_W - (Q_GROUP - 1), 1, stride=1, stride_axis=0)
            bias[h] = jnp.where(in_band, toeplitz[:, :K_WINDOW], -1e30)

    def group(q0, k0, n_keys):
        b0 = K_WINDOW - n_keys
        pair_cols = lambda head: slice(head // 2 * LANES, (head // 2 + 1) * LANES)

        def scores(head):
            sub = head % 2
            q2 = q_ref[pl.ds(q0, Q_GROUP), pair_cols(head)]
            head_lanes = (lane >= sub * A_HEAD_DIM) & (lane < (sub + 1) * A_HEAD_DIM)
            qm = jnp.where(head_lanes, q2, jnp.zeros_like(q2))
            s_scr[sub, :, 0:n_keys] = _dot_nt(qm, k_ref[pl.ds(k0, n_keys), pair_cols(head)])

        scores(0)
        outs = []
        for head in range(A_HEADS):
            sub = head % 2
            if head + 1 < A_HEADS:
                scores(head + 1)
            m = jnp.max(s_scr[sub, :, 0:n_keys] + bias[head, :, b0:], axis=-1, keepdims=True)
            p = jnp.exp(s_scr[sub, :, 0:n_keys] + bias[head, :, b0:] - m)
            inv = 1.0 / jnp.sum(p, axis=-1, keepdims=True)
            v2 = v_ref[pl.ds(k0, n_keys), pair_cols(head)]
            outs.append(_dot(p.astype(BF16), v2) * inv)
            if sub == 1:
                o_ref[pl.ds(q0, Q_GROUP), pair_cols(head)] = jnp.where(
                    lane < A_HEAD_DIM, outs[head - 1], outs[head]).astype(BF16)

    n_short = pad // Q_GROUP
    for g in range(n_short):
        group(g * Q_GROUP, 0, (g + 1) * Q_GROUP)

    def full_group(g, carry):
        q0 = pl.multiple_of(g * Q_GROUP, Q_GROUP)
        group(q0, pl.multiple_of(q0 - pad, Q_GROUP), K_WINDOW)
        return carry

    lax.fori_loop(n_short, seq // Q_GROUP, full_group, 0)


def _attention(qkva, trow):
    b, seq, _ = qkva.shape
    blk = lambda j: pl.BlockSpec((None, seq, A_WIDTH), lambda i, j=j: (i, 0, j))
    return pl.pallas_call(
        _attn_kernel,
        grid=(b,),
        in_specs=[blk(0), blk(1), blk(2), _const_spec(trow.shape)],
        out_specs=pl.BlockSpec((None, seq, A_WIDTH), lambda i: (i, 0, 0)),
        out_shape=jax.ShapeDtypeStruct((b, seq, A_WIDTH), BF16),
        scratch_shapes=[pltpu.VMEM((A_HEADS, Q_GROUP, K_WINDOW), F32),
                        pltpu.VMEM((2, Q_GROUP, K_WINDOW), F32)],
        compiler_params=_params("arbitrary"),
        name="band_attention",
    )(qkva, qkva, qkva, trow)


def _toeplitz_row(rel_bias):
    n_far = K_WINDOW - 1 - REL_CLIP
    n_near = ROLL_W - n_far - (2 * REL_CLIP + 1)
    far = jnp.broadcast_to(rel_bias[:, -1:], (A_HEADS, n_far))
    near = jnp.broadcast_to(rel_bias[:, :1], (A_HEADS, n_near))
    t = jnp.concatenate([far, rel_bias[:, ::-1], near], axis=1).astype(F32)
    return t.reshape(A_HEADS, 1, ROLL_W)


def _stack(x):
    return jnp.concatenate([x[c * CHUNK:(c + 1) * CHUNK] for c in range(SLAB // CHUNK)], axis=1)


def _stack_col(col, lane_chunk):
    out = jnp.broadcast_to(col[0:CHUNK], (CHUNK, SLAB))
    for c in range(1, SLAB // CHUNK):
        out = jnp.where(lane_chunk == c, jnp.broadcast_to(col[c * CHUNK:(c + 1) * CHUNK], (CHUNK, SLAB)), out)
    return out


def _block_diag(x_st, lane_chunk):
    zero = jnp.zeros_like(x_st)
    return jnp.concatenate([jnp.where(lane_chunk == c, x_st, zero) for c in range(SLAB // CHUNK)], axis=0)


def _delta_kernel(q_ref, k_ref, v_ref, bg_ref, z_ref, wn_ref, o_ref, cs, gl_s, gct, st_ref, qp_s, o0_s, mnt_s):
    seq = q_ref.shape[0]
    heads = range(B_HEADS)
    n_c = SLAB // CHUNK

    row_in_chunk = lax.broadcasted_iota(jnp.int32, (seq, LANES), 0) % CHUNK
    cs[0:CHUNK, :] = jnp.zeros((CHUNK, LANES), F32)
    cs[CHUNK:, :] = bg_ref[...]
    shift = 1
    while shift < CHUNK:
        shifted = cs[CHUNK - shift:CHUNK - shift + seq, :]
        cs[CHUNK:, :] = cs[CHUNK:, :] + jnp.where(row_in_chunk >= shift, shifted, 0.0)
        shift *= 2
    gc_seq = cs[CHUNK:, :]
    g3 = gc_seq.reshape(seq // CHUNK, CHUNK, LANES)
    gl_s[...] = jnp.broadcast_to(g3[:, CHUNK - 1:CHUNK, :], g3.shape).reshape(seq, LANES)
    gc_t = gc_seq.T
    for s in range(seq // SLAB):
        gct[s] = gc_t[0:HALO, s * SLAB:(s + 1) * SLAB]
    st_ref[...] = jnp.zeros(st_ref.shape, F32)

    lane_st = lax.broadcasted_iota(jnp.int32, (CHUNK, SLAB), 1)
    i_st = lax.broadcasted_iota(jnp.int32, (CHUNK, SLAB), 0)
    j_st = lane_st % CHUNK
    lane_chunk = lane_st // CHUNK
    incl_st = i_st >= j_st
    strict_st = i_st > j_st
    eye_st = (i_st == j_st).astype(F32)
    level = [strict_st & (((i_st ^ j_st) >> l) == 1) for l in range(6)]
    kbd_mask = (lax.broadcasted_iota(jnp.int32, (SLAB, n_c * B_DIM), 0) // CHUNK
                == lax.broadcasted_iota(jnp.int32, (SLAB, n_c * B_DIM), 1) // B_DIM)

    col = lambda arr, h: arr[:, B_HEADS + h:B_HEADS + h + 1]
    hcols = [slice(h * B_DIM, (h + 1) * B_DIM) for h in heads]
    chunk_rows = [slice(c * CHUNK, (c + 1) * CHUNK) for c in range(n_c)]

    def solve_slabs(slabs):
        jobs = [(s, buf, h) for s, buf in slabs for h in heads]
        each = lambda f: [f(n) for n in range(len(jobs))]
        r0 = [pl.multiple_of(s * SLAB, SLAB) for s, _, _ in jobs]
        rows = [pl.ds(r, SLAB) for r in r0]
        head = [h for _, _, h in jobs]
        bg = each(lambda n: bg_ref[rows[n], :])
        gcs = each(lambda n: cs[pl.ds(CHUNK + r0[n], SLAB), :])
        gam = each(lambda n: col(jnp.exp(gcs[n]), head[n]))
        kd = each(lambda n: col(jnp.exp(gl_s[rows[n], :] - gcs[n]), head[n]))
        q = each(lambda n: q_ref[rows[n], hcols[head[n]]])
        k = each(lambda n: k_ref[rows[n], hcols[head[n]]])
        kf = each(lambda n: k[n].astype(F32))
        bk = each(lambda n: bg[n][:, head[n]:head[n] + 1] * kf[n])
        kbd = each(lambda n: jnp.where(kbd_mask, jnp.concatenate([k[n]] * n_c, axis=1), jnp.zeros((), BF16)))
        qk = each(lambda n: _dot_nt(jnp.concatenate([_stack(q[n]), _stack(bk[n].astype(BF16))], axis=0), kbd[n]))
        yield
        decay = each(lambda n: jnp.exp(jnp.where(
            incl_st, _stack_col(col(gcs[n], head[n]), lane_chunk)
            - gct[jobs[n][0]][B_HEADS + head[n]:B_HEADS + head[n] + 1, :], -1e30)))
        pqk = each(lambda n: (qk[n][:CHUNK] * decay[n]).astype(BF16))
        a = each(lambda n: jnp.where(strict_st, qk[n][CHUNK:] * decay[n], 0.0))
        d = each(lambda n: eye_st - jnp.where(level[0], a[n], 0.0))
        for l in range(1, 6):
            t1 = each(lambda n: _dot(jnp.where(level[l], a[n], 0.0).astype(BF16),
                                     _block_diag(d[n].astype(BF16), lane_chunk)))
            yield
            d = each(lambda n: d[n] - _dot(d[n].astype(BF16), _block_diag(t1[n].astype(BF16), lane_chunk)))
            yield
        rhs = each(lambda n: jnp.concatenate(
            [bg[n][:, head[n]:head[n] + 1] * v_ref[rows[n], hcols[head[n]]].astype(F32), gam[n] * bk[n]],
            axis=1).astype(BF16))
        sol16 = each(lambda n: _dot(_block_diag(d[n].astype(BF16), lane_chunk), rhs[n]).astype(BF16))
        yield
        x2 = each(lambda n: _dot(_block_diag(pqk[n], lane_chunk), sol16[n]))
        yield
        for n, (_, buf, h) in enumerate(jobs):
            o0_s[buf, :, hcols[h]] = x2[n][:, :B_DIM]
            qp_s[buf, :, hcols[h]] = (gam[n] * q[n].astype(F32) - x2[n][:, B_DIM:]).astype(BF16)
        kdec = each(lambda n: (kf[n] * kd[n]).astype(BF16))
        for c, cr in enumerate(chunk_rows):
            for n, (_, buf, h) in enumerate(jobs):
                mnt_s[buf, h * n_c + c] = _dot_tn(kdec[n][cr], sol16[n][cr])
            yield

    def scan_slabs(slabs):
        for s, buf in slabs:
            r0 = pl.multiple_of(s * SLAB, SLAB)
            for c, cr in enumerate(chunk_rows):
                crow = pl.ds(r0 + c * CHUNK, CHUNK)
                egl_row = jnp.exp(gl_s[pl.ds(r0 + c * CHUNK, 1), :])
                for h in heads:
                    st = st_ref[h]
                    st16 = st.astype(BF16)
                    o = _dot(qp_s[buf, cr, hcols[h]], st16) + o0_s[buf, cr, hcols[h]]
                    mnt = mnt_s[buf, h * n_c + c]
                    st_ref[h] = (egl_row[:, B_HEADS + h:B_HEADS + h + 1] * st
                                 - _dot(mnt[:, B_DIM:].astype(BF16), st16) + mnt[:, :B_DIM])
                    y = _rms(o, wn_ref[...]) * _silu(z_ref[crow, hcols[h]].astype(F32))
                    o_ref[crow, hcols[h]] = y.astype(BF16)
                yield

    def weave(*generators):
        live = list(generators)
        while live:
            live = [g for g in live if next(g, live) is not live]

    n_slab = seq // SLAB
    weave(solve_slabs([(s, s) for s in range(PIPE)]))

    def pipelined(i, carry):
        s0 = i * PIPE
        ahead = [(s0 + PIPE + j, (s0 + PIPE + j) % (2 * PIPE)) for j in range(PIPE)]
        current = [(s0 + j, (s0 + j) % (2 * PIPE)) for j in range(PIPE)]
        weave(solve_slabs(ahead), scan_slabs(current))
        return carry

    lax.fori_loop(0, n_slab // PIPE - 1, pipelined, 0)
    weave(scan_slabs([(s, s % (2 * PIPE)) for s in range(n_slab - PIPE, n_slab)]))


def _deltanet(cqkv, bg, z, w_onorm):
    b, seq, _ = cqkv.shape
    blk = lambda j: pl.BlockSpec((None, seq, B_WIDTH), lambda i, j=j: (i, 0, j))
    return pl.pallas_call(
        _delta_kernel,
        grid=(b,),
        in_specs=[blk(0), blk(1), blk(2),
                  pl.BlockSpec((None, seq, LANES), lambda i: (i, 0, 0)),
                  pl.BlockSpec((None, seq, B_WIDTH), lambda i: (i, 0, 0)),
                  _const_spec((1, B_DIM))],
        out_specs=pl.BlockSpec((None, seq, B_WIDTH), lambda i: (i, 0, 0)),
        out_shape=jax.ShapeDtypeStruct((b, seq, B_WIDTH), BF16),
        scratch_shapes=[pltpu.VMEM((CHUNK + seq, LANES), F32),
                        pltpu.VMEM((seq, LANES), F32),
                        pltpu.VMEM((seq // SLAB, HALO, SLAB), F32),
                        pltpu.VMEM((B_HEADS, B_DIM, B_DIM), F32),
                        pltpu.VMEM((2 * PIPE, SLAB, B_WIDTH), BF16),
                        pltpu.VMEM((2 * PIPE, SLAB, B_WIDTH), F32),
                        pltpu.VMEM((2 * PIPE, B_HEADS * (SLAB // CHUNK), B_DIM, 2 * B_DIM), F32)],
        compiler_params=_params("arbitrary"),
        name="gated_deltanet",
    )(cqkv, cqkv, cqkv, bg, z, w_onorm)


def _merge_kernel(x_ref, ya_ref, yb_ref, g_ref, wg_ref, wa_ref, wb_ref, wo_ref, o_ref):
    x = x_ref[...]
    h = _rms(x, g_ref[...]).astype(BF16)
    ga = _sigmoid(_dot(h, wg_ref[:, :D_MODEL]))
    merged = ga * _dot(ya_ref[...], wa_ref[...])
    gb = _sigmoid(_dot(h, wg_ref[:, D_MODEL:]))
    merged = merged + gb * _dot(yb_ref[...], wb_ref[...])
    o_ref[...] = x + _dot(merged.astype(BF16), wo_ref[...])


def _merge(x2, ya, yb, g_mix, w_gates, w_a, w_b, w_out):
    t = x2.shape[0]
    tm = TOKEN_TILE
    row = lambda width: pl.BlockSpec((tm, width), lambda i: (i, 0))
    return pl.pallas_call(
        _merge_kernel,
        grid=(t // tm,),
        in_specs=[row(D_MODEL), row(A_WIDTH), row(B_WIDTH), _const_spec((1, D_MODEL)),
                  _const_spec(w_gates.shape), _const_spec(w_a.shape), _const_spec(w_b.shape),
                  _const_spec(w_out.shape)],
        out_specs=row(D_MODEL),
        out_shape=jax.ShapeDtypeStruct((t, D_MODEL), F32),
        compiler_params=_params("arbitrary"),
        name="gated_merge",
    )(x2, ya, yb, g_mix, w_gates, w_a, w_b, w_out)


def _ffn_kernel(final, x_ref, p_ref, gf_ref, wgu_ref, wd_ref, gp_ref, wpg_ref, wpp_ref, gfin_ref, o_ref, acc):
    x = x_ref[...]
    h = _rms(x, gf_ref[...]).astype(BF16)
    acc[...] = x
    for j in range(D_FF // FF_SLAB):
        gate = _dot(h, wgu_ref[:, j * FF_SLAB:(j + 1) * FF_SLAB])
        up = _dot(h, wgu_ref[:, D_FF + j * FF_SLAB:D_FF + (j + 1) * FF_SLAB])
        act = (_silu(gate) * up).astype(BF16)
        acc[...] += _dot(act, wd_ref[j * FF_SLAB:(j + 1) * FF_SLAB, :])
    x = acc[...]
    hp = _rms(x, gp_ref[...]).astype(BF16)
    ple_gate = _sigmoid(_dot(hp, wpg_ref[...]))
    x = x + ple_gate * _dot(p_ref[...].astype(BF16), wpp_ref[...])
    o_ref[...] = _rms(x, gfin_ref[...]) if final else x


def _ffn(x2, p2, g_ffn, w_gate_up, w_down, g_ple, w_ple_gate, w_ple_proj, g_final, final):
    t = x2.shape[0]
    tm = TOKEN_TILE
    row = lambda width: pl.BlockSpec((tm, width), lambda i: (i, 0))
    return pl.pallas_call(
        functools.partial(_ffn_kernel, final),
        grid=(t // tm,),
        in_specs=[row(D_MODEL), row(PLE_DIM), _const_spec((1, D_MODEL)),
                  _const_spec(w_gate_up.shape), _const_spec(w_down.shape), _const_spec((1, D_MODEL)),
                  _const_spec(w_ple_gate.shape), _const_spec(w_ple_proj.shape), _const_spec((1, D_MODEL))],
        out_specs=row(D_MODEL),
        out_shape=jax.ShapeDtypeStruct((t, D_MODEL), F32),
        scratch_shapes=[pltpu.VMEM((tm, D_MODEL), F32)],
        compiler_params=_params("arbitrary"),
        name="ffn_ple_final",
    )(x2, p2, g_ffn, w_gate_up, w_down, g_ple, w_ple_gate, w_ple_proj, g_final)


def _layer(final, x2, p2, seq, g_mix, w_in, conv_w, a_log, dt_bias, rel_bias, w_onorm, w_branch_a, w_branch_b,
           w_out, g_ffn, w_gate_up, w_down, g_ple, w_ple_gate, w_ple_proj, g_final):
    t = x2.shape[0]
    b = t // seq
    row = lambda v: v.reshape(1, -1).astype(F32)
    w_bd = jnp.pad(w_in[:, SPLIT_Z:SPLIT_DECAY], ((0, 0), (0, LANES - 2 * B_HEADS)))
    w_in_r = jnp.concatenate([w_in[:, :A_WIDTH] * (A_HEAD_DIM ** -0.5), w_in[:, A_WIDTH:SPLIT_Z], w_bd],
                             axis=1).astype(BF16)
    w_gates = w_in[:, SPLIT_DECAY:].astype(BF16)
    head_pad = lambda v: jnp.pad(v.astype(F32), (B_HEADS, LANES - 2 * B_HEADS)).reshape(1, LANES)

    qkva, cqkv, z, bg = _inproj(x2, row(g_mix), w_in_r, conv_w.astype(F32), head_pad(a_log),
                                head_pad(dt_bias), seq)
    ya = _attention(qkva.reshape(b, seq, SPLIT_A), _toeplitz_row(rel_bias))
    yb = _deltanet(cqkv.reshape(b, seq, B_CONV_CH), bg.reshape(b, seq, LANES),
                   z.reshape(b, seq, B_WIDTH), row(w_onorm))
    x2 = _merge(x2, ya.reshape(t, A_WIDTH), yb.reshape(t, B_WIDTH), row(g_mix), w_gates,
                w_branch_a.astype(BF16), w_branch_b.astype(BF16), w_out.astype(BF16))
    return _ffn(x2, p2, row(g_ffn), w_gate_up.astype(BF16), w_down.astype(BF16), row(g_ple),
                w_ple_gate.astype(BF16), w_ple_proj.astype(BF16), row(g_final), final)


def kernel(x, p, g_mix, w_in, conv_w, a_log, dt_bias, rel_bias, w_onorm, w_branch_a, w_branch_b, w_out,
           g_ffn, w_gate_up, w_down, g_ple, w_ple_gate, w_ple_proj, g_final):
    b, seq, _ = x.shape
    depth = p.shape[0]
    x2 = x.reshape(b * seq, D_MODEL)
    for i in range(depth):
        x2 = _layer(i == depth - 1, x2, p[i].reshape(b * seq, PLE_DIM), seq, g_mix[i], w_in[i], conv_w[i],
                    a_log[i], dt_bias[i], rel_bias[i], w_onorm[i], w_branch_a[i], w_branch_b[i], w_out[i],
                    g_ffn[i], w_gate_up[i], w_down[i], g_ple[i], w_ple_gate[i], w_ple_proj[i], g_final)
    return x2.reshape(b, seq, D_MODEL)
```

```python
import functools

import numpy as np
import jax
import jax.numpy as jnp
from jax import lax
from jax.experimental import pallas as pl
from jax.experimental.pallas import tpu as pltpu

D_MODEL = 1024
CHUNK = 64
PLE_DIM = 256
EPS = 1e-6

A_HEADS = 8
A_HEAD_DIM = 64
A_WIDTH = A_HEADS * A_HEAD_DIM
A_LOOKBACK = 8
REL_CLIP = 128

B_HEADS = 4
B_DIM = 128
B_WIDTH = B_HEADS * B_DIM
CONV_WIDTH = 4
B_CONV_CH = 3 * B_WIDTH

D_FF = 2816

SPLIT_A = 3 * A_WIDTH
SPLIT_CONV = SPLIT_A + B_CONV_CH
SPLIT_Z = SPLIT_CONV + B_WIDTH
SPLIT_BETA = SPLIT_Z + B_HEADS
SPLIT_DECAY = SPLIT_BETA + B_HEADS

LANES = 128
MXU_COLS = 256
HALO = 8
TOKEN_TILE = 1024
Q_GROUP = 256
K_WINDOW = Q_GROUP + A_LOOKBACK * CHUNK
ROLL_W = 1024
SLAB = 256
PIPE = 2
FF_SLAB = 256
VMEM_LIMIT = 56 * 1024 * 1024

F32 = jnp.float32
BF16 = jnp.bfloat16
NT_DIMS = (((1,), (1,)), ((), ()))
TN_DIMS = (((0,), (0,)), ((), ()))


def _dot(a, b):
    return jnp.dot(a, b, preferred_element_type=F32)


def _dot_nt(a, b):
    return lax.dot_general(a, b, NT_DIMS, preferred_element_type=F32)


def _dot_tn(a, b):
    return lax.dot_general(a, b, TN_DIMS, preferred_element_type=F32)


def _rms(x, g):
    return x * lax.rsqrt(jnp.mean(x * x, axis=-1, keepdims=True) + EPS) * g


def _sigmoid(x):
    return 0.5 * jnp.tanh(0.5 * x) + 0.5


def _silu(x):
    h = 0.5 * x
    return h * jnp.tanh(h) + h


def _params(*sem):
    return pltpu.CompilerParams(dimension_semantics=sem, vmem_limit_bytes=VMEM_LIMIT)


def _const_spec(shape):
    nd = len(shape)
    return pl.BlockSpec(shape, lambda *_: (0,) * nd, pipeline_mode=pl.Buffered(1))


def _inproj_kernel(tiles_per_seq, x_ref, g_ref, w_ref, cw_ref, alog_ref, dtb_ref,
                   qkva_ref, cqkv_ref, z_ref, bg_ref, cbuf):
    tm = x_ref.shape[0]
    i = pl.program_id(0)
    h = _rms(x_ref[...], g_ref[...]).astype(BF16)

    @pl.when(i % tiles_per_seq == 0)
    def _():
        cbuf[0:HALO, :] = jnp.zeros((HALO, B_CONV_CH), F32)

    @pl.when(i % tiles_per_seq != 0)
    def _():
        cbuf[0:HALO, :] = cbuf[tm:tm + HALO, :]

    plain = [(qkva_ref, 0, j) for j in range(SPLIT_A // MXU_COLS)]
    plain += [(z_ref, SPLIT_CONV, j) for j in range(B_WIDTH // MXU_COLS)]

    def plain_slab(out_ref, w_col0, j):
        cols = slice(j * MXU_COLS, (j + 1) * MXU_COLS)
        out_ref[:, cols] = _dot(h, w_ref[:, w_col0 + j * MXU_COLS:w_col0 + (j + 1) * MXU_COLS]).astype(BF16)

    for j in range(B_CONV_CH // MXU_COLS):
        cols = slice(j * MXU_COLS, (j + 1) * MXU_COLS)
        cbuf[HALO:, cols] = _dot(h, w_ref[:, SPLIT_A + j * MXU_COLS:SPLIT_A + (j + 1) * MXU_COLS])
        plain_slab(*plain.pop(0))
        xs = cbuf[:, cols]
        w = [0.5 * cw_ref[t:t + 1, cols] for t in range(CONV_WIDTH)]
        xs1 = pltpu.roll(xs, 1, 0)
        hc = (pltpu.roll(w[0] * xs1 + w[1] * xs, 2, 0) + (w[2] * xs1 + w[3] * xs))[HALO:]
        c = hc * jnp.tanh(hc) + hc
        for half in range(MXU_COLS // LANES):
            head = j * (MXU_COLS // LANES) + half
            ch = c[:, half * LANES:(half + 1) * LANES]
            if head < 2 * B_HEADS:
                scale = B_DIM ** -0.5 if head < B_HEADS else 1.0
                ch = ch * (lax.rsqrt(jnp.sum(ch * ch, axis=-1, keepdims=True) + EPS) * scale)
            cqkv_ref[:, head * LANES:(head + 1) * LANES] = ch.astype(BF16)

    raw = _dot(h, w_ref[:, SPLIT_Z:SPLIT_Z + LANES])
    lane = lax.broadcasted_iota(jnp.int32, raw.shape, 1)
    sp_in = raw + dtb_ref[...]
    softplus = jnp.maximum(sp_in, 0.0) + jnp.log1p(jnp.exp(-jnp.abs(sp_in)))
    bg_ref[...] = jnp.where(lane < B_HEADS, _sigmoid(raw), -jnp.exp(alog_ref[...]) * softplus)

    for args in plain:
        plain_slab(*args)


def _inproj(x2, g_mix, w_in_r, conv_w, alog_pad, dtb_pad, seq):
    t = x2.shape[0]
    tm = TOKEN_TILE
    nw = w_in_r.shape[1]
    row = lambda width: pl.BlockSpec((tm, width), lambda i: (i, 0))
    return pl.pallas_call(
        functools.partial(_inproj_kernel, seq // tm),
        grid=(t // tm,),
        in_specs=[row(D_MODEL), _const_spec((1, D_MODEL)), _const_spec((D_MODEL, nw)),
                  _const_spec((CONV_WIDTH, B_CONV_CH)), _const_spec((1, LANES)), _const_spec((1, LANES))],
        out_specs=[row(SPLIT_A), row(B_CONV_CH), row(B_WIDTH), row(LANES)],
        out_shape=[jax.ShapeDtypeStruct((t, SPLIT_A), BF16), jax.ShapeDtypeStruct((t, B_CONV_CH), BF16),
                   jax.ShapeDtypeStruct((t, B_WIDTH), BF16), jax.ShapeDtypeStruct((t, LANES), F32)],
        scratch_shapes=[pltpu.VMEM((tm + HALO, B_CONV_CH), F32)],
        compiler_params=_params("arbitrary"),
        name="inproj",
    )(x2, g_mix, w_in_r, conv_w, alog_pad, dtb_pad)


def _attn_kernel(q_ref, k_ref, v_ref, trow_ref, o_ref, bias, s_scr):
    seq = q_ref.shape[0]
    pad = A_LOOKBACK * CHUNK
    lane = lax.broadcasted_iota(jnp.int32, (Q_GROUP, LANES), 1)
    col = lax.broadcasted_iota(jnp.int32, (Q_GROUP, K_WINDOW), 1)

    @pl.when(pl.program_id(0) == 0)
    def _():
        qc = lax.broadcasted_iota(jnp.int32, (Q_GROUP, K_WINDOW), 0) // CHUNK
        kc = col // CHUNK
        in_band = (kc >= qc) & (kc <= qc + A_LOOKBACK)
        for h in range(A_HEADS):
            base = jnp.broadcast_to(trow_ref[h], (Q_GROUP, ROLL_W))
            toeplitz = pltpu.roll(base, ROLL_W - (Q_GROUP - 1), 1, stride=1, stride_axis=0)
            bias[h] = jnp.where(in_band, toeplitz[:, :K_WINDOW], -1e30)

    def group(q0, k0, n_keys):
        b0 = K_WINDOW - n_keys
        pair_cols = lambda head: slice(head // 2 * LANES, (head // 2 + 1) * LANES)

        def scores(head):
            sub = head % 2
            q2 = q_ref[pl.ds(q0, Q_GROUP), pair_cols(head)]
            head_lanes = (lane >= sub * A_HEAD_DIM) & (lane < (sub + 1) * A_HEAD_DIM)
            qm = jnp.where(head_lanes, q2, jnp.zeros_like(q2))
            s_scr[sub, :, 0:n_keys] = _dot_nt(qm, k_ref[pl.ds(k0, n_keys), pair_cols(head)])

        scores(0)
        outs = []
        for head in range(A_HEADS):
            sub = head % 2
            if head + 1 < A_HEADS:
                scores(head + 1)
            m = jnp.max(s_scr[sub, :, 0:n_keys] + bias[head, :, b0:], axis=-1, keepdims=True)
            p = jnp.exp(s_scr[sub, :, 0:n_keys] + bias[head, :, b0:] - m)
            inv = 1.0 / jnp.sum(p, axis=-1, keepdims=True)
            v2 = v_ref[pl.ds(k0, n_keys), pair_cols(head)]
            outs.append(_dot(p.astype(BF16), v2) * inv)
            if sub == 1:
                o_ref[pl.ds(q0, Q_GROUP), pair_cols(head)] = jnp.where(
                    lane < A_HEAD_DIM, outs[head - 1], outs[head]).astype(BF16)

    n_short = pad // Q_GROUP
    for g in range(n_short):
        group(g * Q_GROUP, 0, (g + 1) * Q_GROUP)

    def full_group(g, carry):
        q0 = pl.multiple_of(g * Q_GROUP, Q_GROUP)
        group(q0, pl.multiple_of(q0 - pad, Q_GROUP), K_WINDOW)
        return carry

    lax.fori_loop(n_short, seq // Q_GROUP, full_group, 0)


def _attention(qkva, trow):
    b, seq, _ = qkva.shape
    blk = lambda j: pl.BlockSpec((None, seq, A_WIDTH), lambda i, j=j: (i, 0, j))
    return pl.pallas_call(
        _attn_kernel,
        grid=(b,),
        in_specs=[blk(0), blk(1), blk(2), _const_spec(trow.shape)],
        out_specs=pl.BlockSpec((None, seq, A_WIDTH), lambda i: (i, 0, 0)),
        out_shape=jax.ShapeDtypeStruct((b, seq, A_WIDTH), BF16),
        scratch_shapes=[pltpu.VMEM((A_HEADS, Q_GROUP, K_WINDOW), F32),
                        pltpu.VMEM((2, Q_GROUP, K_WINDOW), F32)],
        compiler_params=_params("arbitrary"),
        name="band_attention",
    )(qkva, qkva, qkva, trow)


def _toeplitz_row(rel_bias):
    n_far = K_WINDOW - 1 - REL_CLIP
    n_near = ROLL_W - n_far - (2 * REL_CLIP + 1)
    far = jnp.broadcast_to(rel_bias[:, -1:], (A_HEADS, n_far))
    near = jnp.broadcast_to(rel_bias[:, :1], (A_HEADS, n_near))
    t = jnp.concatenate([far, rel_bias[:, ::-1], near], axis=1).astype(F32)
    return t.reshape(A_HEADS, 1, ROLL_W)


def _stack(x):
    return jnp.concatenate([x[c * CHUNK:(c + 1) * CHUNK] for c in range(SLAB // CHUNK)], axis=1)


def _stack_col(col, lane_chunk):
    out = jnp.broadcast_to(col[0:CHUNK], (CHUNK, SLAB))
    for c in range(1, SLAB // CHUNK):
        out = jnp.where(lane_chunk == c, jnp.broadcast_to(col[c * CHUNK:(c + 1) * CHUNK], (CHUNK, SLAB)), out)
    return out


def _block_diag(x_st, lane_chunk):
    zero = jnp.zeros_like(x_st)
    return jnp.concatenate([jnp.where(lane_chunk == c, x_st, zero) for c in range(SLAB // CHUNK)], axis=0)


def _delta_kernel(q_ref, k_ref, v_ref, bg_ref, z_ref, wn_ref, o_ref, cs, gl_s, gct, st_ref, qp_s, o0_s, mnt_s):
    seq = q_ref.shape[0]
    heads = range(B_HEADS)
    n_c = SLAB // CHUNK

    row_in_chunk = lax.broadcasted_iota(jnp.int32, (seq, LANES), 0) % CHUNK
    cs[0:CHUNK, :] = jnp.zeros((CHUNK, LANES), F32)
    cs[CHUNK:, :] = bg_ref[...]
    shift = 1
    while shift < CHUNK:
        shifted = cs[CHUNK - shift:CHUNK - shift + seq, :]
        cs[CHUNK:, :] = cs[CHUNK:, :] + jnp.where(row_in_chunk >= shift, shifted, 0.0)
        shift *= 2
    gc_seq = cs[CHUNK:, :]
    g3 = gc_seq.reshape(seq // CHUNK, CHUNK, LANES)
    gl_s[...] = jnp.broadcast_to(g3[:, CHUNK - 1:CHUNK, :], g3.shape).reshape(seq, LANES)
    gc_t = gc_seq.T
    for s in range(seq // SLAB):
        gct[s] = gc_t[0:HALO, s * SLAB:(s + 1) * SLAB]
    st_ref[...] = jnp.zeros(st_ref.shape, F32)

    lane_st = lax.broadcasted_iota(jnp.int32, (CHUNK, SLAB), 1)
    i_st = lax.broadcasted_iota(jnp.int32, (CHUNK, SLAB), 0)
    j_st = lane_st % CHUNK
    lane_chunk = lane_st // CHUNK
    incl_st = i_st >= j_st
    strict_st = i_st > j_st
    eye_st = (i_st == j_st).astype(F32)
    level = [strict_st & (((i_st ^ j_st) >> l) == 1) for l in range(6)]
    kbd_mask = (lax.broadcasted_iota(jnp.int32, (SLAB, n_c * B_DIM), 0) // CHUNK
                == lax.broadcasted_iota(jnp.int32, (SLAB, n_c * B_DIM), 1) // B_DIM)

    col = lambda arr, h: arr[:, B_HEADS + h:B_HEADS + h + 1]
    hcols = [slice(h * B_DIM, (h + 1) * B_DIM) for h in heads]
    chunk_rows = [slice(c * CHUNK, (c + 1) * CHUNK) for c in range(n_c)]

    def solve_slabs(slabs):
        jobs = [(s, buf, h) for s, buf in slabs for h in heads]
        each = lambda f: [f(n) for n in range(len(jobs))]
        r0 = [pl.multiple_of(s * SLAB, SLAB) for s, _, _ in jobs]
        rows = [pl.ds(r, SLAB) for r in r0]
        head = [h for _, _, h in jobs]
        bg = each(lambda n: bg_ref[rows[n], :])
        gcs = each(lambda n: cs[pl.ds(CHUNK + r0[n], SLAB), :])
        gam = each(lambda n: col(jnp.exp(gcs[n]), head[n]))
        kd = each(lambda n: col(jnp.exp(gl_s[rows[n], :] - gcs[n]), head[n]))
        q = each(lambda n: q_ref[rows[n], hcols[head[n]]])
        k = each(lambda n: k_ref[rows[n], hcols[head[n]]])
        kf = each(lambda n: k[n].astype(F32))
        bk = each(lambda n: bg[n][:, head[n]:head[n] + 1] * kf[n])
        kbd = each(lambda n: jnp.where(kbd_mask, jnp.concatenate([k[n]] * n_c, axis=1), jnp.zeros((), BF16)))
        qk = each(lambda n: _dot_nt(jnp.concatenate([_stack(q[n]), _stack(bk[n].astype(BF16))], axis=0), kbd[n]))
        yield
        decay = each(lambda n: jnp.exp(jnp.where(
            incl_st, _stack_col(col(gcs[n], head[n]), lane_chunk)
            - gct[jobs[n][0]][B_HEADS + head[n]:B_HEADS + head[n] + 1, :], -1e30)))
        pqk = each(lambda n: (qk[n][:CHUNK] * decay[n]).astype(BF16))
        a = each(lambda n: jnp.where(strict_st, qk[n][CHUNK:] * decay[n], 0.0))
        d = each(lambda n: eye_st - jnp.where(level[0], a[n], 0.0))
        for l in range(1, 6):
            t1 = each(lambda n: _dot(jnp.where(level[l], a[n], 0.0).astype(BF16),
                                     _block_diag(d[n].astype(BF16), lane_chunk)))
            yield
            d = each(lambda n: d[n] - _dot(d[n].astype(BF16), _block_diag(t1[n].astype(BF16), lane_chunk)))
            yield
        rhs = each(lambda n: jnp.concatenate(
            [bg[n][:, head[n]:head[n] + 1] * v_ref[rows[n], hcols[head[n]]].astype(F32), gam[n] * bk[n]],
            axis=1).astype(BF16))
        sol16 = each(lambda n: _dot(_block_diag(d[n].astype(BF16), lane_chunk), rhs[n]).astype(BF16))
        yield
        x2 = each(lambda n: _dot(_block_diag(pqk[n], lane_chunk), sol16[n]))
        yield
        for n, (_, buf, h) in enumerate(jobs):
            o0_s[buf, :, hcols[h]] = x2[n][:, :B_DIM]
            qp_s[buf, :, hcols[h]] = (gam[n] * q[n].astype(F32) - x2[n][:, B_DIM:]).astype(BF16)
        kdec = each(lambda n: (kf[n] * kd[n]).astype(BF16))
        for c, cr in enumerate(chunk_rows):
            for n, (_, buf, h) in enumerate(jobs):
                mnt_s[buf, h * n_c + c] = _dot_tn(kdec[n][cr], sol16[n][cr])
            yield

    def scan_slabs(slabs):
        for s, buf in slabs:
            r0 = pl.multiple_of(s * SLAB, SLAB)
            for c, cr in enumerate(chunk_rows):
                crow = pl.ds(r0 + c * CHUNK, CHUNK)
                egl_row = jnp.exp(gl_s[pl.ds(r0 + c * CHUNK, 1), :])
                for h in heads:
                    st = st_ref[h]
                    st16 = st.astype(BF16)
                    o = _dot(qp_s[buf, cr, hcols[h]], st16) + o0_s[buf, cr, hcols[h]]
                    mnt = mnt_s[buf, h * n_c + c]
                    st_ref[h] = (egl_row[:, B_HEADS + h:B_HEADS + h + 1] * st
                                 - _dot(mnt[:, B_DIM:].astype(BF16), st16) + mnt[:, :B_DIM])
                    y = _rms(o, wn_ref[...]) * _silu(z_ref[crow, hcols[h]].astype(F32))
                    o_ref[crow, hcols[h]] = y.astype(BF16)
                yield

    def weave(*generators):
        live = list(generators)
        while live:
            live = [g for g in live if next(g, live) is not live]

    n_slab = seq // SLAB
    weave(solve_slabs([(s, s) for s in range(PIPE)]))

    def pipelined(i, carry):
        s0 = i * PIPE
        ahead = [(s0 + PIPE + j, (s0 + PIPE + j) % (2 * PIPE)) for j in range(PIPE)]
        current = [(s0 + j, (s0 + j) % (2 * PIPE)) for j in range(PIPE)]
        weave(solve_slabs(ahead), scan_slabs(current))
        return carry

    lax.fori_loop(0, n_slab // PIPE - 1, pipelined, 0)
    weave(scan_slabs([(s, s % (2 * PIPE)) for s in range(n_slab - PIPE, n_slab)]))


def _deltanet(cqkv, bg, z, w_onorm):
    b, seq, _ = cqkv.shape
    blk = lambda j: pl.BlockSpec((None, seq, B_WIDTH), lambda i, j=j: (i, 0, j))
    return pl.pallas_call(
        _delta_kernel,
        grid=(b,),
        in_specs=[blk(0), blk(1), blk(2),
                  pl.BlockSpec((None, seq, LANES), lambda i: (i, 0, 0)),
                  pl.BlockSpec((None, seq, B_WIDTH), lambda i: (i, 0, 0)),
                  _const_spec((1, B_DIM))],
        out_specs=pl.BlockSpec((None, seq, B_WIDTH), lambda i: (i, 0, 0)),
        out_shape=jax.ShapeDtypeStruct((b, seq, B_WIDTH), BF16),
        scratch_shapes=[pltpu.VMEM((CHUNK + seq, LANES), F32),
                        pltpu.VMEM((seq, LANES), F32),
                        pltpu.VMEM((seq // SLAB, HALO, SLAB), F32),
                        pltpu.VMEM((B_HEADS, B_DIM, B_DIM), F32),
                        pltpu.VMEM((2 * PIPE, SLAB, B_WIDTH), BF16),
                        pltpu.VMEM((2 * PIPE, SLAB, B_WIDTH), F32),
                        pltpu.VMEM((2 * PIPE, B_HEADS * (SLAB // CHUNK), B_DIM, 2 * B_DIM), F32)],
        compiler_params=_params("arbitrary"),
        name="gated_deltanet",
    )(cqkv, cqkv, cqkv, bg, z, w_onorm)


def _merge_kernel(x_ref, ya_ref, yb_ref, g_ref, wg_ref, wa_ref, wb_ref, wo_ref, o_ref):
    x = x_ref[...]
    h = _rms(x, g_ref[...]).astype(BF16)
    ga = _sigmoid(_dot(h, wg_ref[:, :D_MODEL]))
    merged = ga * _dot(ya_ref[...], wa_ref[...])
    gb = _sigmoid(_dot(h, wg_ref[:, D_MODEL:]))
    merged = merged + gb * _dot(yb_ref[...], wb_ref[...])
    o_ref[...] = x + _dot(merged.astype(BF16), wo_ref[...])


def _merge(x2, ya, yb, g_mix, w_gates, w_a, w_b, w_out):
    t = x2.shape[0]
    tm = TOKEN_TILE
    row = lambda width: pl.BlockSpec((tm, width), lambda i: (i, 0))
    return pl.pallas_call(
        _merge_kernel,
        grid=(t // tm,),
        in_specs=[row(D_MODEL), row(A_WIDTH), row(B_WIDTH), _const_spec((1, D_MODEL)),
                  _const_spec(w_gates.shape), _const_spec(w_a.shape), _const_spec(w_b.shape),
                  _const_spec(w_out.shape)],
        out_specs=row(D_MODEL),
        out_shape=jax.ShapeDtypeStruct((t, D_MODEL), F32),
        compiler_params=_params("arbitrary"),
        name="gated_merge",
    )(x2, ya, yb, g_mix, w_gates, w_a, w_b, w_out)


def _ffn_kernel(final, x_ref, p_ref, gf_ref, wgu_ref, wd_ref, gp_ref, wpg_ref, wpp_ref, gfin_ref, o_ref, acc):
    x = x_ref[...]
    h = _rms(x, gf_ref[...]).astype(BF16)
    acc[...] = x
    for j in range(D_FF // FF_SLAB):
        gate = _dot(h, wgu_ref[:, j * FF_SLAB:(j + 1) * FF_SLAB])
        up = _dot(h, wgu_ref[:, D_FF + j * FF_SLAB:D_FF + (j + 1) * FF_SLAB])
        act = (_silu(gate) * up).astype(BF16)
        acc[...] += _dot(act, wd_ref[j * FF_SLAB:(j + 1) * FF_SLAB, :])
    x = acc[...]
    hp = _rms(x, gp_ref[...]).astype(BF16)
    ple_gate = _sigmoid(_dot(hp, wpg_ref[...]))
    x = x + ple_gate * _dot(p_ref[...].astype(BF16), wpp_ref[...])
    o_ref[...] = _rms(x, gfin_ref[...]) if final else x


def _ffn(x2, p2, g_ffn, w_gate_up, w_down, g_ple, w_ple_gate, w_ple_proj, g_final, final):
    t = x2.shape[0]
    tm = TOKEN_TILE
    row = lambda width: pl.BlockSpec((tm, width), lambda i: (i, 0))
    return pl.pallas_call(
        functools.partial(_ffn_kernel, final),
        grid=(t // tm,),
        in_specs=[row(D_MODEL), row(PLE_DIM), _const_spec((1, D_MODEL)),
                  _const_spec(w_gate_up.shape), _const_spec(w_down.shape), _const_spec((1, D_MODEL)),
                  _const_spec(w_ple_gate.shape), _const_spec(w_ple_proj.shape), _const_spec((1, D_MODEL))],
        out_specs=row(D_MODEL),
        out_shape=jax.ShapeDtypeStruct((t, D_MODEL), F32),
        scratch_shapes=[pltpu.VMEM((tm, D_MODEL), F32)],
        compiler_params=_params("arbitrary"),
        name="ffn_ple_final",
    )(x2, p2, g_ffn, w_gate_up, w_down, g_ple, w_ple_gate, w_ple_proj, g_final)


def _layer(final, x2, p2, seq, g_mix, w_in, conv_w, a_log, dt_bias, rel_bias, w_onorm, w_branch_a, w_branch_b,
           w_out, g_ffn, w_gate_up, w_down, g_ple, w_ple_gate, w_ple_proj, g_final):
    t = x2.shape[0]
    b = t // seq
    row = lambda v: v.reshape(1, -1).astype(F32)
    w_bd = jnp.pad(w_in[:, SPLIT_Z:SPLIT_DECAY], ((0, 0), (0, LANES - 2 * B_HEADS)))
    w_in_r = jnp.concatenate([w_in[:, :A_WIDTH] * (A_HEAD_DIM ** -0.5), w_in[:, A_WIDTH:SPLIT_Z], w_bd],
                             axis=1).astype(BF16)
    w_gates = w_in[:, SPLIT_DECAY:].astype(BF16)
    head_pad = lambda v: jnp.pad(v.astype(F32), (B_HEADS, LANES - 2 * B_HEADS)).reshape(1, LANES)

    qkva, cqkv, z, bg = _inproj(x2, row(g_mix), w_in_r, conv_w.astype(F32), head_pad(a_log),
                                head_pad(dt_bias), seq)
    ya = _attention(qkva.reshape(b, seq, SPLIT_A), _toeplitz_row(rel_bias))
    yb = _deltanet(cqkv.reshape(b, seq, B_CONV_CH), bg.reshape(b, seq, LANES),
                   z.reshape(b, seq, B_WIDTH), row(w_onorm))
    x2 = _merge(x2, ya.reshape(t, A_WIDTH), yb.reshape(t, B_WIDTH), row(g_mix), w_gates,
                w_branch_a.astype(BF16), w_branch_b.astype(BF16), w_out.astype(BF16))
    return _ffn(x2, p2, row(g_ffn), w_gate_up.astype(BF16), w_down.astype(BF16), row(g_ple),
                w_ple_gate.astype(BF16), w_ple_proj.astype(BF16), row(g_final), final)


def kernel(x, p, g_mix, w_in, conv_w, a_log, dt_bias, rel_bias, w_onorm, w_branch_a, w_branch_b, w_out,
           g_ffn, w_gate_up, w_down, g_ple, w_ple_gate, w_ple_proj, g_final):
    b, seq, _ = x.shape
    depth = p.shape[0]
    x2 = x.reshape(b * seq, D_MODEL)
    for i in range(depth):
        x2 = _layer(i == depth - 1, x2, p[i].reshape(b * seq, PLE_DIM), seq, g_mix[i], w_in[i], conv_w[i],
                    a_log[i], dt_bias[i], rel_bias[i], w_onorm[i], w_branch_a[i], w_branch_b[i], w_out[i],
                    g_ffn[i], w_gate_up[i], w_down[i], g_ple[i], w_ple_gate[i], w_ple_proj[i], g_final)
    return x2.reshape(b, seq, D_MODEL)
```

```python
import functools

import numpy as np
import jax
import jax.numpy as jnp
from jax import lax
from jax.experimental import pallas as pl
from jax.experimental.pallas import tpu as pltpu

D_MODEL = 1024
CHUNK = 64
PLE_DIM = 256
EPS = 1e-6

A_HEADS = 8
A_HEAD_DIM = 64
A_WIDTH = A_HEADS * A_HEAD_DIM
A_LOOKBACK = 8
REL_CLIP = 128
LOG2E = 1.4426950408889634

B_HEADS = 4
B_DIM = 128
B_WIDTH = B_HEADS * B_DIM
CONV_WIDTH = 4
B_CONV_CH = 3 * B_WIDTH

D_FF = 2816

SPLIT_A = 3 * A_WIDTH
SPLIT_CONV = SPLIT_A + B_CONV_CH
SPLIT_Z = SPLIT_CONV + B_WIDTH
SPLIT_BETA = SPLIT_Z + B_HEADS
SPLIT_DECAY = SPLIT_BETA + B_HEADS

LANES = 128
MXU_COLS = 256
HEADS_PER_TILE = MXU_COLS // A_HEAD_DIM
HALO = 8
TOKEN_TILE = 1024
INPROJ_TILE = 512
Q_GROUP = 256
K_WINDOW = Q_GROUP + A_LOOKBACK * CHUNK
ROLL_W = 1024
SLAB = 256
PIPE = 2
FF_SLAB = 256
VMEM_LIMIT = 56 * 1024 * 1024

F32 = jnp.float32
BF16 = jnp.bfloat16
NT_DIMS = (((1,), (1,)), ((), ()))
TN_DIMS = (((0,), (0,)), ((), ()))


def _dot(a, b):
    return jnp.dot(a, b, preferred_element_type=F32)


def _dot_nt(a, b):
    return lax.dot_general(a, b, NT_DIMS, preferred_element_type=F32)


def _dot_tn(a, b):
    return lax.dot_general(a, b, TN_DIMS, preferred_element_type=F32)


def _rms(x, g):
    return x * lax.rsqrt(jnp.mean(x * x, axis=-1, keepdims=True) + EPS) * g


def _sigmoid(x):
    return 0.5 * jnp.tanh(0.5 * x) + 0.5


def _silu(x):
    h = 0.5 * x
    return h * jnp.tanh(h) + h


def _params(*sem):
    return pltpu.CompilerParams(dimension_semantics=sem, vmem_limit_bytes=VMEM_LIMIT)


def _const_spec(shape):
    nd = len(shape)
    return pl.BlockSpec(shape, lambda *_: (0,) * nd, pipeline_mode=pl.Buffered(1))


def _inproj_kernel(tiles_per_seq, x_ref, g_ref, w_ref, cw_ref, alog_ref, dtb_ref,
                   qkva_ref, cqkv_ref, z_ref, bg_ref, cbuf):
    tm = x_ref.shape[0]
    i = pl.program_id(0)
    h = _rms(x_ref[...], g_ref[...]).astype(BF16)

    @pl.when(i % tiles_per_seq == 0)
    def _():
        cbuf[0:HALO, :] = jnp.zeros((HALO, B_CONV_CH), F32)

    @pl.when(i % tiles_per_seq != 0)
    def _():
        cbuf[0:HALO, :] = cbuf[tm:tm + HALO, :]

    plain = [(qkva_ref, 0, j) for j in range(SPLIT_A // MXU_COLS)]
    plain += [(z_ref, SPLIT_CONV, j) for j in range(B_WIDTH // MXU_COLS)]

    def plain_slab(out_ref, w_col0, j):
        cols = slice(j * MXU_COLS, (j + 1) * MXU_COLS)
        out_ref[:, cols] = _dot(h, w_ref[:, w_col0 + j * MXU_COLS:w_col0 + (j + 1) * MXU_COLS]).astype(BF16)

    for j in range(B_CONV_CH // MXU_COLS):
        cols = slice(j * MXU_COLS, (j + 1) * MXU_COLS)
        cbuf[HALO:, cols] = _dot(h, w_ref[:, SPLIT_A + j * MXU_COLS:SPLIT_A + (j + 1) * MXU_COLS])
        plain_slab(*plain.pop(0))
        xs = cbuf[:, cols]
        w = [0.5 * cw_ref[t:t + 1, cols] for t in range(CONV_WIDTH)]
        xs1 = pltpu.roll(xs, 1, 0)
        hc = (pltpu.roll(w[0] * xs1 + w[1] * xs, 2, 0) + (w[2] * xs1 + w[3] * xs))[HALO:]
        c = hc * jnp.tanh(hc) + hc
        for half in range(MXU_COLS // LANES):
            head = j * (MXU_COLS // LANES) + half
            ch = c[:, half * LANES:(half + 1) * LANES]
            if head < 2 * B_HEADS:
                scale = B_DIM ** -0.5 if head < B_HEADS else 1.0
                ch = ch * (lax.rsqrt(jnp.sum(ch * ch, axis=-1, keepdims=True) + EPS) * scale)
            cqkv_ref[:, head * LANES:(head + 1) * LANES] = ch.astype(BF16)

    raw = _dot(h, w_ref[:, SPLIT_Z:SPLIT_Z + LANES])
    lane = lax.broadcasted_iota(jnp.int32, raw.shape, 1)
    sp_in = raw + dtb_ref[...]
    softplus = jnp.maximum(sp_in, 0.0) + jnp.log1p(jnp.exp(-jnp.abs(sp_in)))
    bg_ref[...] = jnp.where(lane < B_HEADS, _sigmoid(raw), -jnp.exp(alog_ref[...]) * softplus)

    for args in plain:
        plain_slab(*args)


def _inproj(x2, g_mix, w_in_r, conv_w, alog_pad, dtb_pad, seq):
    t = x2.shape[0]
    tm = INPROJ_TILE
    nw = w_in_r.shape[1]
    row = lambda width: pl.BlockSpec((tm, width), lambda i: (i, 0))
    return pl.pallas_call(
        functools.partial(_inproj_kernel, seq // tm),
        grid=(t // tm,),
        in_specs=[row(D_MODEL), _const_spec((1, D_MODEL)), _const_spec((D_MODEL, nw)),
                  _const_spec((CONV_WIDTH, B_CONV_CH)), _const_spec((1, LANES)), _const_spec((1, LANES))],
        out_specs=[row(SPLIT_A), row(B_CONV_CH), row(B_WIDTH), row(LANES)],
        out_shape=[jax.ShapeDtypeStruct((t, SPLIT_A), BF16), jax.ShapeDtypeStruct((t, B_CONV_CH), BF16),
                   jax.ShapeDtypeStruct((t, B_WIDTH), BF16), jax.ShapeDtypeStruct((t, LANES), F32)],
        scratch_shapes=[pltpu.VMEM((tm + HALO, B_CONV_CH), F32)],
        compiler_params=_params("arbitrary"),
        name="inproj",
    )(x2, g_mix, w_in_r, conv_w, alog_pad, dtb_pad)


def _toeplitz_row(rel_bias):
    n_far = K_WINDOW - 1 - REL_CLIP
    n_near = ROLL_W - n_far - (2 * REL_CLIP + 1)
    far = jnp.broadcast_to(rel_bias[:, -1:], (A_HEADS, n_far))
    near = jnp.broadcast_to(rel_bias[:, :1], (A_HEADS, n_near))
    t = jnp.concatenate([far, rel_bias[:, ::-1], near], axis=1).astype(F32)
    return t.reshape(A_HEADS, 1, ROLL_W)


def _stack(x):
    return jnp.concatenate([x[c * CHUNK:(c + 1) * CHUNK] for c in range(SLAB // CHUNK)], axis=1)


def _stack_col(col, lane_chunk):
    out = jnp.broadcast_to(col[0:CHUNK], (CHUNK, SLAB))
    for c in range(1, SLAB // CHUNK):
        out = jnp.where(lane_chunk == c, jnp.broadcast_to(col[c * CHUNK:(c + 1) * CHUNK], (CHUNK, SLAB)), out)
    return out


def _block_diag(x_st, lane_chunk):
    zero = jnp.zeros_like(x_st)
    return jnp.concatenate([jnp.where(lane_chunk == c, x_st, zero) for c in range(SLAB // CHUNK)], axis=0)


def _weave(*generators):
    live = list(generators)
    while live:
        live = [g for g in live if next(g, live) is not live]


def _attention_steps(q_ref, k_ref, v_ref, trow_ref, o_ref, bias, s_scr):
    lane = lax.broadcasted_iota(jnp.int32, (Q_GROUP, LANES), 1)

    @pl.when(pl.program_id(0) == 0)
    def _():
        qc = lax.broadcasted_iota(jnp.int32, (Q_GROUP, K_WINDOW), 0) // CHUNK
        kc = lax.broadcasted_iota(jnp.int32, (Q_GROUP, K_WINDOW), 1) // CHUNK
        in_band = (kc >= qc) & (kc <= qc + A_LOOKBACK)
        for h in range(A_HEADS):
            base = jnp.broadcast_to(trow_ref[h], (Q_GROUP, ROLL_W))
            toeplitz = pltpu.roll(base, ROLL_W - (Q_GROUP - 1), 1, stride=1, stride_axis=0)
            bias[h] = jnp.where(in_band, toeplitz[:, :K_WINDOW] * LOG2E, -1e30)

    pair_cols = lambda head: slice(head // 2 * LANES, (head // 2 + 1) * LANES)

    def attend(groups):
        for q0, k0, n_keys in groups:
            b0 = K_WINDOW - n_keys

            def scores(head):
                sub = head % 2
                q2 = q_ref[pl.ds(q0, Q_GROUP), pair_cols(head)]
                head_lanes = (lane >= sub * A_HEAD_DIM) & (lane < (sub + 1) * A_HEAD_DIM)
                qm = jnp.where(head_lanes, q2, jnp.zeros_like(q2))
                s_scr[sub, :, 0:n_keys] = _dot_nt(qm, k_ref[pl.ds(k0, n_keys), pair_cols(head)])

            scores(0)
            outs = []
            for head in range(A_HEADS):
                sub = head % 2
                if head + 1 < A_HEADS:
                    scores(head + 1)
                yield
                m = jnp.max(s_scr[sub, :, 0:n_keys] + bias[head, :, b0:], axis=-1, keepdims=True)
                p = jnp.exp2((s_scr[sub, :, 0:n_keys] - m) + bias[head, :, b0:]).astype(BF16)
                quad = head // HEADS_PER_TILE * MXU_COLS
                v4 = v_ref[pl.ds(k0, n_keys), quad:quad + MXU_COLS]
                key_lane = lax.broadcasted_iota(jnp.int32, (n_keys, MXU_COLS), 1)
                own0 = head % HEADS_PER_TILE * A_HEAD_DIM
                own = (key_lane >= own0) & (key_lane < own0 + A_HEAD_DIM)
                o_aug = _dot(p, jnp.where(own, v4, jnp.ones_like(v4)))
                half = own0 // LANES * LANES
                o_pair = o_aug[:, half:half + LANES]
                total = o_aug[:, LANES - half:2 * LANES - half]
                outs.append(o_pair * (1.0 / total))
                if sub == 1:
                    o_ref[pl.ds(q0, Q_GROUP), pair_cols(head)] = jnp.where(
                        lane < A_HEAD_DIM, outs[head - 1], outs[head]).astype(BF16)

    return attend


def _delta_steps(q_ref, k_ref, v_ref, bg_ref, z_ref, wn_ref, o_ref, cs, gl_s, gct, st_ref, qp_s, o0_s, mnt_s):
    seq = q_ref.shape[0]
    heads = range(B_HEADS)
    n_c = SLAB // CHUNK

    row_in_chunk = lax.broadcasted_iota(jnp.int32, (seq, LANES), 0) % CHUNK
    cs[0:CHUNK, :] = jnp.zeros((CHUNK, LANES), F32)
    cs[CHUNK:, :] = bg_ref[...]
    shift = 1
    while shift < CHUNK:
        shifted = cs[CHUNK - shift:CHUNK - shift + seq, :]
        cs[CHUNK:, :] = cs[CHUNK:, :] + jnp.where(row_in_chunk >= shift, shifted, 0.0)
        shift *= 2
    gc_seq = cs[CHUNK:, :]
    g3 = gc_seq.reshape(seq // CHUNK, CHUNK, LANES)
    gl_s[...] = jnp.broadcast_to(g3[:, CHUNK - 1:CHUNK, :], g3.shape).reshape(seq, LANES)
    gc_t = gc_seq.T
    for s in range(seq // SLAB):
        gct[s] = gc_t[0:HALO, s * SLAB:(s + 1) * SLAB]
    st_ref[...] = jnp.zeros(st_ref.shape, F32)

    lane_st = lax.broadcasted_iota(jnp.int32, (CHUNK, SLAB), 1)
    i_st = lax.broadcasted_iota(jnp.int32, (CHUNK, SLAB), 0)
    j_st = lane_st % CHUNK
    lane_chunk = lane_st // CHUNK
    incl_st = i_st >= j_st
    strict_st = i_st > j_st
    eye_st = (i_st == j_st).astype(F32)
    level = [strict_st & (((i_st ^ j_st) >> l) == 1) for l in range(6)]
    kbd_mask = (lax.broadcasted_iota(jnp.int32, (SLAB, n_c * B_DIM), 0) // CHUNK
                == lax.broadcasted_iota(jnp.int32, (SLAB, n_c * B_DIM), 1) // B_DIM)

    col = lambda arr, h: arr[:, B_HEADS + h:B_HEADS + h + 1]
    hcols = [slice(h * B_DIM, (h + 1) * B_DIM) for h in heads]
    chunk_rows = [slice(c * CHUNK, (c + 1) * CHUNK) for c in range(n_c)]

    def solve_slabs(slabs):
        jobs = [(s, buf, h) for s, buf in slabs for h in heads]
        each = lambda f: [f(n) for n in range(len(jobs))]
        r0 = [pl.multiple_of(s * SLAB, SLAB) for s, _, _ in jobs]
        rows = [pl.ds(r, SLAB) for r in r0]
        head = [h for _, _, h in jobs]
        bg = each(lambda n: bg_ref[rows[n], :])
        gcs = each(lambda n: cs[pl.ds(CHUNK + r0[n], SLAB), :])
        gam = each(lambda n: col(jnp.exp(gcs[n]), head[n]))
        kd = each(lambda n: col(jnp.exp(gl_s[rows[n], :] - gcs[n]), head[n]))
        q = each(lambda n: q_ref[rows[n], hcols[head[n]]])
        k = each(lambda n: k_ref[rows[n], hcols[head[n]]])
        kf = each(lambda n: k[n].astype(F32))
        bk = each(lambda n: bg[n][:, head[n]:head[n] + 1] * kf[n])
        kbd = each(lambda n: jnp.where(kbd_mask, jnp.concatenate([k[n]] * n_c, axis=1), jnp.zeros((), BF16)))
        qk = each(lambda n: _dot_nt(jnp.concatenate([_stack(q[n]), _stack(bk[n].astype(BF16))], axis=0), kbd[n]))
        yield
        decay = each(lambda n: jnp.exp(jnp.where(
            incl_st, _stack_col(col(gcs[n], head[n]), lane_chunk)
            - gct[jobs[n][0]][B_HEADS + head[n]:B_HEADS + head[n] + 1, :], -1e30)))
        pqk = each(lambda n: (qk[n][:CHUNK] * decay[n]).astype(BF16))
        a = each(lambda n: jnp.where(strict_st, qk[n][CHUNK:] * decay[n], 0.0))
        d = each(lambda n: eye_st - jnp.where(level[0], a[n], 0.0))
        for l in range(1, 6):
            t1 = each(lambda n: _dot(jnp.where(level[l], a[n], 0.0).astype(BF16),
                                     _block_diag(d[n].astype(BF16), lane_chunk)))
            yield
            d = each(lambda n: d[n] - _dot(d[n].astype(BF16), _block_diag(t1[n].astype(BF16), lane_chunk)))
            yield
        rhs = each(lambda n: jnp.concatenate(
            [bg[n][:, head[n]:head[n] + 1] * v_ref[rows[n], hcols[head[n]]].astype(F32), gam[n] * bk[n]],
            axis=1).astype(BF16))
        sol16 = each(lambda n: _dot(_block_diag(d[n].astype(BF16), lane_chunk), rhs[n]).astype(BF16))
        yield
        x2 = each(lambda n: _dot(_block_diag(pqk[n], lane_chunk), sol16[n]))
        yield
        for n, (_, buf, h) in enumerate(jobs):
            o0_s[buf, :, hcols[h]] = x2[n][:, :B_DIM]
            qp_s[buf, :, hcols[h]] = (gam[n] * q[n].astype(F32) - x2[n][:, B_DIM:]).astype(BF16)
        kdec = each(lambda n: (kf[n] * kd[n]).astype(BF16))
        for c, cr in enumerate(chunk_rows):
            for n, (_, buf, h) in enumerate(jobs):
                mnt_s[buf, h * n_c + c] = _dot_tn(kdec[n][cr], sol16[n][cr])
            yield

    def scan_slabs(slabs):
        for s, buf in slabs:
            r0 = pl.multiple_of(s * SLAB, SLAB)
            for c, cr in enumerate(chunk_rows):
                crow = pl.ds(r0 + c * CHUNK, CHUNK)
                egl_row = jnp.exp(gl_s[pl.ds(r0 + c * CHUNK, 1), :])
                for h in heads:
                    st = st_ref[h]
                    st16 = st.astype(BF16)
                    o = _dot(qp_s[buf, cr, hcols[h]], st16) + o0_s[buf, cr, hcols[h]]
                    mnt = mnt_s[buf, h * n_c + c]
                    st_ref[h] = (egl_row[:, B_HEADS + h:B_HEADS + h + 1] * st
                                 - _dot(mnt[:, B_DIM:].astype(BF16), st16) + mnt[:, :B_DIM])
                    y = _rms(o, wn_ref[...]) * _silu(z_ref[crow, hcols[h]].astype(F32))
                    o_ref[crow, hcols[h]] = y.astype(BF16)
                yield

    return solve_slabs, scan_slabs


def _attn_kernel(q_ref, k_ref, v_ref, trow_ref, o_ref, bias, s_scr):
    seq = q_ref.shape[0]
    pad = A_LOOKBACK * CHUNK
    attend = _attention_steps(q_ref, k_ref, v_ref, trow_ref, o_ref, bias, s_scr)
    n_short = pad // Q_GROUP
    _weave(attend([(g * Q_GROUP, 0, (g + 1) * Q_GROUP) for g in range(n_short)]))

    def full_group(g, carry):
        q0 = pl.multiple_of(g * Q_GROUP, Q_GROUP)
        _weave(attend([(q0, pl.multiple_of(q0 - pad, Q_GROUP), K_WINDOW)]))
        return carry

    lax.fori_loop(n_short, seq // Q_GROUP, full_group, 0)


def _attention(qkva, trow):
    b, seq, _ = qkva.shape
    blk = lambda j: pl.BlockSpec((None, seq, A_WIDTH), lambda i, j=j: (i, 0, j))
    return pl.pallas_call(
        _attn_kernel,
        grid=(b,),
        in_specs=[blk(0), blk(1), blk(2), _const_spec(trow.shape)],
        out_specs=pl.BlockSpec((None, seq, A_WIDTH), lambda i: (i, 0, 0)),
        out_shape=jax.ShapeDtypeStruct((b, seq, A_WIDTH), BF16),
        scratch_shapes=[pltpu.VMEM((A_HEADS, Q_GROUP, K_WINDOW), F32),
                        pltpu.VMEM((2, Q_GROUP, K_WINDOW), F32)],
        compiler_params=_params("arbitrary"),
        name="band_attention",
    )(qkva, qkva, qkva, trow)


def _delta_kernel(q_ref, k_ref, v_ref, bg_ref, z_ref, wn_ref, o_ref, cs, gl_s, gct, st_ref, qp_s, o0_s, mnt_s):
    n_slab = q_ref.shape[0] // SLAB
    solve_slabs, scan_slabs = _delta_steps(q_ref, k_ref, v_ref, bg_ref, z_ref, wn_ref, o_ref,
                                           cs, gl_s, gct, st_ref, qp_s, o0_s, mnt_s)
    buffer_of = lambda s: s % (2 * PIPE)

    _weave(solve_slabs([(s, buffer_of(s)) for s in range(PIPE)]))

    def trip(i, carry):
        s0 = i * PIPE
        _weave(solve_slabs([(s0 + PIPE + j, buffer_of(s0 + PIPE + j)) for j in range(PIPE)]),
               scan_slabs([(s0 + j, buffer_of(s0 + j)) for j in range(PIPE)]))
        return carry

    lax.fori_loop(0, n_slab // PIPE - 1, trip, 0)
    _weave(scan_slabs([(s, buffer_of(s)) for s in range(n_slab - PIPE, n_slab)]))


def _deltanet(cqkv, bg, z, w_onorm):
    b, seq, _ = cqkv.shape
    assert (seq // SLAB) % PIPE == 0
    blk = lambda width, j: pl.BlockSpec((None, seq, width), lambda i, j=j: (i, 0, j))
    n_c = SLAB // CHUNK
    return pl.pallas_call(
        _delta_kernel,
        grid=(b,),
        in_specs=[blk(B_WIDTH, 0), blk(B_WIDTH, 1), blk(B_WIDTH, 2), blk(LANES, 0), blk(B_WIDTH, 0),
                  _const_spec((1, B_DIM))],
        out_specs=blk(B_WIDTH, 0),
        out_shape=jax.ShapeDtypeStruct((b, seq, B_WIDTH), BF16),
        scratch_shapes=[pltpu.VMEM((CHUNK + seq, LANES), F32),
                        pltpu.VMEM((seq, LANES), F32),
                        pltpu.VMEM((seq // SLAB, HALO, SLAB), F32),
                        pltpu.VMEM((B_HEADS, B_DIM, B_DIM), F32),
                        pltpu.VMEM((2 * PIPE, SLAB, B_WIDTH), BF16),
                        pltpu.VMEM((2 * PIPE, SLAB, B_WIDTH), F32),
                        pltpu.VMEM((2 * PIPE, B_HEADS * n_c, B_DIM, 2 * B_DIM), F32)],
        compiler_params=_params("arbitrary"),
        name="gated_deltanet",
    )(cqkv, cqkv, cqkv, bg, z, w_onorm)


def _merge_kernel(x_ref, ya_ref, yb_ref, g_ref, wg_ref, wa_ref, wb_ref, wo_ref, o_ref):
    x = x_ref[...]
    h = _rms(x, g_ref[...]).astype(BF16)
    ga = _sigmoid(_dot(h, wg_ref[:, :D_MODEL]))
    merged = ga * _dot(ya_ref[...], wa_ref[...])
    gb = _sigmoid(_dot(h, wg_ref[:, D_MODEL:]))
    merged = merged + gb * _dot(yb_ref[...], wb_ref[...])
    o_ref[...] = x + _dot(merged.astype(BF16), wo_ref[...])


def _merge(x2, ya, yb, g_mix, w_gates, w_a, w_b, w_out):
    t = x2.shape[0]
    tm = TOKEN_TILE
    row = lambda width: pl.BlockSpec((tm, width), lambda i: (i, 0))
    return pl.pallas_call(
        _merge_kernel,
        grid=(t // tm,),
        in_specs=[row(D_MODEL), row(A_WIDTH), row(B_WIDTH), _const_spec((1, D_MODEL)),
                  _const_spec(w_gates.shape), _const_spec(w_a.shape), _const_spec(w_b.shape),
                  _const_spec(w_out.shape)],
        out_specs=row(D_MODEL),
        out_shape=jax.ShapeDtypeStruct((t, D_MODEL), F32),
        compiler_params=_params("arbitrary"),
        name="gated_merge",
    )(x2, ya, yb, g_mix, w_gates, w_a, w_b, w_out)


def _ffn_kernel(final, x_ref, p_ref, gf_ref, wgu_ref, wd_ref, gp_ref, wpg_ref, wpp_ref, gfin_ref, o_ref, acc):
    x = x_ref[...]
    h = _rms(x, gf_ref[...]).astype(BF16)
    acc[...] = x
    for j in range(D_FF // FF_SLAB):
        gate = _dot(h, wgu_ref[:, j * FF_SLAB:(j + 1) * FF_SLAB])
        up = _dot(h, wgu_ref[:, D_FF + j * FF_SLAB:D_FF + (j + 1) * FF_SLAB])
        act = (_silu(gate) * up).astype(BF16)
        acc[...] += _dot(act, wd_ref[j * FF_SLAB:(j + 1) * FF_SLAB, :])
    x = acc[...]
    hp = _rms(x, gp_ref[...]).astype(BF16)
    ple_gate = _sigmoid(_dot(hp, wpg_ref[...]))
    x = x + ple_gate * _dot(p_ref[...].astype(BF16), wpp_ref[...])
    o_ref[...] = _rms(x, gfin_ref[...]) if final else x


def _ffn(x2, p2, g_ffn, w_gate_up, w_down, g_ple, w_ple_gate, w_ple_proj, g_final, final):
    t = x2.shape[0]
    tm = TOKEN_TILE
    row = lambda width: pl.BlockSpec((tm, width), lambda i: (i, 0))
    return pl.pallas_call(
        functools.partial(_ffn_kernel, final),
        grid=(t // tm,),
        in_specs=[row(D_MODEL), row(PLE_DIM), _const_spec((1, D_MODEL)),
                  _const_spec(w_gate_up.shape), _const_spec(w_down.shape), _const_spec((1, D_MODEL)),
                  _const_spec(w_ple_gate.shape), _const_spec(w_ple_proj.shape), _const_spec((1, D_MODEL))],
        out_specs=row(D_MODEL),
        out_shape=jax.ShapeDtypeStruct((t, D_MODEL), F32),
        scratch_shapes=[pltpu.VMEM((tm, D_MODEL), F32)],
        compiler_params=_params("arbitrary"),
        name="ffn_ple_final",
    )(x2, p2, g_ffn, w_gate_up, w_down, g_ple, w_ple_gate, w_ple_proj, g_final)


def _layer(final, x2, p2, seq, g_mix, w_in, conv_w, a_log, dt_bias, rel_bias, w_onorm, w_branch_a, w_branch_b,
           w_out, g_ffn, w_gate_up, w_down, g_ple, w_ple_gate, w_ple_proj, g_final):
    t = x2.shape[0]
    b = t // seq
    row = lambda v: v.reshape(1, -1).astype(F32)
    w_bd = jnp.pad(w_in[:, SPLIT_Z:SPLIT_DECAY], ((0, 0), (0, LANES - 2 * B_HEADS)))
    w_in_r = jnp.concatenate([w_in[:, :A_WIDTH] * (A_HEAD_DIM ** -0.5 * LOG2E), w_in[:, A_WIDTH:SPLIT_Z], w_bd],
                             axis=1).astype(BF16)
    w_gates = w_in[:, SPLIT_DECAY:].astype(BF16)
    head_pad = lambda v: jnp.pad(v.astype(F32), (B_HEADS, LANES - 2 * B_HEADS)).reshape(1, LANES)

    qkva, cqkv, z, bg = _inproj(x2, row(g_mix), w_in_r, conv_w.astype(F32), head_pad(a_log),
                                head_pad(dt_bias), seq)
    ya = _attention(qkva.reshape(b, seq, SPLIT_A), _toeplitz_row(rel_bias))
    yb = _deltanet(cqkv.reshape(b, seq, B_CONV_CH), bg.reshape(b, seq, LANES),
                   z.reshape(b, seq, B_WIDTH), row(w_onorm))
    x2 = _merge(x2, ya.reshape(t, A_WIDTH), yb.reshape(t, B_WIDTH), row(g_mix), w_gates,
                w_branch_a.astype(BF16), w_branch_b.astype(BF16), w_out.astype(BF16))
    return _ffn(x2, p2, row(g_ffn), w_gate_up.astype(BF16), w_down.astype(BF16), row(g_ple),
                w_ple_gate.astype(BF16), w_ple_proj.astype(BF16), row(g_final), final)


def kernel(x, p, g_mix, w_in, conv_w, a_log, dt_bias, rel_bias, w_onorm, w_branch_a, w_branch_b, w_out,
           g_ffn, w_gate_up, w_down, g_ple, w_ple_gate, w_ple_proj, g_final):
    b, seq, _ = x.shape
    depth = p.shape[0]
    x2 = x.reshape(b * seq, D_MODEL)
    for i in range(depth):
        x2 = _layer(i == depth - 1, x2, p[i].reshape(b * seq, PLE_DIM), seq, g_mix[i], w_in[i], conv_w[i],
                    a_log[i], dt_bias[i], rel_bias[i], w_onorm[i], w_branch_a[i], w_branch_b[i], w_out[i],
                    g_ffn[i], w_gate_up[i], w_down[i], g_ple[i], w_ple_gate[i], w_ple_proj[i], g_final)
    return x2.reshape(b, seq, D_MODEL)
```

```python
import functools

import numpy as np
import jax
import jax.numpy as jnp
from jax import lax
from jax.experimental import pallas as pl
from jax.experimental.pallas import tpu as pltpu

D_MODEL = 1024
CHUNK = 64
PLE_DIM = 256
EPS = 1e-6

A_HEADS = 8
A_HEAD_DIM = 64
A_WIDTH = A_HEADS * A_HEAD_DIM
A_LOOKBACK = 8
REL_CLIP = 128
LOG2E = 1.4426950408889634

B_HEADS = 4
B_DIM = 128
B_WIDTH = B_HEADS * B_DIM
CONV_WIDTH = 4
B_CONV_CH = 3 * B_WIDTH

D_FF = 2816

SPLIT_A = 3 * A_WIDTH
SPLIT_CONV = SPLIT_A + B_CONV_CH
SPLIT_Z = SPLIT_CONV + B_WIDTH
SPLIT_BETA = SPLIT_Z + B_HEADS
SPLIT_DECAY = SPLIT_BETA + B_HEADS

LANES = 128
MXU_COLS = 256
HEADS_PER_TILE = MXU_COLS // A_HEAD_DIM
HALO = 8
TOKEN_TILE = 1024
INPROJ_TILE = 512
Q_GROUP = 256
K_WINDOW = Q_GROUP + A_LOOKBACK * CHUNK
ROLL_W = 1024
SLAB = 256
PIPE = 2
FF_SLAB = 256
VMEM_LIMIT = 56 * 1024 * 1024

F32 = jnp.float32
BF16 = jnp.bfloat16
NT_DIMS = (((1,), (1,)), ((), ()))
TN_DIMS = (((0,), (0,)), ((), ()))


def _dot(a, b):
    return jnp.dot(a, b, preferred_element_type=F32)


def _dot_nt(a, b):
    return lax.dot_general(a, b, NT_DIMS, preferred_element_type=F32)


def _dot_tn(a, b):
    return lax.dot_general(a, b, TN_DIMS, preferred_element_type=F32)


def _rms(x, g):
    return x * lax.rsqrt(jnp.mean(x * x, axis=-1, keepdims=True) + EPS) * g


def _sigmoid(x):
    return 0.5 * jnp.tanh(0.5 * x) + 0.5


def _silu(x):
    h = 0.5 * x
    return h * jnp.tanh(h) + h


def _params(*sem):
    return pltpu.CompilerParams(dimension_semantics=sem, vmem_limit_bytes=VMEM_LIMIT)


def _const_spec(shape):
    nd = len(shape)
    return pl.BlockSpec(shape, lambda *_: (0,) * nd, pipeline_mode=pl.Buffered(1))


def _inproj_kernel(tiles_per_seq, x_ref, g_ref, w_ref, cw_ref, alog_ref, dtb_ref,
                   qkva_ref, cqkv_ref, z_ref, bg_ref, cbuf):
    tm = x_ref.shape[0]
    i = pl.program_id(0)
    h = _rms(x_ref[...], g_ref[...]).astype(BF16)

    @pl.when(i % tiles_per_seq == 0)
    def _():
        cbuf[0:HALO, :] = jnp.zeros((HALO, B_CONV_CH), F32)

    @pl.when(i % tiles_per_seq != 0)
    def _():
        cbuf[0:HALO, :] = cbuf[tm:tm + HALO, :]

    plain = [(qkva_ref, 0, j) for j in range(SPLIT_A // MXU_COLS)]
    plain += [(z_ref, SPLIT_CONV, j) for j in range(B_WIDTH // MXU_COLS)]

    def plain_slab(out_ref, w_col0, j):
        cols = slice(j * MXU_COLS, (j + 1) * MXU_COLS)
        out_ref[:, cols] = _dot(h, w_ref[:, w_col0 + j * MXU_COLS:w_col0 + (j + 1) * MXU_COLS]).astype(BF16)

    for j in range(B_CONV_CH // MXU_COLS):
        cols = slice(j * MXU_COLS, (j + 1) * MXU_COLS)
        cbuf[HALO:, cols] = _dot(h, w_ref[:, SPLIT_A + j * MXU_COLS:SPLIT_A + (j + 1) * MXU_COLS])
        plain_slab(*plain.pop(0))
        xs = cbuf[:, cols]
        w = [0.5 * cw_ref[t:t + 1, cols] for t in range(CONV_WIDTH)]
        xs1 = pltpu.roll(xs, 1, 0)
        hc = (pltpu.roll(w[0] * xs1 + w[1] * xs, 2, 0) + (w[2] * xs1 + w[3] * xs))[HALO:]
        c = hc * jnp.tanh(hc) + hc
        for half in range(MXU_COLS // LANES):
            head = j * (MXU_COLS // LANES) + half
            ch = c[:, half * LANES:(half + 1) * LANES]
            if head < 2 * B_HEADS:
                scale = B_DIM ** -0.5 if head < B_HEADS else 1.0
                ch = ch * (lax.rsqrt(jnp.sum(ch * ch, axis=-1, keepdims=True) + EPS) * scale)
            cqkv_ref[:, head * LANES:(head + 1) * LANES] = ch.astype(BF16)

    raw = _dot(h, w_ref[:, SPLIT_Z:SPLIT_Z + LANES])
    lane = lax.broadcasted_iota(jnp.int32, raw.shape, 1)
    sp_in = raw + dtb_ref[...]
    softplus = jnp.maximum(sp_in, 0.0) + jnp.log1p(jnp.exp(-jnp.abs(sp_in)))
    bg_ref[...] = jnp.where(lane < B_HEADS, _sigmoid(raw), -jnp.exp(alog_ref[...]) * softplus)

    for args in plain:
        plain_slab(*args)


def _inproj(x2, g_mix, w_in_r, conv_w, alog_pad, dtb_pad, seq):
    t = x2.shape[0]
    tm = INPROJ_TILE
    row = lambda width: pl.BlockSpec((tm, width), lambda i: (i, 0))
    return pl.pallas_call(
        functools.partial(_inproj_kernel, seq // tm),
        grid=(t // tm,),
        in_specs=[row(D_MODEL), _const_spec((1, D_MODEL)), _const_spec(w_in_r.shape),
                  _const_spec((CONV_WIDTH, B_CONV_CH)), _const_spec((1, LANES)), _const_spec((1, LANES))],
        out_specs=[row(SPLIT_A), row(B_CONV_CH), row(B_WIDTH), row(LANES)],
        out_shape=[jax.ShapeDtypeStruct((t, SPLIT_A), BF16), jax.ShapeDtypeStruct((t, B_CONV_CH), BF16),
                   jax.ShapeDtypeStruct((t, B_WIDTH), BF16), jax.ShapeDtypeStruct((t, LANES), F32)],
        scratch_shapes=[pltpu.VMEM((tm + HALO, B_CONV_CH), F32)],
        compiler_params=_params("arbitrary"),
        name="inproj",
    )(x2, g_mix, w_in_r, conv_w, alog_pad, dtb_pad)


def _toeplitz_row(rel_bias):
    n_far = K_WINDOW - 1 - REL_CLIP
    n_near = ROLL_W - n_far - (2 * REL_CLIP + 1)
    far = jnp.broadcast_to(rel_bias[:, -1:], (A_HEADS, n_far))
    near = jnp.broadcast_to(rel_bias[:, :1], (A_HEADS, n_near))
    t = jnp.concatenate([far, rel_bias[:, ::-1], near], axis=1).astype(F32)
    return t.reshape(A_HEADS, 1, ROLL_W)


def _stack(x):
    return jnp.concatenate([x[c * CHUNK:(c + 1) * CHUNK] for c in range(SLAB // CHUNK)], axis=1)


def _stack_col(col, lane_chunk):
    out = jnp.broadcast_to(col[0:CHUNK], (CHUNK, SLAB))
    for c in range(1, SLAB // CHUNK):
        out = jnp.where(lane_chunk == c, jnp.broadcast_to(col[c * CHUNK:(c + 1) * CHUNK], (CHUNK, SLAB)), out)
    return out


def _block_diag(x_st, lane_chunk):
    zero = jnp.zeros_like(x_st)
    return jnp.concatenate([jnp.where(lane_chunk == c, x_st, zero) for c in range(SLAB // CHUNK)], axis=0)


def _weave(*generators):
    live = list(generators)
    while live:
        live = [g for g in live if next(g, live) is not live]


def _attention_steps(q_ref, k_ref, v_ref, trow_ref, o_ref, bias, s_scr):
    lane = lax.broadcasted_iota(jnp.int32, (Q_GROUP, LANES), 1)

    @pl.when(pl.program_id(0) == 0)
    def _():
        qc = lax.broadcasted_iota(jnp.int32, (Q_GROUP, K_WINDOW), 0) // CHUNK
        kc = lax.broadcasted_iota(jnp.int32, (Q_GROUP, K_WINDOW), 1) // CHUNK
        in_band = (kc >= qc) & (kc <= qc + A_LOOKBACK)
        for h in range(A_HEADS):
            base = jnp.broadcast_to(trow_ref[h], (Q_GROUP, ROLL_W))
            toeplitz = pltpu.roll(base, ROLL_W - (Q_GROUP - 1), 1, stride=1, stride_axis=0)
            bias[h] = jnp.where(in_band, toeplitz[:, :K_WINDOW] * LOG2E, -1e30)

    pair_cols = lambda head: slice(head // 2 * LANES, (head // 2 + 1) * LANES)

    def attend(groups):
        for q0, k0, n_keys in groups:
            b0 = K_WINDOW - n_keys

            def scores(head):
                sub = head % 2
                q2 = q_ref[pl.ds(q0, Q_GROUP), pair_cols(head)]
                head_lanes = (lane >= sub * A_HEAD_DIM) & (lane < (sub + 1) * A_HEAD_DIM)
                qm = jnp.where(head_lanes, q2, jnp.zeros_like(q2))
                s_scr[sub, :, 0:n_keys] = _dot_nt(qm, k_ref[pl.ds(k0, n_keys), pair_cols(head)])

            scores(0)
            outs = []
            for head in range(A_HEADS):
                sub = head % 2
                if head + 1 < A_HEADS:
                    scores(head + 1)
                yield
                m = jnp.max(s_scr[sub, :, 0:n_keys] + bias[head, :, b0:], axis=-1, keepdims=True)
                p = jnp.exp2((s_scr[sub, :, 0:n_keys] - m) + bias[head, :, b0:]).astype(BF16)
                quad = head // HEADS_PER_TILE * MXU_COLS
                v4 = v_ref[pl.ds(k0, n_keys), quad:quad + MXU_COLS]
                key_lane = lax.broadcasted_iota(jnp.int32, (n_keys, MXU_COLS), 1)
                own0 = head % HEADS_PER_TILE * A_HEAD_DIM
                own = (key_lane >= own0) & (key_lane < own0 + A_HEAD_DIM)
                o_aug = _dot(p, jnp.where(own, v4, jnp.ones_like(v4)))
                half = own0 // LANES * LANES
                o_pair = o_aug[:, half:half + LANES]
                total = o_aug[:, LANES - half:2 * LANES - half]
                outs.append(o_pair * (1.0 / total))
                if sub == 1:
                    o_ref[pl.ds(q0, Q_GROUP), pair_cols(head)] = jnp.where(
                        lane < A_HEAD_DIM, outs[head - 1], outs[head]).astype(BF16)

    return attend


def _delta_steps(q_ref, k_ref, v_ref, bg_ref, z_ref, wn_ref, o_ref, cs, gl_s, gct, st_ref, qp_s, o0_s, mnt_s):
    seq = q_ref.shape[0]
    heads = range(B_HEADS)
    n_c = SLAB // CHUNK

    row_in_chunk = lax.broadcasted_iota(jnp.int32, (seq, LANES), 0) % CHUNK
    cs[0:CHUNK, :] = jnp.zeros((CHUNK, LANES), F32)
    cs[CHUNK:, :] = bg_ref[...]
    shift = 1
    while shift < CHUNK:
        shifted = cs[CHUNK - shift:CHUNK - shift + seq, :]
        cs[CHUNK:, :] = cs[CHUNK:, :] + jnp.where(row_in_chunk >= shift, shifted, 0.0)
        shift *= 2
    gc_seq = cs[CHUNK:, :]
    g3 = gc_seq.reshape(seq // CHUNK, CHUNK, LANES)
    gl_s[...] = jnp.broadcast_to(g3[:, CHUNK - 1:CHUNK, :], g3.shape).reshape(seq, LANES)
    gc_t = gc_seq.T
    for s in range(seq // SLAB):
        gct[s] = gc_t[0:HALO, s * SLAB:(s + 1) * SLAB]
    st_ref[...] = jnp.zeros(st_ref.shape, F32)

    lane_st = lax.broadcasted_iota(jnp.int32, (CHUNK, SLAB), 1)
    i_st = lax.broadcasted_iota(jnp.int32, (CHUNK, SLAB), 0)
    j_st = lane_st % CHUNK
    lane_chunk = lane_st // CHUNK
    incl_st = i_st >= j_st
    strict_st = i_st > j_st
    eye_st = (i_st == j_st).astype(F32)
    level = [strict_st & (((i_st ^ j_st) >> l) == 1) for l in range(6)]
    kbd_mask = (lax.broadcasted_iota(jnp.int32, (SLAB, n_c * B_DIM), 0) // CHUNK
                == lax.broadcasted_iota(jnp.int32, (SLAB, n_c * B_DIM), 1) // B_DIM)

    col = lambda arr, h: arr[:, B_HEADS + h:B_HEADS + h + 1]
    hcols = [slice(h * B_DIM, (h + 1) * B_DIM) for h in heads]
    chunk_rows = [slice(c * CHUNK, (c + 1) * CHUNK) for c in range(n_c)]

    def solve_slabs(slabs):
        jobs = [(s, buf, h) for s, buf in slabs for h in heads]
        each = lambda f: [f(n) for n in range(len(jobs))]
        r0 = [pl.multiple_of(s * SLAB, SLAB) for s, _, _ in jobs]
        rows = [pl.ds(r, SLAB) for r in r0]
        head = [h for _, _, h in jobs]
        bg = each(lambda n: bg_ref[rows[n], :])
        gcs = each(lambda n: cs[pl.ds(CHUNK + r0[n], SLAB), :])
        gam = each(lambda n: col(jnp.exp(gcs[n]), head[n]))
        kd = each(lambda n: col(jnp.exp(gl_s[rows[n], :] - gcs[n]), head[n]))
        q = each(lambda n: q_ref[rows[n], hcols[head[n]]])
        k = each(lambda n: k_ref[rows[n], hcols[head[n]]])
        kf = each(lambda n: k[n].astype(F32))
        bk = each(lambda n: bg[n][:, head[n]:head[n] + 1] * kf[n])
        kbd = each(lambda n: jnp.where(kbd_mask, jnp.concatenate([k[n]] * n_c, axis=1), jnp.zeros((), BF16)))
        qk = each(lambda n: _dot_nt(jnp.concatenate([_stack(q[n]), _stack(bk[n].astype(BF16))], axis=0), kbd[n]))
        yield
        decay = each(lambda n: jnp.exp(jnp.where(
            incl_st, _stack_col(col(gcs[n], head[n]), lane_chunk)
            - gct[jobs[n][0]][B_HEADS + head[n]:B_HEADS + head[n] + 1, :], -1e30)))
        pqk = each(lambda n: (qk[n][:CHUNK] * decay[n]).astype(BF16))
        a = each(lambda n: jnp.where(strict_st, qk[n][CHUNK:] * decay[n], 0.0))
        d = each(lambda n: eye_st - jnp.where(level[0], a[n], 0.0))
        for l in range(1, 6):
            t1 = each(lambda n: _dot(jnp.where(level[l], a[n], 0.0).astype(BF16),
                                     _block_diag(d[n].astype(BF16), lane_chunk)))
            yield
            d = each(lambda n: d[n] - _dot(d[n].astype(BF16), _block_diag(t1[n].astype(BF16), lane_chunk)))
            yield
        rhs = each(lambda n: jnp.concatenate(
            [bg[n][:, head[n]:head[n] + 1] * v_ref[rows[n], hcols[head[n]]].astype(F32), gam[n] * bk[n]],
            axis=1).astype(BF16))
        sol16 = each(lambda n: _dot(_block_diag(d[n].astype(BF16), lane_chunk), rhs[n]).astype(BF16))
        yield
        x2 = each(lambda n: _dot(_block_diag(pqk[n], lane_chunk), sol16[n]))
        yield
        for n, (_, buf, h) in enumerate(jobs):
            o0_s[buf, :, hcols[h]] = x2[n][:, :B_DIM]
            qp_s[buf, :, hcols[h]] = (gam[n] * q[n].astype(F32) - x2[n][:, B_DIM:]).astype(BF16)
        kdec = each(lambda n: (kf[n] * kd[n]).astype(BF16))
        for c, cr in enumerate(chunk_rows):
            for n, (_, buf, h) in enumerate(jobs):
                mnt_s[buf, h * n_c + c] = _dot_tn(kdec[n][cr], sol16[n][cr])
            yield

    def scan_slabs(slabs):
        for s, buf in slabs:
            r0 = pl.multiple_of(s * SLAB, SLAB)
            for c, cr in enumerate(chunk_rows):
                crow = pl.ds(r0 + c * CHUNK, CHUNK)
                egl_row = jnp.exp(gl_s[pl.ds(r0 + c * CHUNK, 1), :])
                for h in heads:
                    st = st_ref[h]
                    st16 = st.astype(BF16)
                    o = _dot(qp_s[buf, cr, hcols[h]], st16) + o0_s[buf, cr, hcols[h]]
                    mnt = mnt_s[buf, h * n_c + c]
                    st_ref[h] = (egl_row[:, B_HEADS + h:B_HEADS + h + 1] * st
                                 - _dot(mnt[:, B_DIM:].astype(BF16), st16) + mnt[:, :B_DIM])
                    y = _rms(o, wn_ref[...]) * _silu(z_ref[crow, hcols[h]].astype(F32))
                    o_ref[crow, hcols[h]] = y.astype(BF16)
                yield

    return solve_slabs, scan_slabs


def _attn_kernel(q_ref, k_ref, v_ref, trow_ref, o_ref, bias, s_scr):
    seq = q_ref.shape[0]
    pad = A_LOOKBACK * CHUNK
    attend = _attention_steps(q_ref, k_ref, v_ref, trow_ref, o_ref, bias, s_scr)
    n_short = pad // Q_GROUP
    _weave(attend([(g * Q_GROUP, 0, (g + 1) * Q_GROUP) for g in range(n_short)]))

    def full_group(g, carry):
        q0 = pl.multiple_of(g * Q_GROUP, Q_GROUP)
        _weave(attend([(q0, pl.multiple_of(q0 - pad, Q_GROUP), K_WINDOW)]))
        return carry

    lax.fori_loop(n_short, seq // Q_GROUP, full_group, 0)


def _attention(qkva, trow):
    b, seq, _ = qkva.shape
    blk = lambda j: pl.BlockSpec((None, seq, A_WIDTH), lambda i, j=j: (i, 0, j))
    return pl.pallas_call(
        _attn_kernel,
        grid=(b,),
        in_specs=[blk(0), blk(1), blk(2), _const_spec(trow.shape)],
        out_specs=pl.BlockSpec((None, seq, A_WIDTH), lambda i: (i, 0, 0)),
        out_shape=jax.ShapeDtypeStruct((b, seq, A_WIDTH), BF16),
        scratch_shapes=[pltpu.VMEM((A_HEADS, Q_GROUP, K_WINDOW), F32),
                        pltpu.VMEM((2, Q_GROUP, K_WINDOW), F32)],
        compiler_params=_params("arbitrary"),
        name="band_attention",
    )(qkva, qkva, qkva, trow)


def _delta_kernel(q_ref, k_ref, v_ref, bg_ref, z_ref, wn_ref, o_ref, cs, gl_s, gct, st_ref, qp_s, o0_s, mnt_s):
    n_slab = q_ref.shape[0] // SLAB
    solve_slabs, scan_slabs = _delta_steps(q_ref, k_ref, v_ref, bg_ref, z_ref, wn_ref, o_ref,
                                           cs, gl_s, gct, st_ref, qp_s, o0_s, mnt_s)
    buffer_of = lambda s: s % (2 * PIPE)

    _weave(solve_slabs([(s, buffer_of(s)) for s in range(PIPE)]))

    def trip(i, carry):
        s0 = i * PIPE
        _weave(solve_slabs([(s0 + PIPE + j, buffer_of(s0 + PIPE + j)) for j in range(PIPE)]),
               scan_slabs([(s0 + j, buffer_of(s0 + j)) for j in range(PIPE)]))
        return carry

    lax.fori_loop(0, n_slab // PIPE - 1, trip, 0)
    _weave(scan_slabs([(s, buffer_of(s)) for s in range(n_slab - PIPE, n_slab)]))


def _deltanet(cqkv, bg, z, w_onorm):
    b, seq, _ = cqkv.shape
    assert (seq // SLAB) % PIPE == 0
    blk = lambda width, j: pl.BlockSpec((None, seq, width), lambda i, j=j: (i, 0, j))
    n_c = SLAB // CHUNK
    return pl.pallas_call(
        _delta_kernel,
        grid=(b,),
        in_specs=[blk(B_WIDTH, 0), blk(B_WIDTH, 1), blk(B_WIDTH, 2), blk(LANES, 0), blk(B_WIDTH, 0),
                  _const_spec((1, B_DIM))],
        out_specs=blk(B_WIDTH, 0),
        out_shape=jax.ShapeDtypeStruct((b, seq, B_WIDTH), BF16),
        scratch_shapes=[pltpu.VMEM((CHUNK + seq, LANES), F32),
                        pltpu.VMEM((seq, LANES), F32),
                        pltpu.VMEM((seq // SLAB, HALO, SLAB), F32),
                        pltpu.VMEM((B_HEADS, B_DIM, B_DIM), F32),
                        pltpu.VMEM((2 * PIPE, SLAB, B_WIDTH), BF16),
                        pltpu.VMEM((2 * PIPE, SLAB, B_WIDTH), F32),
                        pltpu.VMEM((2 * PIPE, B_HEADS * n_c, B_DIM, 2 * B_DIM), F32)],
        compiler_params=_params("arbitrary"),
        name="gated_deltanet",
    )(cqkv, cqkv, cqkv, bg, z, w_onorm)


def _merge_kernel(x_ref, ya_ref, yb_ref, g_ref, wg_ref, wa_ref, wb_ref, wo_ref, o_ref):
    x = x_ref[...]
    h = _rms(x, g_ref[...]).astype(BF16)
    ga = _sigmoid(_dot(h, wg_ref[:, :D_MODEL]))
    merged = ga * _dot(ya_ref[...], wa_ref[...])
    gb = _sigmoid(_dot(h, wg_ref[:, D_MODEL:]))
    merged = merged + gb * _dot(yb_ref[...], wb_ref[...])
    o_ref[...] = x + _dot(merged.astype(BF16), wo_ref[...])


def _merge(x2, ya, yb, g_mix, w_gates, w_a, w_b, w_out):
    t = x2.shape[0]
    tm = TOKEN_TILE
    row = lambda width: pl.BlockSpec((tm, width), lambda i: (i, 0))
    return pl.pallas_call(
        _merge_kernel,
        grid=(t // tm,),
        in_specs=[row(D_MODEL), row(A_WIDTH), row(B_WIDTH), _const_spec((1, D_MODEL)),
                  _const_spec(w_gates.shape), _const_spec(w_a.shape), _const_spec(w_b.shape),
                  _const_spec(w_out.shape)],
        out_specs=row(D_MODEL),
        out_shape=jax.ShapeDtypeStruct((t, D_MODEL), F32),
        compiler_params=_params("arbitrary"),
        name="gated_merge",
    )(x2, ya, yb, g_mix, w_gates, w_a, w_b, w_out)


def _ffn_kernel(final, x_ref, p_ref, gf_ref, wgu_ref, wd_ref, gp_ref, wpg_ref, wpp_ref, gfin_ref, o_ref, acc):
    x = x_ref[...]
    h = _rms(x, gf_ref[...]).astype(BF16)
    acc[...] = x
    for j in range(D_FF // FF_SLAB):
        gate = _dot(h, wgu_ref[:, j * FF_SLAB:(j + 1) * FF_SLAB])
        up = _dot(h, wgu_ref[:, D_FF + j * FF_SLAB:D_FF + (j + 1) * FF_SLAB])
        act = (_silu(gate) * up).astype(BF16)
        acc[...] += _dot(act, wd_ref[j * FF_SLAB:(j + 1) * FF_SLAB, :])
    x = acc[...]
    hp = _rms(x, gp_ref[...]).astype(BF16)
    ple_gate = _sigmoid(_dot(hp, wpg_ref[...]))
    x = x + ple_gate * _dot(p_ref[...].astype(BF16), wpp_ref[...])
    o_ref[...] = _rms(x, gfin_ref[...]) if final else x


def _ffn(x2, p2, g_ffn, w_gate_up, w_down, g_ple, w_ple_gate, w_ple_proj, g_final, final):
    t = x2.shape[0]
    tm = TOKEN_TILE
    row = lambda width: pl.BlockSpec((tm, width), lambda i: (i, 0))
    return pl.pallas_call(
        functools.partial(_ffn_kernel, final),
        grid=(t // tm,),
        in_specs=[row(D_MODEL), row(PLE_DIM), _const_spec((1, D_MODEL)),
                  _const_spec(w_gate_up.shape), _const_spec(w_down.shape), _const_spec((1, D_MODEL)),
                  _const_spec(w_ple_gate.shape), _const_spec(w_ple_proj.shape), _const_spec((1, D_MODEL))],
        out_specs=row(D_MODEL),
        out_shape=jax.ShapeDtypeStruct((t, D_MODEL), F32),
        scratch_shapes=[pltpu.VMEM((tm, D_MODEL), F32)],
        compiler_params=_params("arbitrary"),
        name="ffn_ple_final",
    )(x2, p2, g_ffn, w_gate_up, w_down, g_ple, w_ple_gate, w_ple_proj, g_final)


def _layer(final, x2, p2, seq, g_mix, w_in, conv_w, a_log, dt_bias, rel_bias, w_onorm, w_branch_a, w_branch_b,
           w_out, g_ffn, w_gate_up, w_down, g_ple, w_ple_gate, w_ple_proj, g_final):
    t = x2.shape[0]
    b = t // seq
    row = lambda v: v.reshape(1, -1).astype(F32)
    col_scale = jnp.where(jnp.arange(w_in.shape[1]) < A_WIDTH, A_HEAD_DIM ** -0.5 * LOG2E, 1.0).astype(F32)
    w16 = (w_in * col_scale).astype(BF16)
    w_bd = jnp.pad(w16[:, SPLIT_Z:SPLIT_DECAY], ((0, 0), (0, LANES - 2 * B_HEADS)))
    w_in_r = jnp.concatenate([w16[:, :SPLIT_Z], w_bd], axis=1)
    w_gates = w16[:, SPLIT_DECAY:]
    head_pad = lambda v: jnp.pad(v.astype(F32), (B_HEADS, LANES - 2 * B_HEADS)).reshape(1, LANES)

    qkva, cqkv, z, bg = _inproj(x2, row(g_mix), w_in_r, conv_w.astype(F32), head_pad(a_log),
                                head_pad(dt_bias), seq)
    ya = _attention(qkva.reshape(b, seq, SPLIT_A), _toeplitz_row(rel_bias))
    yb = _deltanet(cqkv.reshape(b, seq, B_CONV_CH), bg.reshape(b, seq, LANES),
                   z.reshape(b, seq, B_WIDTH), row(w_onorm))
    x2 = _merge(x2, ya.reshape(t, A_WIDTH), yb.reshape(t, B_WIDTH), row(g_mix), w_gates,
                w_branch_a.astype(BF16), w_branch_b.astype(BF16), w_out.astype(BF16))
    return _ffn(x2, p2, row(g_ffn), w_gate_up.astype(BF16), w_down.astype(BF16), row(g_ple),
                w_ple_gate.astype(BF16), w_ple_proj.astype(BF16), row(g_final), final)


def kernel(x, p, g_mix, w_in, conv_w, a_log, dt_bias, rel_bias, w_onorm, w_branch_a, w_branch_b, w_out,
           g_ffn, w_gate_up, w_down, g_ple, w_ple_gate, w_ple_proj, g_final):
    b, seq, _ = x.shape
    depth = p.shape[0]
    x2 = x.reshape(b * seq, D_MODEL)
    for i in range(depth):
        x2 = _layer(i == depth - 1, x2, p[i].reshape(b * seq, PLE_DIM), seq, g_mix[i], w_in[i], conv_w[i],
                    a_log[i], dt_bias[i], rel_bias[i], w_onorm[i], w_branch_a[i], w_branch_b[i], w_out[i],
                    g_ffn[i], w_gate_up[i], w_down[i], g_ple[i], w_ple_gate[i], w_ple_proj[i], g_final)
    return x2.reshape(b, seq, D_MODEL)
```

```python
import functools

import numpy as np
import jax
import jax.numpy as jnp
from jax import lax
from jax.experimental import pallas as pl
from jax.experimental.pallas import tpu as pltpu

D_MODEL = 1024
CHUNK = 64
PLE_DIM = 256
EPS = 1e-6

A_HEADS = 8
A_HEAD_DIM = 64
A_WIDTH = A_HEADS * A_HEAD_DIM
A_LOOKBACK = 8
REL_CLIP = 128
LOG2E = 1.4426950408889634

B_HEADS = 4
B_DIM = 128
B_WIDTH = B_HEADS * B_DIM
CONV_WIDTH = 4
B_CONV_CH = 3 * B_WIDTH

D_FF = 2816

SPLIT_A = 3 * A_WIDTH
SPLIT_CONV = SPLIT_A + B_CONV_CH
SPLIT_Z = SPLIT_CONV + B_WIDTH
SPLIT_BETA = SPLIT_Z + B_HEADS
SPLIT_DECAY = SPLIT_BETA + B_HEADS

LANES = 128
MXU_COLS = 256
HEADS_PER_TILE = MXU_COLS // A_HEAD_DIM
HALO = 8
TOKEN_TILE = 1024
INPROJ_TILE = 512
Q_GROUP = 256
K_WINDOW = Q_GROUP + A_LOOKBACK * CHUNK
SCORE_BUFFERS = 4
ROLL_W = 1024
SLAB = 256
PIPE = 2
FF_SLAB = 256
VMEM_LIMIT = 56 * 1024 * 1024

F32 = jnp.float32
BF16 = jnp.bfloat16
NT_DIMS = (((1,), (1,)), ((), ()))
TN_DIMS = (((0,), (0,)), ((), ()))


def _dot(a, b):
    return jnp.dot(a, b, preferred_element_type=F32)


def _dot_nt(a, b):
    return lax.dot_general(a, b, NT_DIMS, preferred_element_type=F32)


def _dot_tn(a, b):
    return lax.dot_general(a, b, TN_DIMS, preferred_element_type=F32)


def _rms(x, g):
    return x * lax.rsqrt(jnp.mean(x * x, axis=-1, keepdims=True) + EPS) * g


def _sigmoid(x):
    return 0.5 * jnp.tanh(0.5 * x) + 0.5


def _silu(x):
    h = 0.5 * x
    return h * jnp.tanh(h) + h


def _params(*sem):
    return pltpu.CompilerParams(dimension_semantics=sem, vmem_limit_bytes=VMEM_LIMIT)


def _const_spec(shape):
    nd = len(shape)
    return pl.BlockSpec(shape, lambda *_: (0,) * nd, pipeline_mode=pl.Buffered(1))


def _inproj_kernel(tiles_per_seq, x_ref, g_ref, w_ref, cw_ref, alog_ref, dtb_ref,
                   qkva_ref, cqkv_ref, z_ref, bg_ref, cbuf):
    tm = x_ref.shape[0]
    i = pl.program_id(0)
    h = _rms(x_ref[...], g_ref[...]).astype(BF16)

    @pl.when(i % tiles_per_seq == 0)
    def _():
        cbuf[0:HALO, :] = jnp.zeros((HALO, B_CONV_CH), F32)

    @pl.when(i % tiles_per_seq != 0)
    def _():
        cbuf[0:HALO, :] = cbuf[tm:tm + HALO, :]

    plain = [(qkva_ref, 0, j) for j in range(SPLIT_A // MXU_COLS)]
    plain += [(z_ref, SPLIT_CONV, j) for j in range(B_WIDTH // MXU_COLS)]

    def plain_slab(out_ref, w_col0, j):
        cols = slice(j * MXU_COLS, (j + 1) * MXU_COLS)
        out_ref[:, cols] = _dot(h, w_ref[:, w_col0 + j * MXU_COLS:w_col0 + (j + 1) * MXU_COLS]).astype(BF16)

    for j in range(B_CONV_CH // MXU_COLS):
        cols = slice(j * MXU_COLS, (j + 1) * MXU_COLS)
        cbuf[HALO:, cols] = _dot(h, w_ref[:, SPLIT_A + j * MXU_COLS:SPLIT_A + (j + 1) * MXU_COLS])
        plain_slab(*plain.pop(0))
        xs = cbuf[:, cols]
        w = [0.5 * cw_ref[t:t + 1, cols] for t in range(CONV_WIDTH)]
        xs1 = pltpu.roll(xs, 1, 0)
        hc = (pltpu.roll(w[0] * xs1 + w[1] * xs, 2, 0) + (w[2] * xs1 + w[3] * xs))[HALO:]
        c = hc * jnp.tanh(hc) + hc
        for half in range(MXU_COLS // LANES):
            head = j * (MXU_COLS // LANES) + half
            ch = c[:, half * LANES:(half + 1) * LANES]
            if head < 2 * B_HEADS:
                scale = B_DIM ** -0.5 if head < B_HEADS else 1.0
                ch = ch * (lax.rsqrt(jnp.sum(ch * ch, axis=-1, keepdims=True) + EPS) * scale)
            cqkv_ref[:, head * LANES:(head + 1) * LANES] = ch.astype(BF16)

    raw = _dot(h, w_ref[:, SPLIT_Z:SPLIT_Z + LANES])
    lane = lax.broadcasted_iota(jnp.int32, raw.shape, 1)
    sp_in = raw + dtb_ref[...]
    softplus = jnp.maximum(sp_in, 0.0) + jnp.log1p(jnp.exp(-jnp.abs(sp_in)))
    bg_ref[...] = jnp.where(lane < B_HEADS, _sigmoid(raw), -jnp.exp(alog_ref[...]) * softplus)

    for args in plain:
        plain_slab(*args)


def _inproj(x2, g_mix, w_in_r, conv_w, alog_pad, dtb_pad, seq):
    t = x2.shape[0]
    tm = INPROJ_TILE
    row = lambda width: pl.BlockSpec((tm, width), lambda i: (i, 0))
    return pl.pallas_call(
        functools.partial(_inproj_kernel, seq // tm),
        grid=(t // tm,),
        in_specs=[row(D_MODEL), _const_spec((1, D_MODEL)), _const_spec(w_in_r.shape),
                  _const_spec((CONV_WIDTH, B_CONV_CH)), _const_spec((1, LANES)), _const_spec((1, LANES))],
        out_specs=[row(SPLIT_A), row(B_CONV_CH), row(B_WIDTH), row(LANES)],
        out_shape=[jax.ShapeDtypeStruct((t, SPLIT_A), BF16), jax.ShapeDtypeStruct((t, B_CONV_CH), BF16),
                   jax.ShapeDtypeStruct((t, B_WIDTH), BF16), jax.ShapeDtypeStruct((t, LANES), F32)],
        scratch_shapes=[pltpu.VMEM((tm + HALO, B_CONV_CH), F32)],
        compiler_params=_params("arbitrary"),
        name="inproj",
    )(x2, g_mix, w_in_r, conv_w, alog_pad, dtb_pad)


def _toeplitz_row(rel_bias):
    n_far = K_WINDOW - 1 - REL_CLIP
    n_near = ROLL_W - n_far - (2 * REL_CLIP + 1)
    far = jnp.broadcast_to(rel_bias[:, -1:], (A_HEADS, n_far))
    near = jnp.broadcast_to(rel_bias[:, :1], (A_HEADS, n_near))
    t = jnp.concatenate([far, rel_bias[:, ::-1], near], axis=1).astype(F32)
    return t.reshape(A_HEADS, 1, ROLL_W)


def _stack(x):
    return jnp.concatenate([x[c * CHUNK:(c + 1) * CHUNK] for c in range(SLAB // CHUNK)], axis=1)


def _stack_col(col, lane_chunk):
    out = jnp.broadcast_to(col[0:CHUNK], (CHUNK, SLAB))
    for c in range(1, SLAB // CHUNK):
        out = jnp.where(lane_chunk == c, jnp.broadcast_to(col[c * CHUNK:(c + 1) * CHUNK], (CHUNK, SLAB)), out)
    return out


def _block_diag(x_st, lane_chunk):
    zero = jnp.zeros_like(x_st)
    return jnp.concatenate([jnp.where(lane_chunk == c, x_st, zero) for c in range(SLAB // CHUNK)], axis=0)


def _weave(*generators):
    live = list(generators)
    while live:
        live = [g for g in live if next(g, live) is not live]


def _attention_steps(q_ref, k_ref, v_ref, trow_ref, o_ref, bias_t, s_scr):
    lane = lax.broadcasted_iota(jnp.int32, (Q_GROUP, LANES), 1)

    @pl.when(pl.program_id(0) == 0)
    def _():
        qc = lax.broadcasted_iota(jnp.int32, (Q_GROUP, K_WINDOW), 0) // CHUNK
        kc = lax.broadcasted_iota(jnp.int32, (Q_GROUP, K_WINDOW), 1) // CHUNK
        in_band = (kc >= qc) & (kc <= qc + A_LOOKBACK)
        for h in range(A_HEADS):
            base = jnp.broadcast_to(trow_ref[h], (Q_GROUP, ROLL_W))
            toeplitz = pltpu.roll(base, ROLL_W - (Q_GROUP - 1), 1, stride=1, stride_axis=0)
            bias_t[h] = jnp.where(in_band, toeplitz[:, :K_WINDOW] * LOG2E, -1e30).T

    pair_cols = lambda head: slice(head // 2 * LANES, (head // 2 + 1) * LANES)

    def attend(groups):
        for q0, k0, n_keys in groups:
            b0 = K_WINDOW - n_keys
            value_row = lax.broadcasted_iota(jnp.int32, (LANES, n_keys), 0)
            out_row = lax.broadcasted_iota(jnp.int32, (LANES, Q_GROUP), 0)

            def scores(head):
                sub = head % 2
                q2 = q_ref[pl.ds(q0, Q_GROUP), pair_cols(head)]
                head_lanes = (lane >= sub * A_HEAD_DIM) & (lane < (sub + 1) * A_HEAD_DIM)
                qm = jnp.where(head_lanes, q2, jnp.zeros_like(q2))
                s_scr[head % SCORE_BUFFERS, 0:n_keys, :] = _dot_nt(k_ref[pl.ds(k0, n_keys), pair_cols(head)], qm)

            for head in range(SCORE_BUFFERS - 1):
                scores(head)
            outs = []
            for head in range(A_HEADS):
                sub = head % 2
                buf = head % SCORE_BUFFERS
                if head + SCORE_BUFFERS - 1 < A_HEADS:
                    scores(head + SCORE_BUFFERS - 1)
                yield
                m = jnp.max(s_scr[buf, 0:n_keys, :] + bias_t[head, b0:, :], axis=0, keepdims=True)
                p = jnp.exp2((s_scr[buf, 0:n_keys, :] - m) + bias_t[head, b0:, :]).astype(BF16)
                if sub == 0:
                    v_pair_t = v_ref[pl.ds(k0, n_keys), pair_cols(head)].astype(F32).T
                own = (value_row >= sub * A_HEAD_DIM) & (value_row < (sub + 1) * A_HEAD_DIM)
                o_aug = _dot(jnp.where(own, v_pair_t, 1.0).astype(BF16), p)
                other = (1 - sub) * A_HEAD_DIM
                outs.append(o_aug * (1.0 / o_aug[other:other + 1, :]))
                if sub == 1:
                    pair_t = jnp.where(out_row < A_HEAD_DIM, outs[head - 1], outs[head])
                    o_ref[pl.ds(q0, Q_GROUP), pair_cols(head)] = pair_t.T.astype(BF16)

    return attend


def _delta_steps(q_ref, k_ref, v_ref, bg_ref, z_ref, wn_ref, o_ref, cs, gl_s, gct, st_ref, qp_s, o0_s, mnt_s):
    seq = q_ref.shape[0]
    heads = range(B_HEADS)
    n_c = SLAB // CHUNK

    row_in_chunk = lax.broadcasted_iota(jnp.int32, (seq, LANES), 0) % CHUNK
    cs[0:CHUNK, :] = jnp.zeros((CHUNK, LANES), F32)
    cs[CHUNK:, :] = bg_ref[...]
    shift = 1
    while shift < CHUNK:
        shifted = cs[CHUNK - shift:CHUNK - shift + seq, :]
        cs[CHUNK:, :] = cs[CHUNK:, :] + jnp.where(row_in_chunk >= shift, shifted, 0.0)
        shift *= 2
    gc_seq = cs[CHUNK:, :]
    g3 = gc_seq.reshape(seq // CHUNK, CHUNK, LANES)
    gl_s[...] = jnp.broadcast_to(g3[:, CHUNK - 1:CHUNK, :], g3.shape).reshape(seq, LANES)
    gc_t = gc_seq.T
    for s in range(seq // SLAB):
        gct[s] = gc_t[0:HALO, s * SLAB:(s + 1) * SLAB]
    st_ref[...] = jnp.zeros(st_ref.shape, F32)

    lane_st = lax.broadcasted_iota(jnp.int32, (CHUNK, SLAB), 1)
    i_st = lax.broadcasted_iota(jnp.int32, (CHUNK, SLAB), 0)
    j_st = lane_st % CHUNK
    lane_chunk = lane_st // CHUNK
    incl_st = i_st >= j_st
    strict_st = i_st > j_st
    eye_st = (i_st == j_st).astype(F32)
    level = [strict_st & (((i_st ^ j_st) >> l) == 1) for l in range(6)]
    kbd_mask = (lax.broadcasted_iota(jnp.int32, (SLAB, n_c * B_DIM), 0) // CHUNK
                == lax.broadcasted_iota(jnp.int32, (SLAB, n_c * B_DIM), 1) // B_DIM)

    col = lambda arr, h: arr[:, B_HEADS + h:B_HEADS + h + 1]
    hcols = [slice(h * B_DIM, (h + 1) * B_DIM) for h in heads]
    chunk_rows = [slice(c * CHUNK, (c + 1) * CHUNK) for c in range(n_c)]

    def solve_slabs(slabs):
        jobs = [(s, buf, h) for s, buf in slabs for h in heads]
        each = lambda f: [f(n) for n in range(len(jobs))]
        r0 = [pl.multiple_of(s * SLAB, SLAB) for s, _, _ in jobs]
        rows = [pl.ds(r, SLAB) for r in r0]
        head = [h for _, _, h in jobs]
        bg = each(lambda n: bg_ref[rows[n], :])
        gcs = each(lambda n: cs[pl.ds(CHUNK + r0[n], SLAB), :])
        gam = each(lambda n: col(jnp.exp(gcs[n]), head[n]))
        kd = each(lambda n: col(jnp.exp(gl_s[rows[n], :] - gcs[n]), head[n]))
        q = each(lambda n: q_ref[rows[n], hcols[head[n]]])
        k = each(lambda n: k_ref[rows[n], hcols[head[n]]])
        kf = each(lambda n: k[n].astype(F32))
        bk = each(lambda n: bg[n][:, head[n]:head[n] + 1] * kf[n])
        kbd = each(lambda n: jnp.where(kbd_mask, jnp.concatenate([k[n]] * n_c, axis=1), jnp.zeros((), BF16)))
        qk = each(lambda n: _dot_nt(jnp.concatenate([_stack(q[n]), _stack(bk[n].astype(BF16))], axis=0), kbd[n]))
        yield
        decay = each(lambda n: jnp.exp(jnp.where(
            incl_st, _stack_col(col(gcs[n], head[n]), lane_chunk)
            - gct[jobs[n][0]][B_HEADS + head[n]:B_HEADS + head[n] + 1, :], -1e30)))
        pqk = each(lambda n: (qk[n][:CHUNK] * decay[n]).astype(BF16))
        a = each(lambda n: jnp.where(strict_st, qk[n][CHUNK:] * decay[n], 0.0))
        d = each(lambda n: eye_st - jnp.where(level[0], a[n], 0.0))
        for l in range(1, 6):
            t1 = each(lambda n: _dot(jnp.where(level[l], a[n], 0.0).astype(BF16),
                                     _block_diag(d[n].astype(BF16), lane_chunk)))
            yield
            d = each(lambda n: d[n] - _dot(d[n].astype(BF16), _block_diag(t1[n].astype(BF16), lane_chunk)))
            yield
        rhs = each(lambda n: jnp.concatenate(
            [bg[n][:, head[n]:head[n] + 1] * v_ref[rows[n], hcols[head[n]]].astype(F32), gam[n] * bk[n]],
            axis=1).astype(BF16))
        sol16 = each(lambda n: _dot(_block_diag(d[n].astype(BF16), lane_chunk), rhs[n]).astype(BF16))
        yield
        x2 = each(lambda n: _dot(_block_diag(pqk[n], lane_chunk), sol16[n]))
        yield
        for n, (_, buf, h) in enumerate(jobs):
            o0_s[buf, :, hcols[h]] = x2[n][:, :B_DIM]
            qp_s[buf, :, hcols[h]] = (gam[n] * q[n].astype(F32) - x2[n][:, B_DIM:]).astype(BF16)
        kdec = each(lambda n: (kf[n] * kd[n]).astype(BF16))
        for c, cr in enumerate(chunk_rows):
            for n, (_, buf, h) in enumerate(jobs):
                mnt_s[buf, h * n_c + c] = _dot_tn(kdec[n][cr], sol16[n][cr])
            yield

    def scan_slabs(slabs):
        for s, buf in slabs:
            r0 = pl.multiple_of(s * SLAB, SLAB)
            for c, cr in enumerate(chunk_rows):
                crow = pl.ds(r0 + c * CHUNK, CHUNK)
                egl_row = jnp.exp(gl_s[pl.ds(r0 + c * CHUNK, 1), :])
                for h in heads:
                    st = st_ref[h]
                    st16 = st.astype(BF16)
                    o = _dot(qp_s[buf, cr, hcols[h]], st16) + o0_s[buf, cr, hcols[h]]
                    mnt = mnt_s[buf, h * n_c + c]
                    st_ref[h] = (egl_row[:, B_HEADS + h:B_HEADS + h + 1] * st
                                 - _dot(mnt[:, B_DIM:].astype(BF16), st16) + mnt[:, :B_DIM])
                    y = _rms(o, wn_ref[...]) * _silu(z_ref[crow, hcols[h]].astype(F32))
                    o_ref[crow, hcols[h]] = y.astype(BF16)
                yield

    return solve_slabs, scan_slabs


def _attn_kernel(q_ref, k_ref, v_ref, trow_ref, o_ref, bias, s_scr):
    seq = q_ref.shape[0]
    pad = A_LOOKBACK * CHUNK
    attend = _attention_steps(q_ref, k_ref, v_ref, trow_ref, o_ref, bias, s_scr)
    n_short = pad // Q_GROUP
    _weave(attend([(g * Q_GROUP, 0, (g + 1) * Q_GROUP) for g in range(n_short)]))

    def full_group(g, carry):
        q0 = pl.multiple_of(g * Q_GROUP, Q_GROUP)
        _weave(attend([(q0, pl.multiple_of(q0 - pad, Q_GROUP), K_WINDOW)]))
        return carry

    lax.fori_loop(n_short, seq // Q_GROUP, full_group, 0)


def _attention(qkva, trow):
    b, seq, _ = qkva.shape
    blk = lambda j: pl.BlockSpec((None, seq, A_WIDTH), lambda i, j=j: (i, 0, j))
    return pl.pallas_call(
        _attn_kernel,
        grid=(b,),
        in_specs=[blk(0), blk(1), blk(2), _const_spec(trow.shape)],
        out_specs=pl.BlockSpec((None, seq, A_WIDTH), lambda i: (i, 0, 0)),
        out_shape=jax.ShapeDtypeStruct((b, seq, A_WIDTH), BF16),
        scratch_shapes=[pltpu.VMEM((A_HEADS, K_WINDOW, Q_GROUP), F32),
                        pltpu.VMEM((SCORE_BUFFERS, K_WINDOW, Q_GROUP), F32)],
        compiler_params=_params("arbitrary"),
        name="band_attention",
    )(qkva, qkva, qkva, trow)


def _delta_kernel(q_ref, k_ref, v_ref, bg_ref, z_ref, wn_ref, o_ref, cs, gl_s, gct, st_ref, qp_s, o0_s, mnt_s):
    n_slab = q_ref.shape[0] // SLAB
    solve_slabs, scan_slabs = _delta_steps(q_ref, k_ref, v_ref, bg_ref, z_ref, wn_ref, o_ref,
                                           cs, gl_s, gct, st_ref, qp_s, o0_s, mnt_s)
    buffer_of = lambda s: s % (2 * PIPE)

    _weave(solve_slabs([(s, buffer_of(s)) for s in range(PIPE)]))

    def trip(i, carry):
        s0 = i * PIPE
        _weave(solve_slabs([(s0 + PIPE + j, buffer_of(s0 + PIPE + j)) for j in range(PIPE)]),
               scan_slabs([(s0 + j, buffer_of(s0 + j)) for j in range(PIPE)]))
        return carry

    lax.fori_loop(0, n_slab // PIPE - 1, trip, 0)
    _weave(scan_slabs([(s, buffer_of(s)) for s in range(n_slab - PIPE, n_slab)]))


def _deltanet(cqkv, bg, z, w_onorm):
    b, seq, _ = cqkv.shape
    assert (seq // SLAB) % PIPE == 0
    blk = lambda width, j: pl.BlockSpec((None, seq, width), lambda i, j=j: (i, 0, j))
    n_c = SLAB // CHUNK
    return pl.pallas_call(
        _delta_kernel,
        grid=(b,),
        in_specs=[blk(B_WIDTH, 0), blk(B_WIDTH, 1), blk(B_WIDTH, 2), blk(LANES, 0), blk(B_WIDTH, 0),
                  _const_spec((1, B_DIM))],
        out_specs=blk(B_WIDTH, 0),
        out_shape=jax.ShapeDtypeStruct((b, seq, B_WIDTH), BF16),
        scratch_shapes=[pltpu.VMEM((CHUNK + seq, LANES), F32),
                        pltpu.VMEM((seq, LANES), F32),
                        pltpu.VMEM((seq // SLAB, HALO, SLAB), F32),
                        pltpu.VMEM((B_HEADS, B_DIM, B_DIM), F32),
                        pltpu.VMEM((2 * PIPE, SLAB, B_WIDTH), BF16),
                        pltpu.VMEM((2 * PIPE, SLAB, B_WIDTH), F32),
                        pltpu.VMEM((2 * PIPE, B_HEADS * n_c, B_DIM, 2 * B_DIM), F32)],
        compiler_params=_params("arbitrary"),
        name="gated_deltanet",
    )(cqkv, cqkv, cqkv, bg, z, w_onorm)


def _merge_kernel(x_ref, ya_ref, yb_ref, g_ref, wg_ref, wa_ref, wb_ref, wo_ref, o_ref):
    branch_a = _dot(ya_ref[...], wa_ref[...])
    branch_b = _dot(yb_ref[...], wb_ref[...])
    x = x_ref[...]
    h = _rms(x, g_ref[...]).astype(BF16)
    merged = _sigmoid(_dot(h, wg_ref[:, :D_MODEL])) * branch_a
    merged = merged + _sigmoid(_dot(h, wg_ref[:, D_MODEL:])) * branch_b
    o_ref[...] = x + _dot(merged.astype(BF16), wo_ref[...])


def _merge(x2, ya, yb, g_mix, w_gates, w_a, w_b, w_out):
    t = x2.shape[0]
    tm = TOKEN_TILE
    row = lambda width: pl.BlockSpec((tm, width), lambda i: (i, 0))
    return pl.pallas_call(
        _merge_kernel,
        grid=(t // tm,),
        in_specs=[row(D_MODEL), row(A_WIDTH), row(B_WIDTH), _const_spec((1, D_MODEL)),
                  _const_spec(w_gates.shape), _const_spec(w_a.shape), _const_spec(w_b.shape),
                  _const_spec(w_out.shape)],
        out_specs=row(D_MODEL),
        out_shape=jax.ShapeDtypeStruct((t, D_MODEL), F32),
        compiler_params=_params("arbitrary"),
        name="gated_merge",
    )(x2, ya, yb, g_mix, w_gates, w_a, w_b, w_out)


def _ffn_kernel(final, x_ref, p_ref, gf_ref, wgu_ref, wd_ref, gp_ref, wpg_ref, wpp_ref, gfin_ref, o_ref, acc):
    x = x_ref[...]
    h = _rms(x, gf_ref[...]).astype(BF16)
    acc[...] = x
    for j in range(D_FF // FF_SLAB):
        gate = _dot(h, wgu_ref[:, j * FF_SLAB:(j + 1) * FF_SLAB])
        up = _dot(h, wgu_ref[:, D_FF + j * FF_SLAB:D_FF + (j + 1) * FF_SLAB])
        act = (_silu(gate) * up).astype(BF16)
        acc[...] += _dot(act, wd_ref[j * FF_SLAB:(j + 1) * FF_SLAB, :])
    x = acc[...]
    hp = _rms(x, gp_ref[...]).astype(BF16)
    ple_gate = _sigmoid(_dot(hp, wpg_ref[...]))
    x = x + ple_gate * _dot(p_ref[...].astype(BF16), wpp_ref[...])
    o_ref[...] = _rms(x, gfin_ref[...]) if final else x


def _ffn(x2, p2, g_ffn, w_gate_up, w_down, g_ple, w_ple_gate, w_ple_proj, g_final, final):
    t = x2.shape[0]
    tm = TOKEN_TILE
    row = lambda width: pl.BlockSpec((tm, width), lambda i: (i, 0))
    return pl.pallas_call(
        functools.partial(_ffn_kernel, final),
        grid=(t // tm,),
        in_specs=[row(D_MODEL), row(PLE_DIM), _const_spec((1, D_MODEL)),
                  _const_spec(w_gate_up.shape), _const_spec(w_down.shape), _const_spec((1, D_MODEL)),
                  _const_spec(w_ple_gate.shape), _const_spec(w_ple_proj.shape), _const_spec((1, D_MODEL))],
        out_specs=row(D_MODEL),
        out_shape=jax.ShapeDtypeStruct((t, D_MODEL), F32),
        scratch_shapes=[pltpu.VMEM((tm, D_MODEL), F32)],
        compiler_params=_params("arbitrary"),
        name="ffn_ple_final",
    )(x2, p2, g_ffn, w_gate_up, w_down, g_ple, w_ple_gate, w_ple_proj, g_final)


def _layer(final, x2, p2, seq, g_mix, w_in, conv_w, a_log, dt_bias, rel_bias, w_onorm, w_branch_a, w_branch_b,
           w_out, g_ffn, w_gate_up, w_down, g_ple, w_ple_gate, w_ple_proj, g_final):
    t = x2.shape[0]
    b = t // seq
    row = lambda v: v.reshape(1, -1).astype(F32)
    col_scale = jnp.where(jnp.arange(w_in.shape[1]) < A_WIDTH, A_HEAD_DIM ** -0.5 * LOG2E, 1.0).astype(F32)
    w16 = (w_in * col_scale).astype(BF16)
    w_bd = jnp.pad(w16[:, SPLIT_Z:SPLIT_DECAY], ((0, 0), (0, LANES - 2 * B_HEADS)))
    w_in_r = jnp.concatenate([w16[:, :SPLIT_Z], w_bd], axis=1)
    w_gates = w16[:, SPLIT_DECAY:]
    head_pad = lambda v: jnp.pad(v.astype(F32), (B_HEADS, LANES - 2 * B_HEADS)).reshape(1, LANES)

    qkva, cqkv, z, bg = _inproj(x2, row(g_mix), w_in_r, conv_w.astype(F32), head_pad(a_log),
                                head_pad(dt_bias), seq)
    ya = _attention(qkva.reshape(b, seq, SPLIT_A), _toeplitz_row(rel_bias))
    yb = _deltanet(cqkv.reshape(b, seq, B_CONV_CH), bg.reshape(b, seq, LANES),
                   z.reshape(b, seq, B_WIDTH), row(w_onorm))
    x2 = _merge(x2, ya.reshape(t, A_WIDTH), yb.reshape(t, B_WIDTH), row(g_mix), w_gates,
                w_branch_a.astype(BF16), w_branch_b.astype(BF16), w_out.astype(BF16))
    return _ffn(x2, p2, row(g_ffn), w_gate_up.astype(BF16), w_down.astype(BF16), row(g_ple),
                w_ple_gate.astype(BF16), w_ple_proj.astype(BF16), row(g_final), final)


def kernel(x, p, g_mix, w_in, conv_w, a_log, dt_bias, rel_bias, w_onorm, w_branch_a, w_branch_b, w_out,
           g_ffn, w_gate_up, w_down, g_ple, w_ple_gate, w_ple_proj, g_final):
    b, seq, _ = x.shape
    depth = p.shape[0]
    x2 = x.reshape(b * seq, D_MODEL)
    for i in range(depth):
        x2 = _layer(i == depth - 1, x2, p[i].reshape(b * seq, PLE_DIM), seq, g_mix[i], w_in[i], conv_w[i],
                    a_log[i], dt_bias[i], rel_bias[i], w_onorm[i], w_branch_a[i], w_branch_b[i], w_out[i],
                    g_ffn[i], w_gate_up[i], w_down[i], g_ple[i], w_ple_gate[i], w_ple_proj[i], g_final)
    return x2.reshape(b, seq, D_MODEL)
```

```python
import functools

import jax
import jax.numpy as jnp
from jax import lax
from jax.experimental import pallas as pl
from jax.experimental.pallas import tpu as pltpu

D_MODEL = 1024
CHUNK = 64
PLE_DIM = 256
EPS = 1e-6

A_HEADS = 8
A_HEAD_DIM = 64
A_WIDTH = A_HEADS * A_HEAD_DIM
A_LOOKBACK = 8
REL_CLIP = 128
LOG2E = 1.4426950408889634

B_HEADS = 4
B_DIM = 128
B_WIDTH = B_HEADS * B_DIM
CONV_WIDTH = 4
B_CONV_CH = 3 * B_WIDTH

D_FF = 2816

SPLIT_A = 3 * A_WIDTH
SPLIT_CONV = SPLIT_A + B_CONV_CH
SPLIT_Z = SPLIT_CONV + B_WIDTH
SPLIT_BETA = SPLIT_Z + B_HEADS
SPLIT_DECAY = SPLIT_BETA + B_HEADS

LANES = 128
MXU_COLS = 256
HALO = 8
TOKEN_TILE = 1024
INPROJ_TILE = 512
Q_GROUP = 256
K_WINDOW = Q_GROUP + A_LOOKBACK * CHUNK
SCORE_BUFFERS = 4
ROLL_W = 1024
SLAB = 256
PIPE = 2
FF_SLAB = 256
VMEM_LIMIT = 56 * 1024 * 1024

F32 = jnp.float32
BF16 = jnp.bfloat16
NT_DIMS = (((1,), (1,)), ((), ()))
TN_DIMS = (((0,), (0,)), ((), ()))


def _dot(a, b):
    return jnp.dot(a, b, preferred_element_type=F32)


def _dot_nt(a, b):
    return lax.dot_general(a, b, NT_DIMS, preferred_element_type=F32)


def _dot_tn(a, b):
    return lax.dot_general(a, b, TN_DIMS, preferred_element_type=F32)


def _rms(x, g):
    return x * lax.rsqrt(jnp.mean(x * x, axis=-1, keepdims=True) + EPS) * g


def _sigmoid(x):
    return 0.5 * jnp.tanh(0.5 * x) + 0.5


def _silu(x):
    h = 0.5 * x
    return h * jnp.tanh(h) + h


def _params(*sem):
    return pltpu.CompilerParams(dimension_semantics=sem, vmem_limit_bytes=VMEM_LIMIT)


def _const_spec(shape):
    nd = len(shape)
    return pl.BlockSpec(shape, lambda *_: (0,) * nd, pipeline_mode=pl.Buffered(1))


def _inproj_kernel(tiles_per_seq, x_ref, g_ref, w_ref, cw_ref, alog_ref, dtb_ref,
                   qkva_ref, cqkv_ref, z_ref, bg_ref, cbuf):
    tm = x_ref.shape[0]
    i = pl.program_id(0)
    h = _rms(x_ref[...], g_ref[...]).astype(BF16)

    @pl.when(i % tiles_per_seq == 0)
    def _():
        cbuf[0:HALO, :] = jnp.zeros((HALO, B_CONV_CH), F32)

    @pl.when(i % tiles_per_seq != 0)
    def _():
        cbuf[0:HALO, :] = cbuf[tm:tm + HALO, :]

    plain = [(qkva_ref, 0, j) for j in range(SPLIT_A // MXU_COLS)]
    plain += [(z_ref, SPLIT_CONV, j) for j in range(B_WIDTH // MXU_COLS)]

    def plain_slab(out_ref, w_col0, j):
        cols = slice(j * MXU_COLS, (j + 1) * MXU_COLS)
        out_ref[:, cols] = _dot(h, w_ref[:, w_col0 + j * MXU_COLS:w_col0 + (j + 1) * MXU_COLS]).astype(BF16)

    for j in range(B_CONV_CH // MXU_COLS):
        cols = slice(j * MXU_COLS, (j + 1) * MXU_COLS)
        cbuf[HALO:, cols] = _dot(h, w_ref[:, SPLIT_A + j * MXU_COLS:SPLIT_A + (j + 1) * MXU_COLS])
        plain_slab(*plain.pop(0))
        xs = cbuf[:, cols]
        w = [0.5 * cw_ref[t:t + 1, cols] for t in range(CONV_WIDTH)]
        xs1 = pltpu.roll(xs, 1, 0)
        hc = (pltpu.roll(w[0] * xs1 + w[1] * xs, 2, 0) + (w[2] * xs1 + w[3] * xs))[HALO:]
        c = hc * jnp.tanh(hc) + hc
        for half in range(MXU_COLS // LANES):
            head = j * (MXU_COLS // LANES) + half
            ch = c[:, half * LANES:(half + 1) * LANES]
            if head < 2 * B_HEADS:
                scale = B_DIM ** -0.5 if head < B_HEADS else 1.0
                ch = ch * (lax.rsqrt(jnp.sum(ch * ch, axis=-1, keepdims=True) + EPS) * scale)
            cqkv_ref[:, head * LANES:(head + 1) * LANES] = ch.astype(BF16)

    raw = _dot(h, w_ref[:, SPLIT_Z:SPLIT_Z + LANES])
    lane = lax.broadcasted_iota(jnp.int32, raw.shape, 1)
    sp_in = raw + dtb_ref[...]
    softplus = jnp.maximum(sp_in, 0.0) + jnp.log1p(jnp.exp(-jnp.abs(sp_in)))
    bg_ref[...] = jnp.where(lane < B_HEADS, _sigmoid(raw), -jnp.exp(alog_ref[...]) * softplus)

    for args in plain:
        plain_slab(*args)


def _inproj(x2, g_mix, w_in_r, conv_w, alog_pad, dtb_pad, seq):
    t = x2.shape[0]
    tm = INPROJ_TILE
    row = lambda width: pl.BlockSpec((tm, width), lambda i: (i, 0))
    return pl.pallas_call(
        functools.partial(_inproj_kernel, seq // tm),
        grid=(t // tm,),
        in_specs=[row(D_MODEL), _const_spec((1, D_MODEL)), _const_spec(w_in_r.shape),
                  _const_spec((CONV_WIDTH, B_CONV_CH)), _const_spec((1, LANES)), _const_spec((1, LANES))],
        out_specs=[row(SPLIT_A), row(B_CONV_CH), row(B_WIDTH), row(LANES)],
        out_shape=[jax.ShapeDtypeStruct((t, SPLIT_A), BF16), jax.ShapeDtypeStruct((t, B_CONV_CH), BF16),
                   jax.ShapeDtypeStruct((t, B_WIDTH), BF16), jax.ShapeDtypeStruct((t, LANES), F32)],
        scratch_shapes=[pltpu.VMEM((tm + HALO, B_CONV_CH), F32)],
        compiler_params=_params("arbitrary"),
        name="inproj",
    )(x2, g_mix, w_in_r, conv_w, alog_pad, dtb_pad)


def _toeplitz_row(rel_bias):
    n_far = K_WINDOW - 1 - REL_CLIP
    n_near = ROLL_W - n_far - (2 * REL_CLIP + 1)
    far = jnp.broadcast_to(rel_bias[:, -1:], (A_HEADS, n_far))
    near = jnp.broadcast_to(rel_bias[:, :1], (A_HEADS, n_near))
    t = jnp.concatenate([far, rel_bias[:, ::-1], near], axis=1).astype(F32)
    return t.reshape(A_HEADS, 1, ROLL_W)


def _stack(x):
    return jnp.concatenate([x[c * CHUNK:(c + 1) * CHUNK] for c in range(SLAB // CHUNK)], axis=1)


def _stack_col(col, lane_chunk):
    out = jnp.broadcast_to(col[0:CHUNK], (CHUNK, SLAB))
    for c in range(1, SLAB // CHUNK):
        out = jnp.where(lane_chunk == c, jnp.broadcast_to(col[c * CHUNK:(c + 1) * CHUNK], (CHUNK, SLAB)), out)
    return out


def _block_diag(x_st, lane_chunk):
    zero = jnp.zeros_like(x_st)
    return jnp.concatenate([jnp.where(lane_chunk == c, x_st, zero) for c in range(SLAB // CHUNK)], axis=0)


def _weave(*generators):
    live = list(generators)
    while live:
        live = [g for g in live if next(g, live) is not live]


def _attention_steps(q_ref, k_ref, v_ref, trow_ref, o_ref, bias_t, s_scr):
    lane = lax.broadcasted_iota(jnp.int32, (Q_GROUP, LANES), 1)

    @pl.when(pl.program_id(0) == 0)
    def _():
        qc = lax.broadcasted_iota(jnp.int32, (Q_GROUP, K_WINDOW), 0) // CHUNK
        kc = lax.broadcasted_iota(jnp.int32, (Q_GROUP, K_WINDOW), 1) // CHUNK
        in_band = (kc >= qc) & (kc <= qc + A_LOOKBACK)
        for h in range(A_HEADS):
            base = jnp.broadcast_to(trow_ref[h], (Q_GROUP, ROLL_W))
            toeplitz = pltpu.roll(base, ROLL_W - (Q_GROUP - 1), 1, stride=1, stride_axis=0)
            bias_t[h] = jnp.where(in_band, toeplitz[:, :K_WINDOW] * LOG2E, -1e30).T

    pair_cols = lambda head: slice(head // 2 * LANES, (head // 2 + 1) * LANES)

    def attend(groups):
        for q0, k0, n_keys in groups:
            b0 = K_WINDOW - n_keys
            value_row = lax.broadcasted_iota(jnp.int32, (LANES, n_keys), 0)
            out_row = lax.broadcasted_iota(jnp.int32, (LANES, Q_GROUP), 0)

            def scores(head):
                sub = head % 2
                q2 = q_ref[pl.ds(q0, Q_GROUP), pair_cols(head)]
                head_lanes = (lane >= sub * A_HEAD_DIM) & (lane < (sub + 1) * A_HEAD_DIM)
                qm = jnp.where(head_lanes, q2, jnp.zeros_like(q2))
                s_scr[head % SCORE_BUFFERS, 0:n_keys, :] = _dot_nt(k_ref[pl.ds(k0, n_keys), pair_cols(head)], qm)

            for head in range(SCORE_BUFFERS - 1):
                scores(head)
            outs = []
            for head in range(A_HEADS):
                sub = head % 2
                buf = head % SCORE_BUFFERS
                if head + SCORE_BUFFERS - 1 < A_HEADS:
                    scores(head + SCORE_BUFFERS - 1)
                yield
                biased = s_scr[buf, 0:n_keys, :] + bias_t[head, b0:, :]
                s_scr[buf, 0:n_keys, :] = biased
                m = jnp.max(biased, axis=0, keepdims=True)
                p = jnp.exp2(s_scr[buf, 0:n_keys, :] - m).astype(BF16)
                if sub == 0:
                    v_pair_t = v_ref[pl.ds(k0, n_keys), pair_cols(head)].astype(F32).T
                own = (value_row >= sub * A_HEAD_DIM) & (value_row < (sub + 1) * A_HEAD_DIM)
                o_aug = _dot(jnp.where(own, v_pair_t, 1.0).astype(BF16), p)
                other = (1 - sub) * A_HEAD_DIM
                outs.append(o_aug * (1.0 / o_aug[other:other + 1, :]))
                if sub == 1:
                    pair_t = jnp.where(out_row < A_HEAD_DIM, outs[head - 1], outs[head])
                    o_ref[pl.ds(q0, Q_GROUP), pair_cols(head)] = pair_t.T.astype(BF16)

    return attend


def _delta_steps(q_ref, k_ref, v_ref, bg_ref, z_ref, wn_ref, o_ref, cs, gl_s, gct, st_ref, qp_s, o0_s, mnt_s):
    seq = q_ref.shape[0]
    heads = range(B_HEADS)
    n_c = SLAB // CHUNK

    row_in_chunk = lax.broadcasted_iota(jnp.int32, (seq, LANES), 0) % CHUNK
    cs[0:CHUNK, :] = jnp.zeros((CHUNK, LANES), F32)
    cs[CHUNK:, :] = bg_ref[...]
    shift = 1
    while shift < CHUNK:
        shifted = cs[CHUNK - shift:CHUNK - shift + seq, :]
        cs[CHUNK:, :] = cs[CHUNK:, :] + jnp.where(row_in_chunk >= shift, shifted, 0.0)
        shift *= 2
    gc_seq = cs[CHUNK:, :]
    g3 = gc_seq.reshape(seq // CHUNK, CHUNK, LANES)
    gl_s[...] = jnp.broadcast_to(g3[:, CHUNK - 1:CHUNK, :], g3.shape).reshape(seq, LANES)
    gc_t = gc_seq.T
    for s in range(seq // SLAB):
        gct[s] = gc_t[0:HALO, s * SLAB:(s + 1) * SLAB]
    st_ref[...] = jnp.zeros(st_ref.shape, F32)

    lane_st = lax.broadcasted_iota(jnp.int32, (CHUNK, SLAB), 1)
    i_st = lax.broadcasted_iota(jnp.int32, (CHUNK, SLAB), 0)
    j_st = lane_st % CHUNK
    lane_chunk = lane_st // CHUNK
    incl_st = i_st >= j_st
    strict_st = i_st > j_st
    eye_st = (i_st == j_st).astype(F32)
    level = [strict_st & (((i_st ^ j_st) >> l) == 1) for l in range(6)]
    kbd_mask = (lax.broadcasted_iota(jnp.int32, (SLAB, n_c * B_DIM), 0) // CHUNK
                == lax.broadcasted_iota(jnp.int32, (SLAB, n_c * B_DIM), 1) // B_DIM)

    col = lambda arr, h: arr[:, B_HEADS + h:B_HEADS + h + 1]
    hcols = [slice(h * B_DIM, (h + 1) * B_DIM) for h in heads]
    chunk_rows = [slice(c * CHUNK, (c + 1) * CHUNK) for c in range(n_c)]

    def solve_slabs(slabs):
        jobs = [(s, buf, h) for s, buf in slabs for h in heads]
        each = lambda f: [f(n) for n in range(len(jobs))]
        r0 = [pl.multiple_of(s * SLAB, SLAB) for s, _, _ in jobs]
        rows = [pl.ds(r, SLAB) for r in r0]
        head = [h for _, _, h in jobs]
        bg = each(lambda n: bg_ref[rows[n], :])
        gcs = each(lambda n: cs[pl.ds(CHUNK + r0[n], SLAB), :])
        gam = each(lambda n: col(jnp.exp(gcs[n]), head[n]))
        kd = each(lambda n: col(jnp.exp(gl_s[rows[n], :] - gcs[n]), head[n]))
        q = each(lambda n: q_ref[rows[n], hcols[head[n]]])
        k = each(lambda n: k_ref[rows[n], hcols[head[n]]])
        kf = each(lambda n: k[n].astype(F32))
        bk = each(lambda n: bg[n][:, head[n]:head[n] + 1] * kf[n])
        kbd = each(lambda n: jnp.where(kbd_mask, jnp.concatenate([k[n]] * n_c, axis=1), jnp.zeros((), BF16)))
        qk = each(lambda n: _dot_nt(jnp.concatenate([_stack(q[n]), _stack(bk[n].astype(BF16))], axis=0), kbd[n]))
        yield
        decay = each(lambda n: jnp.exp(jnp.where(
            incl_st, _stack_col(col(gcs[n], head[n]), lane_chunk)
            - gct[jobs[n][0]][B_HEADS + head[n]:B_HEADS + head[n] + 1, :], -1e30)))
        pqk = each(lambda n: (qk[n][:CHUNK] * decay[n]).astype(BF16))
        a = each(lambda n: jnp.where(strict_st, qk[n][CHUNK:] * decay[n], 0.0))
        d = each(lambda n: eye_st - jnp.where(level[0], a[n], 0.0))
        for l in range(1, 6):
            t1 = each(lambda n: _dot(jnp.where(level[l], a[n], 0.0).astype(BF16),
                                     _block_diag(d[n].astype(BF16), lane_chunk)))
            yield
            d = each(lambda n: d[n] - _dot(d[n].astype(BF16), _block_diag(t1[n].astype(BF16), lane_chunk)))
            yield
        rhs = each(lambda n: jnp.concatenate(
            [bg[n][:, head[n]:head[n] + 1] * v_ref[rows[n], hcols[head[n]]].astype(F32), gam[n] * bk[n]],
            axis=1).astype(BF16))
        sol16 = each(lambda n: _dot(_block_diag(d[n].astype(BF16), lane_chunk), rhs[n]).astype(BF16))
        yield
        x2 = each(lambda n: _dot(_block_diag(pqk[n], lane_chunk), sol16[n]))
        yield
        for n, (_, buf, h) in enumerate(jobs):
            o0_s[buf, :, hcols[h]] = x2[n][:, :B_DIM]
            qp_s[buf, :, hcols[h]] = (gam[n] * q[n].astype(F32) - x2[n][:, B_DIM:]).astype(BF16)
        kdec = each(lambda n: (kf[n] * kd[n]).astype(BF16))
        for c, cr in enumerate(chunk_rows):
            for n, (_, buf, h) in enumerate(jobs):
                mnt_s[buf, h * n_c + c] = _dot_tn(kdec[n][cr], sol16[n][cr])
            yield

    def scan_slabs(slabs):
        for s, buf in slabs:
            r0 = pl.multiple_of(s * SLAB, SLAB)
            for c, cr in enumerate(chunk_rows):
                crow = pl.ds(r0 + c * CHUNK, CHUNK)
                egl_row = jnp.exp(gl_s[pl.ds(r0 + c * CHUNK, 1), :])
                for h in heads:
                    st = st_ref[h]
                    st16 = st.astype(BF16)
                    o = _dot(qp_s[buf, cr, hcols[h]], st16) + o0_s[buf, cr, hcols[h]]
                    mnt = mnt_s[buf, h * n_c + c]
                    st_ref[h] = (egl_row[:, B_HEADS + h:B_HEADS + h + 1] * st
                                 - _dot(mnt[:, B_DIM:].astype(BF16), st16) + mnt[:, :B_DIM])
                    y = _rms(o, wn_ref[...]) * _silu(z_ref[crow, hcols[h]].astype(F32))
                    o_ref[crow, hcols[h]] = y.astype(BF16)
                yield

    return solve_slabs, scan_slabs


def _attn_kernel(q_ref, k_ref, v_ref, trow_ref, o_ref, bias, s_scr):
    seq = q_ref.shape[0]
    pad = A_LOOKBACK * CHUNK
    attend = _attention_steps(q_ref, k_ref, v_ref, trow_ref, o_ref, bias, s_scr)
    n_short = pad // Q_GROUP
    _weave(attend([(g * Q_GROUP, 0, (g + 1) * Q_GROUP) for g in range(n_short)]))

    def full_group(g, carry):
        q0 = pl.multiple_of(g * Q_GROUP, Q_GROUP)
        _weave(attend([(q0, pl.multiple_of(q0 - pad, Q_GROUP), K_WINDOW)]))
        return carry

    lax.fori_loop(n_short, seq // Q_GROUP, full_group, 0)


def _attention(qkva, trow):
    b, seq, _ = qkva.shape
    blk = lambda j: pl.BlockSpec((None, seq, A_WIDTH), lambda i, j=j: (i, 0, j))
    return pl.pallas_call(
        _attn_kernel,
        grid=(b,),
        in_specs=[blk(0), blk(1), blk(2), _const_spec(trow.shape)],
        out_specs=pl.BlockSpec((None, seq, A_WIDTH), lambda i: (i, 0, 0)),
        out_shape=jax.ShapeDtypeStruct((b, seq, A_WIDTH), BF16),
        scratch_shapes=[pltpu.VMEM((A_HEADS, K_WINDOW, Q_GROUP), F32),
                        pltpu.VMEM((SCORE_BUFFERS, K_WINDOW, Q_GROUP), F32)],
        compiler_params=_params("arbitrary"),
        name="band_attention",
    )(qkva, qkva, qkva, trow)


def _delta_kernel(q_ref, k_ref, v_ref, bg_ref, z_ref, wn_ref, o_ref, cs, gl_s, gct, st_ref, qp_s, o0_s, mnt_s):
    n_slab = q_ref.shape[0] // SLAB
    solve_slabs, scan_slabs = _delta_steps(q_ref, k_ref, v_ref, bg_ref, z_ref, wn_ref, o_ref,
                                           cs, gl_s, gct, st_ref, qp_s, o0_s, mnt_s)
    buffer_of = lambda s: s % (2 * PIPE)

    _weave(solve_slabs([(s, buffer_of(s)) for s in range(PIPE)]))

    def trip(i, carry):
        s0 = i * PIPE
        _weave(solve_slabs([(s0 + PIPE + j, buffer_of(s0 + PIPE + j)) for j in range(PIPE)]),
               scan_slabs([(s0 + j, buffer_of(s0 + j)) for j in range(PIPE)]))
        return carry

    lax.fori_loop(0, n_slab // PIPE - 1, trip, 0)
    _weave(scan_slabs([(s, buffer_of(s)) for s in range(n_slab - PIPE, n_slab)]))


def _deltanet(cqkv, bg, z, w_onorm):
    b, seq, _ = cqkv.shape
    assert (seq // SLAB) % PIPE == 0
    blk = lambda width, j: pl.BlockSpec((None, seq, width), lambda i, j=j: (i, 0, j))
    n_c = SLAB // CHUNK
    return pl.pallas_call(
        _delta_kernel,
        grid=(b,),
        in_specs=[blk(B_WIDTH, 0), blk(B_WIDTH, 1), blk(B_WIDTH, 2), blk(LANES, 0), blk(B_WIDTH, 0),
                  _const_spec((1, B_DIM))],
        out_specs=blk(B_WIDTH, 0),
        out_shape=jax.ShapeDtypeStruct((b, seq, B_WIDTH), BF16),
        scratch_shapes=[pltpu.VMEM((CHUNK + seq, LANES), F32),
                        pltpu.VMEM((seq, LANES), F32),
                        pltpu.VMEM((seq // SLAB, HALO, SLAB), F32),
                        pltpu.VMEM((B_HEADS, B_DIM, B_DIM), F32),
                        pltpu.VMEM((2 * PIPE, SLAB, B_WIDTH), BF16),
                        pltpu.VMEM((2 * PIPE, SLAB, B_WIDTH), F32),
                        pltpu.VMEM((2 * PIPE, B_HEADS * n_c, B_DIM, 2 * B_DIM), F32)],
        compiler_params=_params("arbitrary"),
        name="gated_deltanet",
    )(cqkv, cqkv, cqkv, bg, z, w_onorm)


def _merge_kernel(x_ref, ya_ref, yb_ref, g_ref, wg_ref, wa_ref, wb_ref, wo_ref, o_ref):
    branch_a = _dot(ya_ref[...], wa_ref[...])
    branch_b = _dot(yb_ref[...], wb_ref[...])
    x = x_ref[...]
    h = _rms(x, g_ref[...]).astype(BF16)
    merged = _sigmoid(_dot(h, wg_ref[:, :D_MODEL])) * branch_a
    merged = merged + _sigmoid(_dot(h, wg_ref[:, D_MODEL:])) * branch_b
    o_ref[...] = x + _dot(merged.astype(BF16), wo_ref[...])


def _merge(x2, ya, yb, g_mix, w_gates, w_a, w_b, w_out):
    t = x2.shape[0]
    tm = TOKEN_TILE
    row = lambda width: pl.BlockSpec((tm, width), lambda i: (i, 0))
    return pl.pallas_call(
        _merge_kernel,
        grid=(t // tm,),
        in_specs=[row(D_MODEL), row(A_WIDTH), row(B_WIDTH), _const_spec((1, D_MODEL)),
                  _const_spec(w_gates.shape), _const_spec(w_a.shape), _const_spec(w_b.shape),
                  _const_spec(w_out.shape)],
        out_specs=row(D_MODEL),
        out_shape=jax.ShapeDtypeStruct((t, D_MODEL), F32),
        compiler_params=_params("arbitrary"),
        name="gated_merge",
    )(x2, ya, yb, g_mix, w_gates, w_a, w_b, w_out)


def _ffn_kernel(final, x_ref, p_ref, gf_ref, wgu_ref, wd_ref, gp_ref, wpg_ref, wpp_ref, gfin_ref, o_ref, acc):
    x = x_ref[...]
    h = _rms(x, gf_ref[...]).astype(BF16)
    acc[...] = x
    for j in range(D_FF // FF_SLAB):
        gate = _dot(h, wgu_ref[:, j * FF_SLAB:(j + 1) * FF_SLAB])
        up = _dot(h, wgu_ref[:, D_FF + j * FF_SLAB:D_FF + (j + 1) * FF_SLAB])
        act = (_silu(gate) * up).astype(BF16)
        acc[...] += _dot(act, wd_ref[j * FF_SLAB:(j + 1) * FF_SLAB, :])
    x = acc[...]
    hp = _rms(x, gp_ref[...]).astype(BF16)
    ple_gate = _sigmoid(_dot(hp, wpg_ref[...]))
    x = x + ple_gate * _dot(p_ref[...].astype(BF16), wpp_ref[...])
    o_ref[...] = _rms(x, gfin_ref[...]) if final else x


def _ffn(x2, p2, g_ffn, w_gate_up, w_down, g_ple, w_ple_gate, w_ple_proj, g_final, final):
    t = x2.shape[0]
    tm = TOKEN_TILE
    row = lambda width: pl.BlockSpec((tm, width), lambda i: (i, 0))
    return pl.pallas_call(
        functools.partial(_ffn_kernel, final),
        grid=(t // tm,),
        in_specs=[row(D_MODEL), row(PLE_DIM), _const_spec((1, D_MODEL)),
                  _const_spec(w_gate_up.shape), _const_spec(w_down.shape), _const_spec((1, D_MODEL)),
                  _const_spec(w_ple_gate.shape), _const_spec(w_ple_proj.shape), _const_spec((1, D_MODEL))],
        out_specs=row(D_MODEL),
        out_shape=jax.ShapeDtypeStruct((t, D_MODEL), F32),
        scratch_shapes=[pltpu.VMEM((tm, D_MODEL), F32)],
        compiler_params=_params("arbitrary"),
        name="ffn_ple_final",
    )(x2, p2, g_ffn, w_gate_up, w_down, g_ple, w_ple_gate, w_ple_proj, g_final)


def _layer(final, x2, p2, seq, g_mix, w_in, conv_w, a_log, dt_bias, rel_bias, w_onorm, w_branch_a, w_branch_b,
           w_out, g_ffn, w_gate_up, w_down, g_ple, w_ple_gate, w_ple_proj, g_final):
    t = x2.shape[0]
    b = t // seq
    row = lambda v: v.reshape(1, -1).astype(F32)
    col_scale = jnp.where(jnp.arange(w_in.shape[1]) < A_WIDTH, A_HEAD_DIM ** -0.5 * LOG2E, 1.0).astype(F32)
    w16 = (w_in * col_scale).astype(BF16)
    w_bd = jnp.pad(w16[:, SPLIT_Z:SPLIT_DECAY], ((0, 0), (0, LANES - 2 * B_HEADS)))
    w_in_r = jnp.concatenate([w16[:, :SPLIT_Z], w_bd], axis=1)
    w_gates = w16[:, SPLIT_DECAY:]
    head_pad = lambda v: jnp.pad(v.astype(F32), (B_HEADS, LANES - 2 * B_HEADS)).reshape(1, LANES)

    qkva, cqkv, z, bg = _inproj(x2, row(g_mix), w_in_r, conv_w.astype(F32), head_pad(a_log),
                                head_pad(dt_bias), seq)
    ya = _attention(qkva.reshape(b, seq, SPLIT_A), _toeplitz_row(rel_bias))
    yb = _deltanet(cqkv.reshape(b, seq, B_CONV_CH), bg.reshape(b, seq, LANES),
                   z.reshape(b, seq, B_WIDTH), row(w_onorm))
    x2 = _merge(x2, ya.reshape(t, A_WIDTH), yb.reshape(t, B_WIDTH), row(g_mix), w_gates,
                w_branch_a.astype(BF16), w_branch_b.astype(BF16), w_out.astype(BF16))
    return _ffn(x2, p2, row(g_ffn), w_gate_up.astype(BF16), w_down.astype(BF16), row(g_ple),
                w_ple_gate.astype(BF16), w_ple_proj.astype(BF16), row(g_final), final)


def kernel(x, p, g_mix, w_in, conv_w, a_log, dt_bias, rel_bias, w_onorm, w_branch_a, w_branch_b, w_out,
           g_ffn, w_gate_up, w_down, g_ple, w_ple_gate, w_ple_proj, g_final):
    b, seq, _ = x.shape
    depth = p.shape[0]
    x2 = x.reshape(b * seq, D_MODEL)
    for i in range(depth):
        x2 = _layer(i == depth - 1, x2, p[i].reshape(b * seq, PLE_DIM), seq, g_mix[i], w_in[i], conv_w[i],
                    a_log[i], dt_bias[i], rel_bias[i], w_onorm[i], w_branch_a[i], w_branch_b[i], w_out[i],
                    g_ffn[i], w_gate_up[i], w_down[i], g_ple[i], w_ple_gate[i], w_ple_proj[i], g_final)
    return x2.reshape(b, seq, D_MODEL)
```

```python
import functools

import jax
import jax.numpy as jnp
from jax import lax
from jax.experimental import pallas as pl
from jax.experimental.pallas import tpu as pltpu

D_MODEL = 1024
CHUNK = 64
PLE_DIM = 256
EPS = 1e-6

A_HEADS = 8
A_HEAD_DIM = 64
A_WIDTH = A_HEADS * A_HEAD_DIM
A_LOOKBACK = 8
REL_CLIP = 128
LOG2E = 1.4426950408889634

B_HEADS = 4
B_DIM = 128
B_WIDTH = B_HEADS * B_DIM
CONV_WIDTH = 4
B_CONV_CH = 3 * B_WIDTH

D_FF = 2816

SPLIT_A = 3 * A_WIDTH
SPLIT_CONV = SPLIT_A + B_CONV_CH
SPLIT_Z = SPLIT_CONV + B_WIDTH
SPLIT_BETA = SPLIT_Z + B_HEADS
SPLIT_DECAY = SPLIT_BETA + B_HEADS

LANES = 128
MXU_COLS = 256
HALO = 8
TOKEN_TILE = 1024
INPROJ_TILE = 512
Q_GROUP = 256
K_WINDOW = Q_GROUP + A_LOOKBACK * CHUNK
SCORE_BUFFERS = 6
ROLL_W = 1024
SLAB = 256
PIPE = 2
FF_SLAB = 256
VMEM_LIMIT = 56 * 1024 * 1024

F32 = jnp.float32
BF16 = jnp.bfloat16
NT_DIMS = (((1,), (1,)), ((), ()))
TN_DIMS = (((0,), (0,)), ((), ()))


def _dot(a, b):
    return jnp.dot(a, b, preferred_element_type=F32)


def _dot_nt(a, b):
    return lax.dot_general(a, b, NT_DIMS, preferred_element_type=F32)


def _dot_tn(a, b):
    return lax.dot_general(a, b, TN_DIMS, preferred_element_type=F32)


def _rms(x, g):
    return x * lax.rsqrt(jnp.mean(x * x, axis=-1, keepdims=True) + EPS) * g


def _sigmoid(x):
    return 0.5 * jnp.tanh(0.5 * x) + 0.5


def _silu(x):
    h = 0.5 * x
    return h * jnp.tanh(h) + h


def _params(*sem):
    return pltpu.CompilerParams(dimension_semantics=sem, vmem_limit_bytes=VMEM_LIMIT)


def _const_spec(shape):
    nd = len(shape)
    return pl.BlockSpec(shape, lambda *_: (0,) * nd, pipeline_mode=pl.Buffered(1))


def _inproj_kernel(tiles_per_seq, x_ref, g_ref, w_ref, cw_ref, alog_ref, dtb_ref,
                   qkva_ref, cqkv_ref, z_ref, bg_ref, cbuf):
    tm = x_ref.shape[0]
    i = pl.program_id(0)
    h = _rms(x_ref[...], g_ref[...]).astype(BF16)

    @pl.when(i % tiles_per_seq == 0)
    def _():
        cbuf[0:HALO, :] = jnp.zeros((HALO, B_CONV_CH), F32)

    @pl.when(i % tiles_per_seq != 0)
    def _():
        cbuf[0:HALO, :] = cbuf[tm:tm + HALO, :]

    plain = [(qkva_ref, 0, j) for j in range(SPLIT_A // MXU_COLS)]
    plain += [(z_ref, SPLIT_CONV, j) for j in range(B_WIDTH // MXU_COLS)]

    def plain_slab(out_ref, w_col0, j):
        cols = slice(j * MXU_COLS, (j + 1) * MXU_COLS)
        out_ref[:, cols] = _dot(h, w_ref[:, w_col0 + j * MXU_COLS:w_col0 + (j + 1) * MXU_COLS]).astype(BF16)

    for j in range(B_CONV_CH // MXU_COLS):
        cols = slice(j * MXU_COLS, (j + 1) * MXU_COLS)
        cbuf[HALO:, cols] = _dot(h, w_ref[:, SPLIT_A + j * MXU_COLS:SPLIT_A + (j + 1) * MXU_COLS])
        plain_slab(*plain.pop(0))
        xs = cbuf[:, cols]
        w = [0.5 * cw_ref[t:t + 1, cols] for t in range(CONV_WIDTH)]
        xs1 = pltpu.roll(xs, 1, 0)
        hc = (pltpu.roll(w[0] * xs1 + w[1] * xs, 2, 0) + (w[2] * xs1 + w[3] * xs))[HALO:]
        c = hc * jnp.tanh(hc) + hc
        for half in range(MXU_COLS // LANES):
            head = j * (MXU_COLS // LANES) + half
            ch = c[:, half * LANES:(half + 1) * LANES]
            if head < 2 * B_HEADS:
                scale = B_DIM ** -0.5 if head < B_HEADS else 1.0
                ch = ch * (lax.rsqrt(jnp.sum(ch * ch, axis=-1, keepdims=True) + EPS) * scale)
            cqkv_ref[:, head * LANES:(head + 1) * LANES] = ch.astype(BF16)

    raw = _dot(h, w_ref[:, SPLIT_Z:SPLIT_Z + LANES])
    lane = lax.broadcasted_iota(jnp.int32, raw.shape, 1)
    sp_in = raw + dtb_ref[...]
    softplus = jnp.maximum(sp_in, 0.0) + jnp.log1p(jnp.exp(-jnp.abs(sp_in)))
    bg_ref[...] = jnp.where(lane < B_HEADS, _sigmoid(raw), -jnp.exp(alog_ref[...]) * softplus)

    for args in plain:
        plain_slab(*args)


def _inproj(x2, g_mix, w_in_r, conv_w, alog_pad, dtb_pad, seq):
    t = x2.shape[0]
    tm = INPROJ_TILE
    assert seq % tm == 0 and t % seq == 0
    row = lambda width: pl.BlockSpec((tm, width), lambda i: (i, 0))
    return pl.pallas_call(
        functools.partial(_inproj_kernel, seq // tm),
        grid=(t // tm,),
        in_specs=[row(D_MODEL), _const_spec((1, D_MODEL)), _const_spec(w_in_r.shape),
                  _const_spec((CONV_WIDTH, B_CONV_CH)), _const_spec((1, LANES)), _const_spec((1, LANES))],
        out_specs=[row(SPLIT_A), row(B_CONV_CH), row(B_WIDTH), row(LANES)],
        out_shape=[jax.ShapeDtypeStruct((t, SPLIT_A), BF16), jax.ShapeDtypeStruct((t, B_CONV_CH), BF16),
                   jax.ShapeDtypeStruct((t, B_WIDTH), BF16), jax.ShapeDtypeStruct((t, LANES), F32)],
        scratch_shapes=[pltpu.VMEM((tm + HALO, B_CONV_CH), F32)],
        compiler_params=_params("arbitrary"),
        name="inproj",
    )(x2, g_mix, w_in_r, conv_w, alog_pad, dtb_pad)


def _toeplitz_row(rel_bias):
    n_far = K_WINDOW - 1 - REL_CLIP
    n_near = ROLL_W - n_far - (2 * REL_CLIP + 1)
    far = jnp.broadcast_to(rel_bias[:, -1:], (A_HEADS, n_far))
    near = jnp.broadcast_to(rel_bias[:, :1], (A_HEADS, n_near))
    t = jnp.concatenate([far, rel_bias[:, ::-1], near], axis=1).astype(F32)
    return t.reshape(A_HEADS, 1, ROLL_W)


def _stack(x):
    return jnp.concatenate([x[c * CHUNK:(c + 1) * CHUNK] for c in range(SLAB // CHUNK)], axis=1)


def _stack_col(col, lane_chunk):
    out = jnp.broadcast_to(col[0:CHUNK], (CHUNK, SLAB))
    for c in range(1, SLAB // CHUNK):
        out = jnp.where(lane_chunk == c, jnp.broadcast_to(col[c * CHUNK:(c + 1) * CHUNK], (CHUNK, SLAB)), out)
    return out


def _block_diag(x_st, lane_chunk):
    zero = jnp.zeros_like(x_st)
    return jnp.concatenate([jnp.where(lane_chunk == c, x_st, zero) for c in range(SLAB // CHUNK)], axis=0)


def _weave(*generators):
    live = list(generators)
    while live:
        live = [g for g in live if next(g, live) is not live]


def _attention_steps(q_ref, k_ref, v_ref, trow_ref, o_ref, bias_t, s_scr):
    lane = lax.broadcasted_iota(jnp.int32, (Q_GROUP, LANES), 1)

    @pl.when(pl.program_id(0) == 0)
    def _():
        qc = lax.broadcasted_iota(jnp.int32, (Q_GROUP, K_WINDOW), 0) // CHUNK
        kc = lax.broadcasted_iota(jnp.int32, (Q_GROUP, K_WINDOW), 1) // CHUNK
        in_band = (kc >= qc) & (kc <= qc + A_LOOKBACK)
        for h in range(A_HEADS):
            base = jnp.broadcast_to(trow_ref[h], (Q_GROUP, ROLL_W))
            toeplitz = pltpu.roll(base, ROLL_W - (Q_GROUP - 1), 1, stride=1, stride_axis=0)
            bias_t[h] = jnp.where(in_band, toeplitz[:, :K_WINDOW] * LOG2E, -1e30).T

    pair_cols = lambda head: slice(head // 2 * LANES, (head // 2 + 1) * LANES)

    def attend(groups):
        for q0, k0, n_keys in groups:
            b0 = K_WINDOW - n_keys
            value_row = lax.broadcasted_iota(jnp.int32, (LANES, n_keys), 0)
            out_row = lax.broadcasted_iota(jnp.int32, (LANES, Q_GROUP), 0)

            def scores(head):
                sub = head % 2
                q2 = q_ref[pl.ds(q0, Q_GROUP), pair_cols(head)]
                head_lanes = (lane >= sub * A_HEAD_DIM) & (lane < (sub + 1) * A_HEAD_DIM)
                qm = jnp.where(head_lanes, q2, jnp.zeros_like(q2))
                s_scr[head % SCORE_BUFFERS, 0:n_keys, :] = _dot_nt(k_ref[pl.ds(k0, n_keys), pair_cols(head)], qm)

            for head in range(SCORE_BUFFERS - 1):
                scores(head)
            outs = []
            for head in range(A_HEADS):
                sub = head % 2
                buf = head % SCORE_BUFFERS
                if head + SCORE_BUFFERS - 1 < A_HEADS:
                    scores(head + SCORE_BUFFERS - 1)
                yield
                biased = s_scr[buf, 0:n_keys, :] + bias_t[head, b0:, :]
                s_scr[buf, 0:n_keys, :] = biased
                m = jnp.max(biased, axis=0, keepdims=True)
                p = jnp.exp2(s_scr[buf, 0:n_keys, :] - m).astype(BF16)
                if sub == 0:
                    v_pair_t = v_ref[pl.ds(k0, n_keys), pair_cols(head)].astype(F32).T
                own = (value_row >= sub * A_HEAD_DIM) & (value_row < (sub + 1) * A_HEAD_DIM)
                o_aug = _dot(jnp.where(own, v_pair_t, 1.0).astype(BF16), p)
                other = (1 - sub) * A_HEAD_DIM
                outs.append(o_aug * (1.0 / o_aug[other:other + 1, :]))
                if sub == 1:
                    pair_t = jnp.where(out_row < A_HEAD_DIM, outs[head - 1], outs[head])
                    o_ref[pl.ds(q0, Q_GROUP), pair_cols(head)] = pair_t.T.astype(BF16)

    return attend


def _delta_steps(q_ref, k_ref, v_ref, bg_ref, z_ref, wn_ref, o_ref, cs, gl_s, gct, st_ref, qp_s, o0_s, mnt_s):
    seq = q_ref.shape[0]
    heads = range(B_HEADS)
    n_c = SLAB // CHUNK

    row_in_chunk = lax.broadcasted_iota(jnp.int32, (seq, LANES), 0) % CHUNK
    cs[0:CHUNK, :] = jnp.zeros((CHUNK, LANES), F32)
    cs[CHUNK:, :] = bg_ref[...]
    shift = 1
    while shift < CHUNK:
        shifted = cs[CHUNK - shift:CHUNK - shift + seq, :]
        cs[CHUNK:, :] = cs[CHUNK:, :] + jnp.where(row_in_chunk >= shift, shifted, 0.0)
        shift *= 2
    gc_seq = cs[CHUNK:, :]
    g3 = gc_seq.reshape(seq // CHUNK, CHUNK, LANES)
    gl_s[...] = jnp.broadcast_to(g3[:, CHUNK - 1:CHUNK, :], g3.shape).reshape(seq, LANES)
    gc_t = gc_seq.T
    for s in range(seq // SLAB):
        gct[s] = gc_t[0:HALO, s * SLAB:(s + 1) * SLAB]
    st_ref[...] = jnp.zeros(st_ref.shape, F32)

    lane_st = lax.broadcasted_iota(jnp.int32, (CHUNK, SLAB), 1)
    i_st = lax.broadcasted_iota(jnp.int32, (CHUNK, SLAB), 0)
    j_st = lane_st % CHUNK
    lane_chunk = lane_st // CHUNK
    incl_st = i_st >= j_st
    strict_st = i_st > j_st
    eye_st = (i_st == j_st).astype(F32)
    level = [strict_st & (((i_st ^ j_st) >> l) == 1) for l in range(6)]
    kbd_mask = (lax.broadcasted_iota(jnp.int32, (SLAB, n_c * B_DIM), 0) // CHUNK
                == lax.broadcasted_iota(jnp.int32, (SLAB, n_c * B_DIM), 1) // B_DIM)

    col = lambda arr, h: arr[:, B_HEADS + h:B_HEADS + h + 1]
    hcols = [slice(h * B_DIM, (h + 1) * B_DIM) for h in heads]
    chunk_rows = [slice(c * CHUNK, (c + 1) * CHUNK) for c in range(n_c)]

    def solve_slabs(slabs):
        jobs = [(s, buf, h) for s, buf in slabs for h in heads]
        each = lambda f: [f(n) for n in range(len(jobs))]
        r0 = [pl.multiple_of(s * SLAB, SLAB) for s, _, _ in jobs]
        rows = [pl.ds(r, SLAB) for r in r0]
        head = [h for _, _, h in jobs]
        bg = each(lambda n: bg_ref[rows[n], :])
        gcs = each(lambda n: cs[pl.ds(CHUNK + r0[n], SLAB), :])
        gam = each(lambda n: col(jnp.exp(gcs[n]), head[n]))
        kd = each(lambda n: col(jnp.exp(gl_s[rows[n], :] - gcs[n]), head[n]))
        q = each(lambda n: q_ref[rows[n], hcols[head[n]]])
        k = each(lambda n: k_ref[rows[n], hcols[head[n]]])
        kf = each(lambda n: k[n].astype(F32))
        bk = each(lambda n: bg[n][:, head[n]:head[n] + 1] * kf[n])
        kbd = each(lambda n: jnp.where(kbd_mask, jnp.concatenate([k[n]] * n_c, axis=1), jnp.zeros((), BF16)))
        qk = each(lambda n: _dot_nt(jnp.concatenate([_stack(q[n]), _stack(bk[n].astype(BF16))], axis=0), kbd[n]))
        yield
        decay = each(lambda n: jnp.exp(jnp.where(
            incl_st, _stack_col(col(gcs[n], head[n]), lane_chunk)
            - gct[jobs[n][0]][B_HEADS + head[n]:B_HEADS + head[n] + 1, :], -1e30)))
        pqk = each(lambda n: (qk[n][:CHUNK] * decay[n]).astype(BF16))
        a = each(lambda n: jnp.where(strict_st, qk[n][CHUNK:] * decay[n], 0.0))
        d = each(lambda n: eye_st - jnp.where(level[0], a[n], 0.0))
        for l in range(1, 6):
            t1 = each(lambda n: _dot(jnp.where(level[l], a[n], 0.0).astype(BF16),
                                     _block_diag(d[n].astype(BF16), lane_chunk)))
            yield
            d = each(lambda n: d[n] - _dot(d[n].astype(BF16), _block_diag(t1[n].astype(BF16), lane_chunk)))
            yield
        rhs = each(lambda n: jnp.concatenate(
            [bg[n][:, head[n]:head[n] + 1] * v_ref[rows[n], hcols[head[n]]].astype(F32), gam[n] * bk[n]],
            axis=1).astype(BF16))
        sol16 = each(lambda n: _dot(_block_diag(d[n].astype(BF16), lane_chunk), rhs[n]).astype(BF16))
        yield
        x2 = each(lambda n: _dot(_block_diag(pqk[n], lane_chunk), sol16[n]))
        yield
        for n, (_, buf, h) in enumerate(jobs):
            o0_s[buf, :, hcols[h]] = x2[n][:, :B_DIM]
            qp_s[buf, :, hcols[h]] = (gam[n] * q[n].astype(F32) - x2[n][:, B_DIM:]).astype(BF16)
        kdec = each(lambda n: (kf[n] * kd[n]).astype(BF16))
        for c, cr in enumerate(chunk_rows):
            for n, (_, buf, h) in enumerate(jobs):
                mnt_s[buf, h * n_c + c] = _dot_tn(kdec[n][cr], sol16[n][cr])
            yield

    def scan_slabs(slabs):
        for s, buf in slabs:
            r0 = pl.multiple_of(s * SLAB, SLAB)
            for c, cr in enumerate(chunk_rows):
                crow = pl.ds(r0 + c * CHUNK, CHUNK)
                egl_row = jnp.exp(gl_s[pl.ds(r0 + c * CHUNK, 1), :])
                for h in heads:
                    st = st_ref[h]
                    st16 = st.astype(BF16)
                    o = _dot(qp_s[buf, cr, hcols[h]], st16) + o0_s[buf, cr, hcols[h]]
                    mnt = mnt_s[buf, h * n_c + c]
                    st_ref[h] = (egl_row[:, B_HEADS + h:B_HEADS + h + 1] * st
                                 - _dot(mnt[:, B_DIM:].astype(BF16), st16) + mnt[:, :B_DIM])
                    y = _rms(o, wn_ref[...]) * _silu(z_ref[crow, hcols[h]].astype(F32))
                    o_ref[crow, hcols[h]] = y.astype(BF16)
                yield

    return solve_slabs, scan_slabs


def _attn_kernel(q_ref, k_ref, v_ref, trow_ref, o_ref, bias, s_scr):
    seq = q_ref.shape[0]
    pad = A_LOOKBACK * CHUNK
    attend = _attention_steps(q_ref, k_ref, v_ref, trow_ref, o_ref, bias, s_scr)
    n_short = pad // Q_GROUP
    _weave(attend([(g * Q_GROUP, 0, (g + 1) * Q_GROUP) for g in range(n_short)]))

    def full_group(g, carry):
        q0 = pl.multiple_of(g * Q_GROUP, Q_GROUP)
        _weave(attend([(q0, pl.multiple_of(q0 - pad, Q_GROUP), K_WINDOW)]))
        return carry

    lax.fori_loop(n_short, seq // Q_GROUP, full_group, 0)


def _attention(qkva, trow):
    b, seq, _ = qkva.shape
    assert seq % Q_GROUP == 0 and (A_LOOKBACK * CHUNK) % Q_GROUP == 0 and ROLL_W >= Q_GROUP + K_WINDOW - 1
    blk = lambda j: pl.BlockSpec((None, seq, A_WIDTH), lambda i, j=j: (i, 0, j))
    return pl.pallas_call(
        _attn_kernel,
        grid=(b,),
        in_specs=[blk(0), blk(1), blk(2), _const_spec(trow.shape)],
        out_specs=pl.BlockSpec((None, seq, A_WIDTH), lambda i: (i, 0, 0)),
        out_shape=jax.ShapeDtypeStruct((b, seq, A_WIDTH), BF16),
        scratch_shapes=[pltpu.VMEM((A_HEADS, K_WINDOW, Q_GROUP), F32),
                        pltpu.VMEM((SCORE_BUFFERS, K_WINDOW, Q_GROUP), F32)],
        compiler_params=_params("arbitrary"),
        name="band_attention",
    )(qkva, qkva, qkva, trow)


def _delta_kernel(q_ref, k_ref, v_ref, bg_ref, z_ref, wn_ref, o_ref, cs, gl_s, gct, st_ref, qp_s, o0_s, mnt_s):
    n_slab = q_ref.shape[0] // SLAB
    solve_slabs, scan_slabs = _delta_steps(q_ref, k_ref, v_ref, bg_ref, z_ref, wn_ref, o_ref,
                                           cs, gl_s, gct, st_ref, qp_s, o0_s, mnt_s)
    buffer_of = lambda s: s % (2 * PIPE)

    _weave(solve_slabs([(s, buffer_of(s)) for s in range(PIPE)]))

    def trip(i, carry):
        s0 = i * PIPE
        _weave(scan_slabs([(s0 + j, buffer_of(s0 + j)) for j in range(PIPE)]),
               solve_slabs([(s0 + PIPE + j, buffer_of(s0 + PIPE + j)) for j in range(PIPE)]))
        return carry

    lax.fori_loop(0, n_slab // PIPE - 1, trip, 0)
    _weave(scan_slabs([(s, buffer_of(s)) for s in range(n_slab - PIPE, n_slab)]))


def _deltanet(cqkv, bg, z, w_onorm):
    b, seq, _ = cqkv.shape
    assert (seq // SLAB) % PIPE == 0
    blk = lambda width, j: pl.BlockSpec((None, seq, width), lambda i, j=j: (i, 0, j))
    n_c = SLAB // CHUNK
    return pl.pallas_call(
        _delta_kernel,
        grid=(b,),
        in_specs=[blk(B_WIDTH, 0), blk(B_WIDTH, 1), blk(B_WIDTH, 2), blk(LANES, 0), blk(B_WIDTH, 0),
                  _const_spec((1, B_DIM))],
        out_specs=blk(B_WIDTH, 0),
        out_shape=jax.ShapeDtypeStruct((b, seq, B_WIDTH), BF16),
        scratch_shapes=[pltpu.VMEM((CHUNK + seq, LANES), F32),
                        pltpu.VMEM((seq, LANES), F32),
                        pltpu.VMEM((seq // SLAB, HALO, SLAB), F32),
                        pltpu.VMEM((B_HEADS, B_DIM, B_DIM), F32),
                        pltpu.VMEM((2 * PIPE, SLAB, B_WIDTH), BF16),
                        pltpu.VMEM((2 * PIPE, SLAB, B_WIDTH), F32),
                        pltpu.VMEM((2 * PIPE, B_HEADS * n_c, B_DIM, 2 * B_DIM), F32)],
        compiler_params=_params("arbitrary"),
        name="gated_deltanet",
    )(cqkv, cqkv, cqkv, bg, z, w_onorm)


def _merge_kernel(x_ref, ya_ref, yb_ref, g_ref, wg_ref, wa_ref, wb_ref, wo_ref, o_ref):
    branch_a = _dot(ya_ref[...], wa_ref[...])
    branch_b = _dot(yb_ref[...], wb_ref[...])
    x = x_ref[...]
    h = _rms(x, g_ref[...]).astype(BF16)
    merged = _sigmoid(_dot(h, wg_ref[:, :D_MODEL])) * branch_a
    merged = merged + _sigmoid(_dot(h, wg_ref[:, D_MODEL:])) * branch_b
    o_ref[...] = x + _dot(merged.astype(BF16), wo_ref[...])


def _merge(x2, ya, yb, g_mix, w_gates, w_a, w_b, w_out):
    t = x2.shape[0]
    tm = TOKEN_TILE
    assert t % tm == 0
    row = lambda width: pl.BlockSpec((tm, width), lambda i: (i, 0))
    return pl.pallas_call(
        _merge_kernel,
        grid=(t // tm,),
        in_specs=[row(D_MODEL), row(A_WIDTH), row(B_WIDTH), _const_spec((1, D_MODEL)),
                  _const_spec(w_gates.shape), _const_spec(w_a.shape), _const_spec(w_b.shape),
                  _const_spec(w_out.shape)],
        out_specs=row(D_MODEL),
        out_shape=jax.ShapeDtypeStruct((t, D_MODEL), F32),
        compiler_params=_params("arbitrary"),
        name="gated_merge",
    )(x2, ya, yb, g_mix, w_gates, w_a, w_b, w_out)


def _ffn_kernel(final, x_ref, p_ref, gf_ref, wgu_ref, wd_ref, gp_ref, wpg_ref, wpp_ref, gfin_ref, o_ref, acc):
    x = x_ref[...]
    h = _rms(x, gf_ref[...]).astype(BF16)
    acc[...] = x
    for j in range(D_FF // FF_SLAB):
        gate = _dot(h, wgu_ref[:, j * FF_SLAB:(j + 1) * FF_SLAB])
        up = _dot(h, wgu_ref[:, D_FF + j * FF_SLAB:D_FF + (j + 1) * FF_SLAB])
        act = (_silu(gate) * up).astype(BF16)
        acc[...] += _dot(act, wd_ref[j * FF_SLAB:(j + 1) * FF_SLAB, :])
    x = acc[...]
    hp = _rms(x, gp_ref[...]).astype(BF16)
    ple_gate = _sigmoid(_dot(hp, wpg_ref[...]))
    x = x + ple_gate * _dot(p_ref[...].astype(BF16), wpp_ref[...])
    o_ref[...] = _rms(x, gfin_ref[...]) if final else x


def _ffn(x2, p2, g_ffn, w_gate_up, w_down, g_ple, w_ple_gate, w_ple_proj, g_final, final):
    t = x2.shape[0]
    tm = TOKEN_TILE
    assert t % tm == 0 and D_FF % FF_SLAB == 0
    row = lambda width: pl.BlockSpec((tm, width), lambda i: (i, 0))
    return pl.pallas_call(
        functools.partial(_ffn_kernel, final),
        grid=(t // tm,),
        in_specs=[row(D_MODEL), row(PLE_DIM), _const_spec((1, D_MODEL)),
                  _const_spec(w_gate_up.shape), _const_spec(w_down.shape), _const_spec((1, D_MODEL)),
                  _const_spec(w_ple_gate.shape), _const_spec(w_ple_proj.shape), _const_spec((1, D_MODEL))],
        out_specs=row(D_MODEL),
        out_shape=jax.ShapeDtypeStruct((t, D_MODEL), F32),
        scratch_shapes=[pltpu.VMEM((tm, D_MODEL), F32)],
        compiler_params=_params("arbitrary"),
        name="ffn_ple_final",
    )(x2, p2, g_ffn, w_gate_up, w_down, g_ple, w_ple_gate, w_ple_proj, g_final)


def _layer(final, x2, p2, seq, g_mix, w_in, conv_w, a_log, dt_bias, rel_bias, w_onorm, w_branch_a, w_branch_b,
           w_out, g_ffn, w_gate_up, w_down, g_ple, w_ple_gate, w_ple_proj, g_final):
    t = x2.shape[0]
    b = t // seq
    row = lambda v: v.reshape(1, -1).astype(F32)
    col_scale = jnp.where(jnp.arange(w_in.shape[1]) < A_WIDTH, A_HEAD_DIM ** -0.5 * LOG2E, 1.0).astype(F32)
    w16 = (w_in * col_scale).astype(BF16)
    w_bd = jnp.pad(w16[:, SPLIT_Z:SPLIT_DECAY], ((0, 0), (0, LANES - 2 * B_HEADS)))
    w_in_r = jnp.concatenate([w16[:, :SPLIT_Z], w_bd], axis=1)
    w_gates = w16[:, SPLIT_DECAY:]
    head_pad = lambda v: jnp.pad(v.astype(F32), (B_HEADS, LANES - 2 * B_HEADS)).reshape(1, LANES)

    qkva, cqkv, z, bg = _inproj(x2, row(g_mix), w_in_r, conv_w.astype(F32), head_pad(a_log),
                                head_pad(dt_bias), seq)
    ya = _attention(qkva.reshape(b, seq, SPLIT_A), _toeplitz_row(rel_bias))
    yb = _deltanet(cqkv.reshape(b, seq, B_CONV_CH), bg.reshape(b, seq, LANES),
                   z.reshape(b, seq, B_WIDTH), row(w_onorm))
    x2 = _merge(x2, ya.reshape(t, A_WIDTH), yb.reshape(t, B_WIDTH), row(g_mix), w_gates,
                w_branch_a.astype(BF16), w_branch_b.astype(BF16), w_out.astype(BF16))
    return _ffn(x2, p2, row(g_ffn), w_gate_up.astype(BF16), w_down.astype(BF16), row(g_ple),
                w_ple_gate.astype(BF16), w_ple_proj.astype(BF16), row(g_final), final)


def kernel(x, p, g_mix, w_in, conv_w, a_log, dt_bias, rel_bias, w_onorm, w_branch_a, w_branch_b, w_out,
           g_ffn, w_gate_up, w_down, g_ple, w_ple_gate, w_ple_proj, g_final):
    b, seq, _ = x.shape
    depth = p.shape[0]
    x2 = x.reshape(b * seq, D_MODEL)
    for i in range(depth):
        x2 = _layer(i == depth - 1, x2, p[i].reshape(b * seq, PLE_DIM), seq, g_mix[i], w_in[i], conv_w[i],
                    a_log[i], dt_bias[i], rel_bias[i], w_onorm[i], w_branch_a[i], w_branch_b[i], w_out[i],
                    g_ffn[i], w_gate_up[i], w_down[i], g_ple[i], w_ple_gate[i], w_ple_proj[i], g_final)
    return x2.reshape(b, seq, D_MODEL)
```

```python
import functools

import jax
import jax.numpy as jnp
from jax import lax
from jax.experimental import pallas as pl
from jax.experimental.pallas import tpu as pltpu

D_MODEL = 1024
CHUNK = 64
PLE_DIM = 256
EPS = 1e-6

A_HEADS = 8
A_HEAD_DIM = 64
A_WIDTH = A_HEADS * A_HEAD_DIM
A_LOOKBACK = 8
REL_CLIP = 128
LOG2E = 1.4426950408889634

B_HEADS = 4
B_DIM = 128
B_WIDTH = B_HEADS * B_DIM
CONV_WIDTH = 4
B_CONV_CH = 3 * B_WIDTH

D_FF = 2816

SPLIT_A = 3 * A_WIDTH
SPLIT_CONV = SPLIT_A + B_CONV_CH
SPLIT_Z = SPLIT_CONV + B_WIDTH
SPLIT_BETA = SPLIT_Z + B_HEADS
SPLIT_DECAY = SPLIT_BETA + B_HEADS

LANES = 128
MXU_COLS = 256
HALO = 8
TOKEN_TILE = 1024
NORM_ROWS = 128
INPROJ_TILE = 512
Q_GROUP = 256
K_WINDOW = Q_GROUP + A_LOOKBACK * CHUNK
SCORE_BUFFERS = 6
ROLL_W = 1024
SLAB = 256
PIPE = 2
FF_SLAB = 256
VMEM_LIMIT = 56 * 1024 * 1024

F32 = jnp.float32
BF16 = jnp.bfloat16
NT_DIMS = (((1,), (1,)), ((), ()))
TN_DIMS = (((0,), (0,)), ((), ()))


def _dot(a, b):
    return jnp.dot(a, b, preferred_element_type=F32)


def _dot_nt(a, b):
    return lax.dot_general(a, b, NT_DIMS, preferred_element_type=F32)


def _dot_tn(a, b):
    return lax.dot_general(a, b, TN_DIMS, preferred_element_type=F32)


def _rms(x, g):
    return x * lax.rsqrt(jnp.mean(x * x, axis=-1, keepdims=True) + EPS) * g


def _sigmoid(x):
    return 0.5 * jnp.tanh(0.5 * x) + 0.5


def _silu(x):
    h = 0.5 * x
    return h * jnp.tanh(h) + h


def _params(*sem):
    return pltpu.CompilerParams(dimension_semantics=sem, vmem_limit_bytes=VMEM_LIMIT)


def _const_spec(shape):
    nd = len(shape)
    return pl.BlockSpec(shape, lambda *_: (0,) * nd, pipeline_mode=pl.Buffered(1))


def _inproj_kernel(tiles_per_seq, x_ref, g_ref, w_ref, cw_ref, alog_ref, dtb_ref,
                   qkva_ref, cqkv_ref, z_ref, bg_ref, cbuf, hbuf):
    tm = x_ref.shape[0]
    i = pl.program_id(0)

    @pl.when(i % tiles_per_seq == 0)
    def _():
        cbuf[0:HALO, :] = jnp.zeros((HALO, B_CONV_CH), F32)

    @pl.when(i % tiles_per_seq != 0)
    def _():
        cbuf[0:HALO, :] = cbuf[tm:tm + HALO, :]

    for r in range(0, tm, NORM_ROWS):
        hr = _rms(x_ref[r:r + NORM_ROWS, :], g_ref[...]).astype(BF16)
        hbuf[r:r + NORM_ROWS, :] = hr
        cbuf[HALO + r:HALO + r + NORM_ROWS, 0:MXU_COLS] = _dot(hr, w_ref[:, SPLIT_A:SPLIT_A + MXU_COLS])
    h = hbuf[...]

    plain = [(qkva_ref, 0, j) for j in range(SPLIT_A // MXU_COLS)]
    plain += [(z_ref, SPLIT_CONV, j) for j in range(B_WIDTH // MXU_COLS)]

    def plain_slab(out_ref, w_col0, j):
        cols = slice(j * MXU_COLS, (j + 1) * MXU_COLS)
        out_ref[:, cols] = _dot(h, w_ref[:, w_col0 + j * MXU_COLS:w_col0 + (j + 1) * MXU_COLS]).astype(BF16)

    for j in range(B_CONV_CH // MXU_COLS):
        cols = slice(j * MXU_COLS, (j + 1) * MXU_COLS)
        if j > 0:
            cbuf[HALO:, cols] = _dot(h, w_ref[:, SPLIT_A + j * MXU_COLS:SPLIT_A + (j + 1) * MXU_COLS])
        plain_slab(*plain.pop(0))
        xs = cbuf[:, cols]
        w = [0.5 * cw_ref[t:t + 1, cols] for t in range(CONV_WIDTH)]
        xs1 = pltpu.roll(xs, 1, 0)
        hc = (pltpu.roll(w[0] * xs1 + w[1] * xs, 2, 0) + (w[2] * xs1 + w[3] * xs))[HALO:]
        c = hc * jnp.tanh(hc) + hc
        for half in range(MXU_COLS // LANES):
            head = j * (MXU_COLS // LANES) + half
            ch = c[:, half * LANES:(half + 1) * LANES]
            if head < 2 * B_HEADS:
                scale = B_DIM ** -0.5 if head < B_HEADS else 1.0
                ch = ch * (lax.rsqrt(jnp.sum(ch * ch, axis=-1, keepdims=True) + EPS) * scale)
            cqkv_ref[:, head * LANES:(head + 1) * LANES] = ch.astype(BF16)

    raw = _dot(h, w_ref[:, SPLIT_Z:SPLIT_Z + LANES])
    lane = lax.broadcasted_iota(jnp.int32, raw.shape, 1)
    sp_in = raw + dtb_ref[...]
    softplus = jnp.maximum(sp_in, 0.0) + jnp.log1p(jnp.exp(-jnp.abs(sp_in)))
    bg_ref[...] = jnp.where(lane < B_HEADS, _sigmoid(raw), -jnp.exp(alog_ref[...]) * softplus)

    for args in plain:
        plain_slab(*args)


def _inproj(x2, g_mix, w_in_r, conv_w, alog_pad, dtb_pad, seq):
    t = x2.shape[0]
    tm = INPROJ_TILE
    assert seq % tm == 0 and t % seq == 0
    row = lambda width: pl.BlockSpec((tm, width), lambda i: (i, 0))
    return pl.pallas_call(
        functools.partial(_inproj_kernel, seq // tm),
        grid=(t // tm,),
        in_specs=[row(D_MODEL), _const_spec((1, D_MODEL)), _const_spec(w_in_r.shape),
                  _const_spec((CONV_WIDTH, B_CONV_CH)), _const_spec((1, LANES)), _const_spec((1, LANES))],
        out_specs=[row(SPLIT_A), row(B_CONV_CH), row(B_WIDTH), row(LANES)],
        out_shape=[jax.ShapeDtypeStruct((t, SPLIT_A), BF16), jax.ShapeDtypeStruct((t, B_CONV_CH), BF16),
                   jax.ShapeDtypeStruct((t, B_WIDTH), BF16), jax.ShapeDtypeStruct((t, LANES), F32)],
        scratch_shapes=[pltpu.VMEM((tm + HALO, B_CONV_CH), F32),
                        pltpu.VMEM((tm, D_MODEL), BF16)],
        compiler_params=_params("arbitrary"),
        name="inproj",
    )(x2, g_mix, w_in_r, conv_w, alog_pad, dtb_pad)


def _toeplitz_row(rel_bias):
    n_far = K_WINDOW - 1 - REL_CLIP
    n_near = ROLL_W - n_far - (2 * REL_CLIP + 1)
    far = jnp.broadcast_to(rel_bias[:, -1:], (A_HEADS, n_far))
    near = jnp.broadcast_to(rel_bias[:, :1], (A_HEADS, n_near))
    t = jnp.concatenate([far, rel_bias[:, ::-1], near], axis=1).astype(F32)
    return t.reshape(A_HEADS, 1, ROLL_W)


def _stack(x):
    return jnp.concatenate([x[c * CHUNK:(c + 1) * CHUNK] for c in range(SLAB // CHUNK)], axis=1)


def _stack_col(col, lane_chunk):
    out = jnp.broadcast_to(col[0:CHUNK], (CHUNK, SLAB))
    for c in range(1, SLAB // CHUNK):
        out = jnp.where(lane_chunk == c, jnp.broadcast_to(col[c * CHUNK:(c + 1) * CHUNK], (CHUNK, SLAB)), out)
    return out


def _block_diag(x_st, lane_chunk):
    zero = jnp.zeros_like(x_st)
    return jnp.concatenate([jnp.where(lane_chunk == c, x_st, zero) for c in range(SLAB // CHUNK)], axis=0)


def _weave(*generators):
    live = list(generators)
    while live:
        live = [g for g in live if next(g, live) is not live]


def _attention_steps(q_ref, k_ref, v_ref, trow_ref, o_ref, bias_t, s_scr):
    lane = lax.broadcasted_iota(jnp.int32, (Q_GROUP, LANES), 1)

    @pl.when(pl.program_id(0) == 0)
    def _():
        qc = lax.broadcasted_iota(jnp.int32, (Q_GROUP, K_WINDOW), 0) // CHUNK
        kc = lax.broadcasted_iota(jnp.int32, (Q_GROUP, K_WINDOW), 1) // CHUNK
        in_band = (kc >= qc) & (kc <= qc + A_LOOKBACK)
        for h in range(A_HEADS):
            base = jnp.broadcast_to(trow_ref[h], (Q_GROUP, ROLL_W))
            toeplitz = pltpu.roll(base, ROLL_W - (Q_GROUP - 1), 1, stride=1, stride_axis=0)
            bias_t[h] = jnp.where(in_band, toeplitz[:, :K_WINDOW] * LOG2E, -1e30).T

    pair_cols = lambda head: slice(head // 2 * LANES, (head // 2 + 1) * LANES)

    def attend(groups):
        for q0, k0, n_keys in groups:
            b0 = K_WINDOW - n_keys
            value_row = lax.broadcasted_iota(jnp.int32, (LANES, n_keys), 0)
            out_row = lax.broadcasted_iota(jnp.int32, (LANES, Q_GROUP), 0)

            def scores(head):
                sub = head % 2
                q2 = q_ref[pl.ds(q0, Q_GROUP), pair_cols(head)]
                head_lanes = (lane >= sub * A_HEAD_DIM) & (lane < (sub + 1) * A_HEAD_DIM)
                qm = jnp.where(head_lanes, q2, jnp.zeros_like(q2))
                s_scr[head % SCORE_BUFFERS, 0:n_keys, :] = _dot_nt(k_ref[pl.ds(k0, n_keys), pair_cols(head)], qm)

            for head in range(SCORE_BUFFERS - 1):
                scores(head)
            outs = []
            for head in range(A_HEADS):
                sub = head % 2
                buf = head % SCORE_BUFFERS
                if head + SCORE_BUFFERS - 1 < A_HEADS:
                    scores(head + SCORE_BUFFERS - 1)
                yield
                biased = s_scr[buf, 0:n_keys, :] + bias_t[head, b0:, :]
                s_scr[buf, 0:n_keys, :] = biased
                m = jnp.max(biased, axis=0, keepdims=True)
                p = jnp.exp2(s_scr[buf, 0:n_keys, :] - m).astype(BF16)
                if sub == 0:
                    v_pair_t = v_ref[pl.ds(k0, n_keys), pair_cols(head)].astype(F32).T
                own = (value_row >= sub * A_HEAD_DIM) & (value_row < (sub + 1) * A_HEAD_DIM)
                o_aug = _dot(jnp.where(own, v_pair_t, 1.0).astype(BF16), p)
                other = (1 - sub) * A_HEAD_DIM
                outs.append(o_aug * (1.0 / o_aug[other:other + 1, :]))
                if sub == 1:
                    pair_t = jnp.where(out_row < A_HEAD_DIM, outs[head - 1], outs[head])
                    o_ref[pl.ds(q0, Q_GROUP), pair_cols(head)] = pair_t.T.astype(BF16)

    return attend


def _delta_steps(q_ref, k_ref, v_ref, bg_ref, z_ref, wn_ref, o_ref, cs, gl_s, gct, st_ref, qp_s, o0_s, mnt_s):
    seq = q_ref.shape[0]
    heads = range(B_HEADS)
    n_c = SLAB // CHUNK

    row_in_chunk = lax.broadcasted_iota(jnp.int32, (seq, LANES), 0) % CHUNK
    cs[0:CHUNK, :] = jnp.zeros((CHUNK, LANES), F32)
    cs[CHUNK:, :] = bg_ref[...]
    shift = 1
    while shift < CHUNK:
        shifted = cs[CHUNK - shift:CHUNK - shift + seq, :]
        cs[CHUNK:, :] = cs[CHUNK:, :] + jnp.where(row_in_chunk >= shift, shifted, 0.0)
        shift *= 2
    gc_seq = cs[CHUNK:, :]
    g3 = gc_seq.reshape(seq // CHUNK, CHUNK, LANES)
    gl_s[...] = jnp.broadcast_to(g3[:, CHUNK - 1:CHUNK, :], g3.shape).reshape(seq, LANES)
    gc_t = gc_seq.T
    for s in range(seq // SLAB):
        gct[s] = gc_t[0:HALO, s * SLAB:(s + 1) * SLAB]
    st_ref[...] = jnp.zeros(st_ref.shape, F32)

    lane_st = lax.broadcasted_iota(jnp.int32, (CHUNK, SLAB), 1)
    i_st = lax.broadcasted_iota(jnp.int32, (CHUNK, SLAB), 0)
    j_st = lane_st % CHUNK
    lane_chunk = lane_st // CHUNK
    incl_st = i_st >= j_st
    strict_st = i_st > j_st
    eye_st = (i_st == j_st).astype(F32)
    level = [strict_st & (((i_st ^ j_st) >> l) == 1) for l in range(6)]
    kbd_mask = (lax.broadcasted_iota(jnp.int32, (SLAB, n_c * B_DIM), 0) // CHUNK
                == lax.broadcasted_iota(jnp.int32, (SLAB, n_c * B_DIM), 1) // B_DIM)

    col = lambda arr, h: arr[:, B_HEADS + h:B_HEADS + h + 1]
    hcols = [slice(h * B_DIM, (h + 1) * B_DIM) for h in heads]
    chunk_rows = [slice(c * CHUNK, (c + 1) * CHUNK) for c in range(n_c)]

    def solve_slabs(slabs):
        jobs = [(s, buf, h) for s, buf in slabs for h in heads]
        each = lambda f: [f(n) for n in range(len(jobs))]
        r0 = [pl.multiple_of(s * SLAB, SLAB) for s, _, _ in jobs]
        rows = [pl.ds(r, SLAB) for r in r0]
        head = [h for _, _, h in jobs]
        bg = each(lambda n: bg_ref[rows[n], :])
        gcs = each(lambda n: cs[pl.ds(CHUNK + r0[n], SLAB), :])
        gam = each(lambda n: col(jnp.exp(gcs[n]), head[n]))
        kd = each(lambda n: col(jnp.exp(gl_s[rows[n], :] - gcs[n]), head[n]))
        q = each(lambda n: q_ref[rows[n], hcols[head[n]]])
        k = each(lambda n: k_ref[rows[n], hcols[head[n]]])
        kf = each(lambda n: k[n].astype(F32))
        bk = each(lambda n: bg[n][:, head[n]:head[n] + 1] * kf[n])
        kbd = each(lambda n: jnp.where(kbd_mask, jnp.concatenate([k[n]] * n_c, axis=1), jnp.zeros((), BF16)))
        qk = each(lambda n: _dot_nt(jnp.concatenate([_stack(q[n]), _stack(bk[n].astype(BF16))], axis=0), kbd[n]))
        yield
        decay = each(lambda n: jnp.exp(jnp.where(
            incl_st, _stack_col(col(gcs[n], head[n]), lane_chunk)
            - gct[jobs[n][0]][B_HEADS + head[n]:B_HEADS + head[n] + 1, :], -1e30)))
        pqk = each(lambda n: (qk[n][:CHUNK] * decay[n]).astype(BF16))
        a = each(lambda n: jnp.where(strict_st, qk[n][CHUNK:] * decay[n], 0.0))
        d = each(lambda n: eye_st - jnp.where(level[0], a[n], 0.0))
        for l in range(1, 6):
            t1 = each(lambda n: _dot(jnp.where(level[l], a[n], 0.0).astype(BF16),
                                     _block_diag(d[n].astype(BF16), lane_chunk)))
            yield
            d = each(lambda n: d[n] - _dot(d[n].astype(BF16), _block_diag(t1[n].astype(BF16), lane_chunk)))
            yield
        rhs = each(lambda n: jnp.concatenate(
            [bg[n][:, head[n]:head[n] + 1] * v_ref[rows[n], hcols[head[n]]].astype(F32), gam[n] * bk[n]],
            axis=1).astype(BF16))
        sol16 = each(lambda n: _dot(_block_diag(d[n].astype(BF16), lane_chunk), rhs[n]).astype(BF16))
        yield
        x2 = each(lambda n: _dot(_block_diag(pqk[n], lane_chunk), sol16[n]))
        yield
        for n, (_, buf, h) in enumerate(jobs):
            o0_s[buf, :, hcols[h]] = x2[n][:, :B_DIM]
            qp_s[buf, :, hcols[h]] = (gam[n] * q[n].astype(F32) - x2[n][:, B_DIM:]).astype(BF16)
        kdec = each(lambda n: (kf[n] * kd[n]).astype(BF16))
        for c, cr in enumerate(chunk_rows):
            for n, (_, buf, h) in enumerate(jobs):
                mnt_s[buf, h * n_c + c] = _dot_tn(kdec[n][cr], sol16[n][cr])
            yield

    def scan_slabs(slabs):
        for s, buf in slabs:
            r0 = pl.multiple_of(s * SLAB, SLAB)
            for c, cr in enumerate(chunk_rows):
                crow = pl.ds(r0 + c * CHUNK, CHUNK)
                egl_row = jnp.exp(gl_s[pl.ds(r0 + c * CHUNK, 1), :])
                for h in heads:
                    st = st_ref[h]
                    st16 = st.astype(BF16)
                    o = _dot(qp_s[buf, cr, hcols[h]], st16) + o0_s[buf, cr, hcols[h]]
                    mnt = mnt_s[buf, h * n_c + c]
                    st_ref[h] = (egl_row[:, B_HEADS + h:B_HEADS + h + 1] * st
                                 - _dot(mnt[:, B_DIM:].astype(BF16), st16) + mnt[:, :B_DIM])
                    y = _rms(o, wn_ref[...]) * _silu(z_ref[crow, hcols[h]].astype(F32))
                    o_ref[crow, hcols[h]] = y.astype(BF16)
                yield

    return solve_slabs, scan_slabs


def _attn_kernel(q_ref, k_ref, v_ref, trow_ref, o_ref, bias, s_scr):
    seq = q_ref.shape[0]
    pad = A_LOOKBACK * CHUNK
    attend = _attention_steps(q_ref, k_ref, v_ref, trow_ref, o_ref, bias, s_scr)
    n_short = pad // Q_GROUP
    _weave(attend([(g * Q_GROUP, 0, (g + 1) * Q_GROUP) for g in range(n_short)]))

    def full_group(g, carry):
        q0 = pl.multiple_of(g * Q_GROUP, Q_GROUP)
        _weave(attend([(q0, pl.multiple_of(q0 - pad, Q_GROUP), K_WINDOW)]))
        return carry

    lax.fori_loop(n_short, seq // Q_GROUP, full_group, 0)


def _attention(qkva, trow):
    b, seq, _ = qkva.shape
    assert seq % Q_GROUP == 0 and (A_LOOKBACK * CHUNK) % Q_GROUP == 0 and ROLL_W >= Q_GROUP + K_WINDOW - 1
    blk = lambda j: pl.BlockSpec((None, seq, A_WIDTH), lambda i, j=j: (i, 0, j))
    return pl.pallas_call(
        _attn_kernel,
        grid=(b,),
        in_specs=[blk(0), blk(1), blk(2), _const_spec(trow.shape)],
        out_specs=pl.BlockSpec((None, seq, A_WIDTH), lambda i: (i, 0, 0)),
        out_shape=jax.ShapeDtypeStruct((b, seq, A_WIDTH), BF16),
        scratch_shapes=[pltpu.VMEM((A_HEADS, K_WINDOW, Q_GROUP), F32),
                        pltpu.VMEM((SCORE_BUFFERS, K_WINDOW, Q_GROUP), F32)],
        compiler_params=_params("arbitrary"),
        name="band_attention",
    )(qkva, qkva, qkva, trow)


def _delta_kernel(q_ref, k_ref, v_ref, bg_ref, z_ref, wn_ref, o_ref, cs, gl_s, gct, st_ref, qp_s, o0_s, mnt_s):
    n_slab = q_ref.shape[0] // SLAB
    solve_slabs, scan_slabs = _delta_steps(q_ref, k_ref, v_ref, bg_ref, z_ref, wn_ref, o_ref,
                                           cs, gl_s, gct, st_ref, qp_s, o0_s, mnt_s)
    buffer_of = lambda s: s % (2 * PIPE)

    _weave(solve_slabs([(s, buffer_of(s)) for s in range(PIPE)]))

    def trip(i, carry):
        s0 = i * PIPE
        _weave(scan_slabs([(s0 + j, buffer_of(s0 + j)) for j in range(PIPE)]),
               solve_slabs([(s0 + PIPE + j, buffer_of(s0 + PIPE + j)) for j in range(PIPE)]))
        return carry

    lax.fori_loop(0, n_slab // PIPE - 1, trip, 0)
    _weave(scan_slabs([(s, buffer_of(s)) for s in range(n_slab - PIPE, n_slab)]))


def _deltanet(cqkv, bg, z, w_onorm):
    b, seq, _ = cqkv.shape
    assert (seq // SLAB) % PIPE == 0
    blk = lambda width, j: pl.BlockSpec((None, seq, width), lambda i, j=j: (i, 0, j))
    n_c = SLAB // CHUNK
    return pl.pallas_call(
        _delta_kernel,
        grid=(b,),
        in_specs=[blk(B_WIDTH, 0), blk(B_WIDTH, 1), blk(B_WIDTH, 2), blk(LANES, 0), blk(B_WIDTH, 0),
                  _const_spec((1, B_DIM))],
        out_specs=blk(B_WIDTH, 0),
        out_shape=jax.ShapeDtypeStruct((b, seq, B_WIDTH), BF16),
        scratch_shapes=[pltpu.VMEM((CHUNK + seq, LANES), F32),
                        pltpu.VMEM((seq, LANES), F32),
                        pltpu.VMEM((seq // SLAB, HALO, SLAB), F32),
                        pltpu.VMEM((B_HEADS, B_DIM, B_DIM), F32),
                        pltpu.VMEM((2 * PIPE, SLAB, B_WIDTH), BF16),
                        pltpu.VMEM((2 * PIPE, SLAB, B_WIDTH), F32),
                        pltpu.VMEM((2 * PIPE, B_HEADS * n_c, B_DIM, 2 * B_DIM), F32)],
        compiler_params=_params("arbitrary"),
        name="gated_deltanet",
    )(cqkv, cqkv, cqkv, bg, z, w_onorm)


def _merge_kernel(x_ref, ya_ref, yb_ref, g_ref, wg_ref, wa_ref, wb_ref, wo_ref, o_ref):
    branch_a = _dot(ya_ref[...], wa_ref[...])
    branch_b = _dot(yb_ref[...], wb_ref[...])
    x = x_ref[...]
    h = _rms(x, g_ref[...]).astype(BF16)
    merged = _sigmoid(_dot(h, wg_ref[:, :D_MODEL])) * branch_a
    merged = merged + _sigmoid(_dot(h, wg_ref[:, D_MODEL:])) * branch_b
    o_ref[...] = x + _dot(merged.astype(BF16), wo_ref[...])


def _merge(x2, ya, yb, g_mix, w_gates, w_a, w_b, w_out):
    t = x2.shape[0]
    tm = TOKEN_TILE
    assert t % tm == 0
    row = lambda width: pl.BlockSpec((tm, width), lambda i: (i, 0))
    return pl.pallas_call(
        _merge_kernel,
        grid=(t // tm,),
        in_specs=[row(D_MODEL), row(A_WIDTH), row(B_WIDTH), _const_spec((1, D_MODEL)),
                  _const_spec(w_gates.shape), _const_spec(w_a.shape), _const_spec(w_b.shape),
                  _const_spec(w_out.shape)],
        out_specs=row(D_MODEL),
        out_shape=jax.ShapeDtypeStruct((t, D_MODEL), F32),
        compiler_params=_params("arbitrary"),
        name="gated_merge",
    )(x2, ya, yb, g_mix, w_gates, w_a, w_b, w_out)


def _ffn_kernel(final, x_ref, p_ref, gf_ref, wgu_ref, wd_ref, gp_ref, wpg_ref, wpp_ref, gfin_ref, o_ref, acc):
    x = x_ref[...]
    h = _rms(x, gf_ref[...]).astype(BF16)
    acc[...] = x
    for j in range(D_FF // FF_SLAB):
        gate = _dot(h, wgu_ref[:, j * FF_SLAB:(j + 1) * FF_SLAB])
        up = _dot(h, wgu_ref[:, D_FF + j * FF_SLAB:D_FF + (j + 1) * FF_SLAB])
        act = (_silu(gate) * up).astype(BF16)
        acc[...] += _dot(act, wd_ref[j * FF_SLAB:(j + 1) * FF_SLAB, :])
    x = acc[...]
    hp = _rms(x, gp_ref[...]).astype(BF16)
    ple_gate = _sigmoid(_dot(hp, wpg_ref[...]))
    x = x + ple_gate * _dot(p_ref[...].astype(BF16), wpp_ref[...])
    o_ref[...] = _rms(x, gfin_ref[...]) if final else x


def _ffn(x2, p2, g_ffn, w_gate_up, w_down, g_ple, w_ple_gate, w_ple_proj, g_final, final):
    t = x2.shape[0]
    tm = TOKEN_TILE
    assert t % tm == 0 and D_FF % FF_SLAB == 0
    row = lambda width: pl.BlockSpec((tm, width), lambda i: (i, 0))
    return pl.pallas_call(
        functools.partial(_ffn_kernel, final),
        grid=(t // tm,),
        in_specs=[row(D_MODEL), row(PLE_DIM), _const_spec((1, D_MODEL)),
                  _const_spec(w_gate_up.shape), _const_spec(w_down.shape), _const_spec((1, D_MODEL)),
                  _const_spec(w_ple_gate.shape), _const_spec(w_ple_proj.shape), _const_spec((1, D_MODEL))],
        out_specs=row(D_MODEL),
        out_shape=jax.ShapeDtypeStruct((t, D_MODEL), F32),
        scratch_shapes=[pltpu.VMEM((tm, D_MODEL), F32)],
        compiler_params=_params("arbitrary"),
        name="ffn_ple_final",
    )(x2, p2, g_ffn, w_gate_up, w_down, g_ple, w_ple_gate, w_ple_proj, g_final)


def _layer(final, x2, p2, seq, g_mix, w_in, conv_w, a_log, dt_bias, rel_bias, w_onorm, w_branch_a, w_branch_b,
           w_out, g_ffn, w_gate_up, w_down, g_ple, w_ple_gate, w_ple_proj, g_final):
    t = x2.shape[0]
    b = t // seq
    row = lambda v: v.reshape(1, -1).astype(F32)
    col_scale = jnp.where(jnp.arange(w_in.shape[1]) < A_WIDTH, A_HEAD_DIM ** -0.5 * LOG2E, 1.0).astype(F32)
    w16 = (w_in * col_scale).astype(BF16)
    w_bd = jnp.pad(w16[:, SPLIT_Z:SPLIT_DECAY], ((0, 0), (0, LANES - 2 * B_HEADS)))
    w_in_r = jnp.concatenate([w16[:, :SPLIT_Z], w_bd], axis=1)
    w_gates = w16[:, SPLIT_DECAY:]
    head_pad = lambda v: jnp.pad(v.astype(F32), (B_HEADS, LANES - 2 * B_HEADS)).reshape(1, LANES)

    qkva, cqkv, z, bg = _inproj(x2, row(g_mix), w_in_r, conv_w.astype(F32), head_pad(a_log),
                                head_pad(dt_bias), seq)
    ya = _attention(qkva.reshape(b, seq, SPLIT_A), _toeplitz_row(rel_bias))
    yb = _deltanet(cqkv.reshape(b, seq, B_CONV_CH), bg.reshape(b, seq, LANES),
                   z.reshape(b, seq, B_WIDTH), row(w_onorm))
    x2 = _merge(x2, ya.reshape(t, A_WIDTH), yb.reshape(t, B_WIDTH), row(g_mix), w_gates,
                w_branch_a.astype(BF16), w_branch_b.astype(BF16), w_out.astype(BF16))
    return _ffn(x2, p2, row(g_ffn), w_gate_up.astype(BF16), w_down.astype(BF16), row(g_ple),
                w_ple_gate.astype(BF16), w_ple_proj.astype(BF16), row(g_final), final)


def kernel(x, p, g_mix, w_in, conv_w, a_log, dt_bias, rel_bias, w_onorm, w_branch_a, w_branch_b, w_out,
           g_ffn, w_gate_up, w_down, g_ple, w_ple_gate, w_ple_proj, g_final):
    b, seq, _ = x.shape
    depth = p.shape[0]
    x2 = x.reshape(b * seq, D_MODEL)
    for i in range(depth):
        x2 = _layer(i == depth - 1, x2, p[i].reshape(b * seq, PLE_DIM), seq, g_mix[i], w_in[i], conv_w[i],
                    a_log[i], dt_bias[i], rel_bias[i], w_onorm[i], w_branch_a[i], w_branch_b[i], w_out[i],
                    g_ffn[i], w_gate_up[i], w_down[i], g_ple[i], w_ple_gate[i], w_ple_proj[i], g_final)
    return x2.reshape(b, seq, D_MODEL)
```

```python
import functools

import jax
import jax.numpy as jnp
from jax import lax
from jax.experimental import pallas as pl
from jax.experimental.pallas import tpu as pltpu

D_MODEL = 1024
CHUNK = 64
PLE_DIM = 256
EPS = 1e-6

A_HEADS = 8
A_HEAD_DIM = 64
A_WIDTH = A_HEADS * A_HEAD_DIM
A_LOOKBACK = 8
REL_CLIP = 128
LOG2E = 1.4426950408889634

B_HEADS = 4
B_DIM = 128
B_WIDTH = B_HEADS * B_DIM
CONV_WIDTH = 4
B_CONV_CH = 3 * B_WIDTH

D_FF = 2816

SPLIT_A = 3 * A_WIDTH
SPLIT_CONV = SPLIT_A + B_CONV_CH
SPLIT_Z = SPLIT_CONV + B_WIDTH
SPLIT_BETA = SPLIT_Z + B_HEADS
SPLIT_DECAY = SPLIT_BETA + B_HEADS

LANES = 128
MXU_COLS = 256
HALO = 8
TOKEN_TILE = 1024
NORM_ROWS = 128
INPROJ_TILE = 512
Q_GROUP = 256
K_WINDOW = Q_GROUP + A_LOOKBACK * CHUNK
SCORE_BUFFERS = 6
ROLL_W = 1024
SLAB = 128
PIPE = 4
FF_SLAB = 256
VMEM_LIMIT = 56 * 1024 * 1024

F32 = jnp.float32
BF16 = jnp.bfloat16
NT_DIMS = (((1,), (1,)), ((), ()))
TN_DIMS = (((0,), (0,)), ((), ()))


def _dot(a, b):
    return jnp.dot(a, b, preferred_element_type=F32)


def _dot_nt(a, b):
    return lax.dot_general(a, b, NT_DIMS, preferred_element_type=F32)


def _dot_tn(a, b):
    return lax.dot_general(a, b, TN_DIMS, preferred_element_type=F32)


def _rms(x, g):
    return x * lax.rsqrt(jnp.mean(x * x, axis=-1, keepdims=True) + EPS) * g


def _sigmoid(x):
    return 0.5 * jnp.tanh(0.5 * x) + 0.5


def _silu(x):
    h = 0.5 * x
    return h * jnp.tanh(h) + h


def _params(*sem):
    return pltpu.CompilerParams(dimension_semantics=sem, vmem_limit_bytes=VMEM_LIMIT)


def _const_spec(shape):
    nd = len(shape)
    return pl.BlockSpec(shape, lambda *_: (0,) * nd, pipeline_mode=pl.Buffered(1))


def _inproj_kernel(tiles_per_seq, x_ref, g_ref, w_ref, cw_ref, alog_ref, dtb_ref,
                   qkva_ref, cqkv_ref, z_ref, bg_ref, cbuf, hbuf):
    tm = x_ref.shape[0]
    i = pl.program_id(0)

    @pl.when(i % tiles_per_seq == 0)
    def _():
        cbuf[0:HALO, :] = jnp.zeros((HALO, B_CONV_CH), F32)

    @pl.when(i % tiles_per_seq != 0)
    def _():
        cbuf[0:HALO, :] = cbuf[tm:tm + HALO, :]

    for r in range(0, tm, NORM_ROWS):
        hr = _rms(x_ref[r:r + NORM_ROWS, :], g_ref[...]).astype(BF16)
        hbuf[r:r + NORM_ROWS, :] = hr
        cbuf[HALO + r:HALO + r + NORM_ROWS, 0:MXU_COLS] = _dot(hr, w_ref[:, SPLIT_A:SPLIT_A + MXU_COLS])
    h = hbuf[...]

    plain = [(qkva_ref, 0, j) for j in range(SPLIT_A // MXU_COLS)]
    plain += [(z_ref, SPLIT_CONV, j) for j in range(B_WIDTH // MXU_COLS)]

    def plain_slab(out_ref, w_col0, j):
        cols = slice(j * MXU_COLS, (j + 1) * MXU_COLS)
        out_ref[:, cols] = _dot(h, w_ref[:, w_col0 + j * MXU_COLS:w_col0 + (j + 1) * MXU_COLS]).astype(BF16)

    for j in range(B_CONV_CH // MXU_COLS):
        cols = slice(j * MXU_COLS, (j + 1) * MXU_COLS)
        if j > 0:
            cbuf[HALO:, cols] = _dot(h, w_ref[:, SPLIT_A + j * MXU_COLS:SPLIT_A + (j + 1) * MXU_COLS])
        plain_slab(*plain.pop(0))
        xs = cbuf[:, cols]
        w = [0.5 * cw_ref[t:t + 1, cols] for t in range(CONV_WIDTH)]
        xs1 = pltpu.roll(xs, 1, 0)
        hc = (pltpu.roll(w[0] * xs1 + w[1] * xs, 2, 0) + (w[2] * xs1 + w[3] * xs))[HALO:]
        c = hc * jnp.tanh(hc) + hc
        for half in range(MXU_COLS // LANES):
            head = j * (MXU_COLS // LANES) + half
            ch = c[:, half * LANES:(half + 1) * LANES]
            if head < 2 * B_HEADS:
                scale = B_DIM ** -0.5 if head < B_HEADS else 1.0
                ch = ch * (lax.rsqrt(jnp.sum(ch * ch, axis=-1, keepdims=True) + EPS) * scale)
            cqkv_ref[:, head * LANES:(head + 1) * LANES] = ch.astype(BF16)

    raw = _dot(h, w_ref[:, SPLIT_Z:SPLIT_Z + LANES])
    lane = lax.broadcasted_iota(jnp.int32, raw.shape, 1)
    sp_in = raw + dtb_ref[...]
    softplus = jnp.maximum(sp_in, 0.0) + jnp.log1p(jnp.exp(-jnp.abs(sp_in)))
    bg_ref[...] = jnp.where(lane < B_HEADS, _sigmoid(raw), -jnp.exp(alog_ref[...]) * softplus)

    for args in plain:
        plain_slab(*args)


def _inproj(x2, g_mix, w_in_r, conv_w, alog_pad, dtb_pad, seq):
    t = x2.shape[0]
    tm = INPROJ_TILE
    assert seq % tm == 0 and t % seq == 0
    row = lambda width: pl.BlockSpec((tm, width), lambda i: (i, 0))
    return pl.pallas_call(
        functools.partial(_inproj_kernel, seq // tm),
        grid=(t // tm,),
        in_specs=[row(D_MODEL), _const_spec((1, D_MODEL)), _const_spec(w_in_r.shape),
                  _const_spec((CONV_WIDTH, B_CONV_CH)), _const_spec((1, LANES)), _const_spec((1, LANES))],
        out_specs=[row(SPLIT_A), row(B_CONV_CH), row(B_WIDTH), row(LANES)],
        out_shape=[jax.ShapeDtypeStruct((t, SPLIT_A), BF16), jax.ShapeDtypeStruct((t, B_CONV_CH), BF16),
                   jax.ShapeDtypeStruct((t, B_WIDTH), BF16), jax.ShapeDtypeStruct((t, LANES), F32)],
        scratch_shapes=[pltpu.VMEM((tm + HALO, B_CONV_CH), F32),
                        pltpu.VMEM((tm, D_MODEL), BF16)],
        compiler_params=_params("arbitrary"),
        name="inproj",
    )(x2, g_mix, w_in_r, conv_w, alog_pad, dtb_pad)


def _toeplitz_row(rel_bias):
    n_far = K_WINDOW - 1 - REL_CLIP
    n_near = ROLL_W - n_far - (2 * REL_CLIP + 1)
    far = jnp.broadcast_to(rel_bias[:, -1:], (A_HEADS, n_far))
    near = jnp.broadcast_to(rel_bias[:, :1], (A_HEADS, n_near))
    t = jnp.concatenate([far, rel_bias[:, ::-1], near], axis=1).astype(F32)
    return t.reshape(A_HEADS, 1, ROLL_W)


def _stack(x):
    return jnp.concatenate([x[c * CHUNK:(c + 1) * CHUNK] for c in range(SLAB // CHUNK)], axis=1)


def _stack_col(col, lane_chunk):
    out = jnp.broadcast_to(col[0:CHUNK], (CHUNK, SLAB))
    for c in range(1, SLAB // CHUNK):
        out = jnp.where(lane_chunk == c, jnp.broadcast_to(col[c * CHUNK:(c + 1) * CHUNK], (CHUNK, SLAB)), out)
    return out


def _block_diag(x_st, lane_chunk):
    zero = jnp.zeros_like(x_st)
    return jnp.concatenate([jnp.where(lane_chunk == c, x_st, zero) for c in range(SLAB // CHUNK)], axis=0)


def _weave(*generators):
    live = list(generators)
    while live:
        live = [g for g in live if next(g, live) is not live]


def _attention_steps(q_ref, k_ref, v_ref, trow_ref, o_ref, bias_t, s_scr):
    lane = lax.broadcasted_iota(jnp.int32, (Q_GROUP, LANES), 1)

    @pl.when(pl.program_id(0) == 0)
    def _():
        qc = lax.broadcasted_iota(jnp.int32, (Q_GROUP, K_WINDOW), 0) // CHUNK
        kc = lax.broadcasted_iota(jnp.int32, (Q_GROUP, K_WINDOW), 1) // CHUNK
        in_band = (kc >= qc) & (kc <= qc + A_LOOKBACK)
        for h in range(A_HEADS):
            base = jnp.broadcast_to(trow_ref[h], (Q_GROUP, ROLL_W))
            toeplitz = pltpu.roll(base, ROLL_W - (Q_GROUP - 1), 1, stride=1, stride_axis=0)
            bias_t[h] = jnp.where(in_band, toeplitz[:, :K_WINDOW] * LOG2E, -1e30).T

    pair_cols = lambda head: slice(head // 2 * LANES, (head // 2 + 1) * LANES)

    def attend(groups):
        for q0, k0, n_keys in groups:
            b0 = K_WINDOW - n_keys
            value_row = lax.broadcasted_iota(jnp.int32, (LANES, n_keys), 0)
            out_row = lax.broadcasted_iota(jnp.int32, (LANES, Q_GROUP), 0)

            def scores(head):
                sub = head % 2
                q2 = q_ref[pl.ds(q0, Q_GROUP), pair_cols(head)]
                head_lanes = (lane >= sub * A_HEAD_DIM) & (lane < (sub + 1) * A_HEAD_DIM)
                qm = jnp.where(head_lanes, q2, jnp.zeros_like(q2))
                s_scr[head % SCORE_BUFFERS, 0:n_keys, :] = _dot_nt(k_ref[pl.ds(k0, n_keys), pair_cols(head)], qm)

            for head in range(SCORE_BUFFERS - 1):
                scores(head)
            outs = []
            for head in range(A_HEADS):
                sub = head % 2
                buf = head % SCORE_BUFFERS
                if head + SCORE_BUFFERS - 1 < A_HEADS:
                    scores(head + SCORE_BUFFERS - 1)
                yield
                biased = s_scr[buf, 0:n_keys, :] + bias_t[head, b0:, :]
                s_scr[buf, 0:n_keys, :] = biased
                m = jnp.max(biased, axis=0, keepdims=True)
                p = jnp.exp2(s_scr[buf, 0:n_keys, :] - m).astype(BF16)
                if sub == 0:
                    v_pair_t = v_ref[pl.ds(k0, n_keys), pair_cols(head)].astype(F32).T
                own = (value_row >= sub * A_HEAD_DIM) & (value_row < (sub + 1) * A_HEAD_DIM)
                o_aug = _dot(jnp.where(own, v_pair_t, 1.0).astype(BF16), p)
                other = (1 - sub) * A_HEAD_DIM
                outs.append(o_aug * (1.0 / o_aug[other:other + 1, :]))
                if sub == 1:
                    pair_t = jnp.where(out_row < A_HEAD_DIM, outs[head - 1], outs[head])
                    o_ref[pl.ds(q0, Q_GROUP), pair_cols(head)] = pair_t.T.astype(BF16)

    return attend


def _delta_steps(q_ref, k_ref, v_ref, bg_ref, z_ref, wn_ref, o_ref, cs, gl_s, gct, st_ref, qp_s, o0_s, mnt_s):
    seq = q_ref.shape[0]
    heads = range(B_HEADS)
    n_c = SLAB // CHUNK

    row_in_chunk = lax.broadcasted_iota(jnp.int32, (seq, LANES), 0) % CHUNK
    cs[0:CHUNK, :] = jnp.zeros((CHUNK, LANES), F32)
    cs[CHUNK:, :] = bg_ref[...]
    shift = 1
    while shift < CHUNK:
        shifted = cs[CHUNK - shift:CHUNK - shift + seq, :]
        cs[CHUNK:, :] = cs[CHUNK:, :] + jnp.where(row_in_chunk >= shift, shifted, 0.0)
        shift *= 2
    gc_seq = cs[CHUNK:, :]
    g3 = gc_seq.reshape(seq // CHUNK, CHUNK, LANES)
    gl_s[...] = jnp.broadcast_to(g3[:, CHUNK - 1:CHUNK, :], g3.shape).reshape(seq, LANES)
    gc_t = gc_seq.T
    for s in range(seq // SLAB):
        gct[s] = gc_t[0:HALO, s * SLAB:(s + 1) * SLAB]
    st_ref[...] = jnp.zeros(st_ref.shape, F32)

    lane_st = lax.broadcasted_iota(jnp.int32, (CHUNK, SLAB), 1)
    i_st = lax.broadcasted_iota(jnp.int32, (CHUNK, SLAB), 0)
    j_st = lane_st % CHUNK
    lane_chunk = lane_st // CHUNK
    incl_st = i_st >= j_st
    strict_st = i_st > j_st
    eye_st = (i_st == j_st).astype(F32)
    level = [strict_st & (((i_st ^ j_st) >> l) == 1) for l in range(6)]
    kbd_mask = (lax.broadcasted_iota(jnp.int32, (SLAB, n_c * B_DIM), 0) // CHUNK
                == lax.broadcasted_iota(jnp.int32, (SLAB, n_c * B_DIM), 1) // B_DIM)

    col = lambda arr, h: arr[:, B_HEADS + h:B_HEADS + h + 1]
    hcols = [slice(h * B_DIM, (h + 1) * B_DIM) for h in heads]
    chunk_rows = [slice(c * CHUNK, (c + 1) * CHUNK) for c in range(n_c)]

    def solve_slabs(slabs):
        jobs = [(s, buf, h) for s, buf in slabs for h in heads]
        each = lambda f: [f(n) for n in range(len(jobs))]
        r0 = [pl.multiple_of(s * SLAB, SLAB) for s, _, _ in jobs]
        rows = [pl.ds(r, SLAB) for r in r0]
        head = [h for _, _, h in jobs]
        bg = each(lambda n: bg_ref[rows[n], :])
        gcs = each(lambda n: cs[pl.ds(CHUNK + r0[n], SLAB), :])
        gam = each(lambda n: col(jnp.exp(gcs[n]), head[n]))
        kd = each(lambda n: col(jnp.exp(gl_s[rows[n], :] - gcs[n]), head[n]))
        q = each(lambda n: q_ref[rows[n], hcols[head[n]]])
        k = each(lambda n: k_ref[rows[n], hcols[head[n]]])
        kf = each(lambda n: k[n].astype(F32))
        bk = each(lambda n: bg[n][:, head[n]:head[n] + 1] * kf[n])
        kbd = each(lambda n: jnp.where(kbd_mask, jnp.concatenate([k[n]] * n_c, axis=1), jnp.zeros((), BF16)))
        qk = each(lambda n: _dot_nt(jnp.concatenate([_stack(q[n]), _stack(bk[n].astype(BF16))], axis=0), kbd[n]))
        yield
        decay = each(lambda n: jnp.exp(jnp.where(
            incl_st, _stack_col(col(gcs[n], head[n]), lane_chunk)
            - gct[jobs[n][0]][B_HEADS + head[n]:B_HEADS + head[n] + 1, :], -1e30)))
        pqk = each(lambda n: (qk[n][:CHUNK] * decay[n]).astype(BF16))
        a = each(lambda n: jnp.where(strict_st, qk[n][CHUNK:] * decay[n], 0.0))
        d = each(lambda n: eye_st - jnp.where(level[0], a[n], 0.0))
        for l in range(1, 6):
            t1 = each(lambda n: _dot(jnp.where(level[l], a[n], 0.0).astype(BF16),
                                     _block_diag(d[n].astype(BF16), lane_chunk)))
            yield
            d = each(lambda n: d[n] - _dot(d[n].astype(BF16), _block_diag(t1[n].astype(BF16), lane_chunk)))
            yield
        rhs = each(lambda n: jnp.concatenate(
            [bg[n][:, head[n]:head[n] + 1] * v_ref[rows[n], hcols[head[n]]].astype(F32), gam[n] * bk[n]],
            axis=1).astype(BF16))
        sol16 = each(lambda n: _dot(_block_diag(d[n].astype(BF16), lane_chunk), rhs[n]).astype(BF16))
        yield
        x2 = each(lambda n: _dot(_block_diag(pqk[n], lane_chunk), sol16[n]))
        yield
        for n, (_, buf, h) in enumerate(jobs):
            o0_s[buf, :, hcols[h]] = x2[n][:, :B_DIM]
            qp_s[buf, :, hcols[h]] = (gam[n] * q[n].astype(F32) - x2[n][:, B_DIM:]).astype(BF16)
        kdec = each(lambda n: (kf[n] * kd[n]).astype(BF16))
        for c, cr in enumerate(chunk_rows):
            for n, (_, buf, h) in enumerate(jobs):
                mnt_s[buf, h * n_c + c] = _dot_tn(kdec[n][cr], sol16[n][cr])
            yield

    def scan_slabs(slabs):
        for s, buf in slabs:
            r0 = pl.multiple_of(s * SLAB, SLAB)
            for c, cr in enumerate(chunk_rows):
                crow = pl.ds(r0 + c * CHUNK, CHUNK)
                egl_row = jnp.exp(gl_s[pl.ds(r0 + c * CHUNK, 1), :])
                for h in heads:
                    st = st_ref[h]
                    st16 = st.astype(BF16)
                    o = _dot(qp_s[buf, cr, hcols[h]], st16) + o0_s[buf, cr, hcols[h]]
                    mnt = mnt_s[buf, h * n_c + c]
                    st_ref[h] = (egl_row[:, B_HEADS + h:B_HEADS + h + 1] * st
                                 - _dot(mnt[:, B_DIM:].astype(BF16), st16) + mnt[:, :B_DIM])
                    y = _rms(o, wn_ref[...]) * _silu(z_ref[crow, hcols[h]].astype(F32))
                    o_ref[crow, hcols[h]] = y.astype(BF16)
                yield

    return solve_slabs, scan_slabs


def _attn_kernel(q_ref, k_ref, v_ref, trow_ref, o_ref, bias, s_scr):
    seq = q_ref.shape[0]
    pad = A_LOOKBACK * CHUNK
    attend = _attention_steps(q_ref, k_ref, v_ref, trow_ref, o_ref, bias, s_scr)
    n_short = pad // Q_GROUP
    _weave(attend([(g * Q_GROUP, 0, (g + 1) * Q_GROUP) for g in range(n_short)]))

    def full_group(g, carry):
        q0 = pl.multiple_of(g * Q_GROUP, Q_GROUP)
        _weave(attend([(q0, pl.multiple_of(q0 - pad, Q_GROUP), K_WINDOW)]))
        return carry

    lax.fori_loop(n_short, seq // Q_GROUP, full_group, 0)


def _attention(qkva, trow):
    b, seq, _ = qkva.shape
    assert seq % Q_GROUP == 0 and (A_LOOKBACK * CHUNK) % Q_GROUP == 0 and ROLL_W >= Q_GROUP + K_WINDOW - 1
    blk = lambda j: pl.BlockSpec((None, seq, A_WIDTH), lambda i, j=j: (i, 0, j))
    return pl.pallas_call(
        _attn_kernel,
        grid=(b,),
        in_specs=[blk(0), blk(1), blk(2), _const_spec(trow.shape)],
        out_specs=pl.BlockSpec((None, seq, A_WIDTH), lambda i: (i, 0, 0)),
        out_shape=jax.ShapeDtypeStruct((b, seq, A_WIDTH), BF16),
        scratch_shapes=[pltpu.VMEM((A_HEADS, K_WINDOW, Q_GROUP), F32),
                        pltpu.VMEM((SCORE_BUFFERS, K_WINDOW, Q_GROUP), F32)],
        compiler_params=_params("arbitrary"),
        name="band_attention",
    )(qkva, qkva, qkva, trow)


def _delta_kernel(q_ref, k_ref, v_ref, bg_ref, z_ref, wn_ref, o_ref, cs, gl_s, gct, st_ref, qp_s, o0_s, mnt_s):
    n_slab = q_ref.shape[0] // SLAB
    solve_slabs, scan_slabs = _delta_steps(q_ref, k_ref, v_ref, bg_ref, z_ref, wn_ref, o_ref,
                                           cs, gl_s, gct, st_ref, qp_s, o0_s, mnt_s)
    buffer_of = lambda s: s % (2 * PIPE)

    _weave(solve_slabs([(s, buffer_of(s)) for s in range(PIPE)]))

    def trip(i, carry):
        s0 = i * PIPE
        _weave(scan_slabs([(s0 + j, buffer_of(s0 + j)) for j in range(PIPE)]),
               solve_slabs([(s0 + PIPE + j, buffer_of(s0 + PIPE + j)) for j in range(PIPE)]))
        return carry

    lax.fori_loop(0, n_slab // PIPE - 1, trip, 0)
    _weave(scan_slabs([(s, buffer_of(s)) for s in range(n_slab - PIPE, n_slab)]))


def _deltanet(cqkv, bg, z, w_onorm):
    b, seq, _ = cqkv.shape
    assert (seq // SLAB) % PIPE == 0
    blk = lambda width, j: pl.BlockSpec((None, seq, width), lambda i, j=j: (i, 0, j))
    n_c = SLAB // CHUNK
    return pl.pallas_call(
        _delta_kernel,
        grid=(b,),
        in_specs=[blk(B_WIDTH, 0), blk(B_WIDTH, 1), blk(B_WIDTH, 2), blk(LANES, 0), blk(B_WIDTH, 0),
                  _const_spec((1, B_DIM))],
        out_specs=blk(B_WIDTH, 0),
        out_shape=jax.ShapeDtypeStruct((b, seq, B_WIDTH), BF16),
        scratch_shapes=[pltpu.VMEM((CHUNK + seq, LANES), F32),
                        pltpu.VMEM((seq, LANES), F32),
                        pltpu.VMEM((seq // SLAB, HALO, SLAB), F32),
                        pltpu.VMEM((B_HEADS, B_DIM, B_DIM), F32),
                        pltpu.VMEM((2 * PIPE, SLAB, B_WIDTH), BF16),
                        pltpu.VMEM((2 * PIPE, SLAB, B_WIDTH), F32),
                        pltpu.VMEM((2 * PIPE, B_HEADS * n_c, B_DIM, 2 * B_DIM), F32)],
        compiler_params=_params("arbitrary"),
        name="gated_deltanet",
    )(cqkv, cqkv, cqkv, bg, z, w_onorm)


def _merge_kernel(x_ref, ya_ref, yb_ref, g_ref, wg_ref, wa_ref, wb_ref, wo_ref, o_ref):
    branch_a = _dot(ya_ref[...], wa_ref[...])
    branch_b = _dot(yb_ref[...], wb_ref[...])
    x = x_ref[...]
    h = _rms(x, g_ref[...]).astype(BF16)
    merged = _sigmoid(_dot(h, wg_ref[:, :D_MODEL])) * branch_a
    merged = merged + _sigmoid(_dot(h, wg_ref[:, D_MODEL:])) * branch_b
    o_ref[...] = x + _dot(merged.astype(BF16), wo_ref[...])


def _merge(x2, ya, yb, g_mix, w_gates, w_a, w_b, w_out):
    t = x2.shape[0]
    tm = TOKEN_TILE
    assert t % tm == 0
    row = lambda width: pl.BlockSpec((tm, width), lambda i: (i, 0))
    return pl.pallas_call(
        _merge_kernel,
        grid=(t // tm,),
        in_specs=[row(D_MODEL), row(A_WIDTH), row(B_WIDTH), _const_spec((1, D_MODEL)),
                  _const_spec(w_gates.shape), _const_spec(w_a.shape), _const_spec(w_b.shape),
                  _const_spec(w_out.shape)],
        out_specs=row(D_MODEL),
        out_shape=jax.ShapeDtypeStruct((t, D_MODEL), F32),
        compiler_params=_params("arbitrary"),
        name="gated_merge",
    )(x2, ya, yb, g_mix, w_gates, w_a, w_b, w_out)


def _ffn_kernel(final, x_ref, p_ref, gf_ref, wgu_ref, wd_ref, gp_ref, wpg_ref, wpp_ref, gfin_ref, o_ref, acc):
    x = x_ref[...]
    h = _rms(x, gf_ref[...]).astype(BF16)
    acc[...] = x
    for j in range(D_FF // FF_SLAB):
        gate = _dot(h, wgu_ref[:, j * FF_SLAB:(j + 1) * FF_SLAB])
        up = _dot(h, wgu_ref[:, D_FF + j * FF_SLAB:D_FF + (j + 1) * FF_SLAB])
        act = (_silu(gate) * up).astype(BF16)
        acc[...] += _dot(act, wd_ref[j * FF_SLAB:(j + 1) * FF_SLAB, :])
    x = acc[...]
    hp = _rms(x, gp_ref[...]).astype(BF16)
    ple_gate = _sigmoid(_dot(hp, wpg_ref[...]))
    x = x + ple_gate * _dot(p_ref[...].astype(BF16), wpp_ref[...])
    o_ref[...] = _rms(x, gfin_ref[...]) if final else x


def _ffn(x2, p2, g_ffn, w_gate_up, w_down, g_ple, w_ple_gate, w_ple_proj, g_final, final):
    t = x2.shape[0]
    tm = TOKEN_TILE
    assert t % tm == 0 and D_FF % FF_SLAB == 0
    row = lambda width: pl.BlockSpec((tm, width), lambda i: (i, 0))
    return pl.pallas_call(
        functools.partial(_ffn_kernel, final),
        grid=(t // tm,),
        in_specs=[row(D_MODEL), row(PLE_DIM), _const_spec((1, D_MODEL)),
                  _const_spec(w_gate_up.shape), _const_spec(w_down.shape), _const_spec((1, D_MODEL)),
                  _const_spec(w_ple_gate.shape), _const_spec(w_ple_proj.shape), _const_spec((1, D_MODEL))],
        out_specs=row(D_MODEL),
        out_shape=jax.ShapeDtypeStruct((t, D_MODEL), F32),
        scratch_shapes=[pltpu.VMEM((tm, D_MODEL), F32)],
        compiler_params=_params("arbitrary"),
        name="ffn_ple_final",
    )(x2, p2, g_ffn, w_gate_up, w_down, g_ple, w_ple_gate, w_ple_proj, g_final)


def _layer(final, x2, p2, seq, g_mix, w_in, conv_w, a_log, dt_bias, rel_bias, w_onorm, w_branch_a, w_branch_b,
           w_out, g_ffn, w_gate_up, w_down, g_ple, w_ple_gate, w_ple_proj, g_final):
    t = x2.shape[0]
    b = t // seq
    row = lambda v: v.reshape(1, -1).astype(F32)
    col_scale = jnp.where(jnp.arange(w_in.shape[1]) < A_WIDTH, A_HEAD_DIM ** -0.5 * LOG2E, 1.0).astype(F32)
    w16 = (w_in * col_scale).astype(BF16)
    w_bd = jnp.pad(w16[:, SPLIT_Z:SPLIT_DECAY], ((0, 0), (0, LANES - 2 * B_HEADS)))
    w_in_r = jnp.concatenate([w16[:, :SPLIT_Z], w_bd], axis=1)
    w_gates = w16[:, SPLIT_DECAY:]
    head_pad = lambda v: jnp.pad(v.astype(F32), (B_HEADS, LANES - 2 * B_HEADS)).reshape(1, LANES)

    qkva, cqkv, z, bg = _inproj(x2, row(g_mix), w_in_r, conv_w.astype(F32), head_pad(a_log),
                                head_pad(dt_bias), seq)
    ya = _attention(qkva.reshape(b, seq, SPLIT_A), _toeplitz_row(rel_bias))
    yb = _deltanet(cqkv.reshape(b, seq, B_CONV_CH), bg.reshape(b, seq, LANES),
                   z.reshape(b, seq, B_WIDTH), row(w_onorm))
    x2 = _merge(x2, ya.reshape(t, A_WIDTH), yb.reshape(t, B_WIDTH), row(g_mix), w_gates,
                w_branch_a.astype(BF16), w_branch_b.astype(BF16), w_out.astype(BF16))
    return _ffn(x2, p2, row(g_ffn), w_gate_up.astype(BF16), w_down.astype(BF16), row(g_ple),
                w_ple_gate.astype(BF16), w_ple_proj.astype(BF16), row(g_final), final)


def kernel(x, p, g_mix, w_in, conv_w, a_log, dt_bias, rel_bias, w_onorm, w_branch_a, w_branch_b, w_out,
           g_ffn, w_gate_up, w_down, g_ple, w_ple_gate, w_ple_proj, g_final):
    b, seq, _ = x.shape
    depth = p.shape[0]
    x2 = x.reshape(b * seq, D_MODEL)
    for i in range(depth):
        x2 = _layer(i == depth - 1, x2, p[i].reshape(b * seq, PLE_DIM), seq, g_mix[i], w_in[i], conv_w[i],
                    a_log[i], dt_bias[i], rel_bias[i], w_onorm[i], w_branch_a[i], w_branch_b[i], w_out[i],
                    g_ffn[i], w_gate_up[i], w_down[i], g_ple[i], w_ple_gate[i], w_ple_proj[i], g_final)
    return x2.reshape(b, seq, D_MODEL)
```
